```python
import jax, jax.numpy as jnp
from jax import lax
import numpy as np

D_MODEL = 4096
BATCH = 4
SEQ = 2048
DEPTH = 2
DEC_BATCH = 8
DEC_SEQ = 16
PAST_LEN = 4096

CHUNK = 64
EPS = 1e-6
NEG_BIG = -1e30
LB_FLOOR = 1e-30
A_HEADS = 8
A_DK = 128
A_DV = 128
A_WIDTH = A_HEADS * A_DV
B_HEADS = 32
B_KV_HEADS = 4
B_GROUP = B_HEADS // B_KV_HEADS
B_HD = 64
B_WIDTH = B_HEADS * B_HD
B_SCALE = B_HD ** -0.5
WINDOW = 128
WINDOW_CHUNKS = WINDOW // CHUNK
ROPE_THETA = 10000.0
C_HEADS = 8
C_DK = 64
C_DV = 128
C_WIDTH = C_HEADS * C_DV
GLA_RANK = 16
GLA_NORMALIZER = 16.0
MIX_WIDTH = A_WIDTH + B_WIDTH + C_WIDTH
D_FF = ((8 * D_MODEL + 3 * 256 - 1) // (3 * 256)) * 256
IN_SPLITS = (A_HEADS * A_DK, A_HEADS * A_DK, A_WIDTH, A_WIDTH,
             B_WIDTH, B_KV_HEADS * B_HD, B_KV_HEADS * B_HD,
             C_HEADS * C_DK, C_HEADS * C_DK, C_WIDTH, C_WIDTH, GLA_RANK)
IN_COLS = sum(IN_SPLITS)

kernel_name = 'hymba_hgrn2_swasink_gla_stream_step'


def rmsnorm(x, g):
    xf = x.astype(jnp.float32)
    y = xf * lax.rsqrt(jnp.mean(xf * xf, axis=-1, keepdims=True) + EPS)
    return (y * g.astype(jnp.float32)).astype(x.dtype)


def rope(x, pos):
    half = x.shape[-1] // 2
    inv = ROPE_THETA ** (-jnp.arange(half, dtype=jnp.float32) / half)
    ang = pos.astype(jnp.float32)[:, None] * inv[None, :]
    cos = jnp.cos(ang)[:, None, :]
    sin = jnp.sin(ang)[:, None, :]
    xf = x.astype(jnp.float32)
    x1, x2 = xf[..., :half], xf[..., half:]
    return jnp.concatenate([x1 * cos - x2 * sin, x2 * cos + x1 * sin], axis=-1).astype(x.dtype)


def split_cols(proj):
    idx, acc = [], 0
    for s in IN_SPLITS[:-1]:
        acc += s
        idx.append(acc)
    return jnp.split(proj, idx, axis=-1)


def gated_linear_recurrence(q, k, v, log_f, s0, block):
    b, t, h, dk = q.shape
    dv = v.shape[-1]
    nb = t // block

    def to_blocks(a):
        return jnp.moveaxis(a.reshape(b, nb, block, h, a.shape[-1]), 1, 0)

    qb, kb, vb, gb = to_blocks(q), to_blocks(k), to_blocks(v), to_blocks(log_f)
    causal = jnp.tril(jnp.ones((block, block), dtype=bool))[None, :, :, None, None]

    def step(s, inp):
        qc, kc, vc, gc = [a.astype(jnp.float32) for a in inp]
        cum = jnp.cumsum(gc, axis=1)
        diff = cum[:, :, None] - cum[:, None, :]
        decay = jnp.where(causal, jnp.exp(jnp.where(causal, diff, 0.0)), 0.0)
        attn = jnp.einsum('bthd,bshd,btshd->bhts', qc, kc, decay)
        o_intra = jnp.einsum('bhts,bshv->bthv', attn, vc)
        o_inter = jnp.einsum('bthd,bhdv->bthv', qc * jnp.exp(cum), s)
        last = cum[:, -1]
        k_dec = kc * jnp.exp(last[:, None] - cum)
        s_new = jnp.exp(last)[..., None] * s + jnp.einsum('bshd,bshv->bhdv', k_dec, vc)
        return s_new, o_intra + o_inter

    s_final, o = lax.scan(step, s0.astype(jnp.float32), (qb, kb, vb, gb))
    o = jnp.moveaxis(o, 0, 1).reshape(b, t, h, dv)
    return o, s_final


def hgrn2_mixer(a_q, a_f, a_i, a_g, lb, norm_g, s0, block):
    b, t, _ = a_q.shape
    z = a_f.astype(jnp.float32)
    log_f = jnp.logaddexp(jnp.log(jnp.maximum(lb, LB_FLOOR)), jnp.log1p(-lb) + jax.nn.log_sigmoid(z))
    k_in = (1.0 - lb) * jax.nn.sigmoid(-z)
    shp = (b, t, A_HEADS, A_DK)
    o, s_new = gated_linear_recurrence(a_q.reshape(shp), k_in.reshape(shp),
                                       a_i.reshape(b, t, A_HEADS, A_DV), log_f.reshape(shp), s0, block)
    o = rmsnorm(o, norm_g) * jax.nn.silu(a_g.astype(jnp.float32)).reshape(b, t, A_HEADS, A_DV)
    return o.reshape(b, t, A_WIDTH).astype(a_q.dtype), s_new


def gla_mixer(c_q, c_k, c_v, c_r, c_a, w_a2, b_a, norm_g, s0, block):
    b, t, _ = c_q.shape
    log_alpha = jax.nn.log_sigmoid((c_a @ w_a2 + b_a).astype(jnp.float32)) / GLA_NORMALIZER
    shp = (b, t, C_HEADS, C_DK)
    o, s_new = gated_linear_recurrence(c_q.reshape(shp) * (C_DK ** -0.5), c_k.reshape(shp),
                                       c_v.reshape(b, t, C_HEADS, C_DV), log_alpha.reshape(shp), s0, block)
    o = rmsnorm(o, norm_g) * jax.nn.silu(c_r.astype(jnp.float32)).reshape(b, t, C_HEADS, C_DV)
    return o.reshape(b, t, C_WIDTH).astype(c_q.dtype), s_new


def sink_softmax(scores, sinks):
    sink = jnp.broadcast_to(sinks.astype(jnp.float32).reshape(B_KV_HEADS, B_GROUP, 1, 1),
                            scores.shape[:-1] + (1,))
    return jax.nn.softmax(jnp.concatenate([scores, sink], axis=-1), axis=-1)[..., :-1]


def swa_prompt(q, k, v, sinks):
    b, t = q.shape[:2]
    nc = t // CHUNK
    band = (WINDOW_CHUNKS + 1) * CHUNK
    qc = q.reshape(b, nc, CHUNK, B_KV_HEADS, B_GROUP, B_HD)

    def banded(a):
        ap = jnp.pad(a, ((0, 0), (WINDOW_CHUNKS * CHUNK, 0), (0, 0), (0, 0)))
        ap = ap.reshape(b, nc + WINDOW_CHUNKS, CHUNK, B_KV_HEADS, B_HD)
        return jnp.concatenate([ap[:, j:j + nc] for j in range(WINDOW_CHUNKS + 1)], axis=2)

    kb, vb = banded(k), banded(v)
    valid = (jnp.arange(nc)[:, None] + jnp.arange(band)[None, :] // CHUNK) >= WINDOW_CHUNKS
    scores = jnp.einsum('bcqkgd,bcskd->bckgqs', qc, kb).astype(jnp.float32) * B_SCALE
    scores = jnp.where(valid[None, :, None, None, None, :], scores, NEG_BIG)
    p = sink_softmax(scores, sinks).astype(v.dtype)
    o = jnp.einsum('bckgqs,bcskd->bcqkgd', p, vb)
    return o.reshape(b, t, B_WIDTH)


def swa_sample(q, k, v, cache_k, cache_v, sinks):
    b, t = q.shape[:2]
    qs = q.reshape(b, t, B_KV_HEADS, B_GROUP, B_HD)
    ka = jnp.concatenate([cache_k.astype(k.dtype), k], axis=1)
    va = jnp.concatenate([cache_v.astype(v.dtype), v], axis=1)
    scores = jnp.einsum('bqkgd,bskd->bkgqs', qs, ka).astype(jnp.float32) * B_SCALE
    p = sink_softmax(scores, sinks).astype(v.dtype)
    o = jnp.einsum('bkgqs,bskd->bqkgd', p, va)
    return o.reshape(b, t, B_WIDTH)


def hybrid_layer(x, pos, block, kv_cache, s_a, s_c, lb, norm_mix, w_in, hgrn_norm, sinks,
                 w_a2, b_a, gla_norm, w_out, norm_ffn, w_gate_up, w_down):
    b, t, _ = x.shape
    h = rmsnorm(x, norm_mix)
    a_q, a_f, a_i, a_g, b_q, b_k, b_v, c_q, c_k, c_v, c_r, c_a = split_cols(h @ w_in)
    o_a, s_a_new = hgrn2_mixer(a_q, a_f, a_i, a_g, lb, hgrn_norm, s_a, block)
    q = rope(b_q.reshape(b, t, B_HEADS, B_HD), pos)
    k = rope(b_k.reshape(b, t, B_KV_HEADS, B_HD), pos)
    v = b_v.reshape(b, t, B_KV_HEADS, B_HD)
    if kv_cache is None:
        o_b = swa_prompt(q, k, v, sinks)
        k_rows, v_rows = k[:, -WINDOW:], v[:, -WINDOW:]
    else:
        o_b = swa_sample(q, k, v, kv_cache[0], kv_cache[1], sinks)
        k_rows, v_rows = k, v
    o_c, s_c_new = gla_mixer(c_q, c_k, c_v, c_r, c_a, w_a2, b_a, gla_norm, s_c, block)
    x = x + jnp.concatenate([o_a, o_b, o_c], axis=-1) @ w_out
    h = rmsnorm(x, norm_ffn)
    gate, up = jnp.split(h @ w_gate_up, 2, axis=-1)
    x = x + (jax.nn.silu(gate) * up) @ w_down
    return x, k_rows, v_rows, s_a_new, s_c_new


def setup_inputs(seed: int = 0) -> dict:
    key = jax.random.key(seed)
    ks = jax.random.split(key, 20)
    f32 = jnp.float32

    def nrm(k, shape, scale):
        return jax.random.normal(k, shape, dtype=f32) * scale

    cache_rows = min(WINDOW, PAST_LEN)
    return {
        'x_prompt': nrm(ks[0], (BATCH, SEQ, D_MODEL), 1.0),
        'x_sample': nrm(ks[1], (DEC_BATCH, DEC_SEQ, D_MODEL), 1.0),
        'cache_k_swa': nrm(ks[2], (DEPTH, DEC_BATCH, cache_rows, B_KV_HEADS, B_HD), 1.0),
        'cache_v_swa': nrm(ks[3], (DEPTH, DEC_BATCH, cache_rows, B_KV_HEADS, B_HD), 1.0),
        'state_hgrn': nrm(ks[4], (DEPTH, DEC_BATCH, A_HEADS, A_DK, A_DV), 0.5),
        'state_gla': nrm(ks[5], (DEPTH, DEC_BATCH, C_HEADS, C_DK, C_DV), 0.5),
        'norm_mix': 1.0 + nrm(ks[6], (DEPTH, D_MODEL), 0.02),
        'w_in': nrm(ks[7], (DEPTH, D_MODEL, IN_COLS), D_MODEL ** -0.5),
        'hgrn_lb_logits': nrm(ks[8], (DEPTH, A_HEADS * A_DK), 0.5),
        'hgrn_norm': 1.0 + nrm(ks[9], (DEPTH, A_DV), 0.02),
        'swa_sinks': nrm(ks[10], (DEPTH, B_HEADS), 0.5),
        'gla_w_alpha2': nrm(ks[11], (DEPTH, GLA_RANK, C_HEADS * C_DK), GLA_RANK ** -0.5),
        'gla_b_alpha': nrm(ks[12], (DEPTH, C_HEADS * C_DK), 0.1),
        'gla_norm': 1.0 + nrm(ks[13], (DEPTH, C_DV), 0.02),
        'w_out': nrm(ks[14], (DEPTH, MIX_WIDTH, D_MODEL), MIX_WIDTH ** -0.5),
        'norm_ffn': 1.0 + nrm(ks[15], (DEPTH, D_MODEL), 0.02),
        'w_gate_up': nrm(ks[16], (DEPTH, D_MODEL, 2 * D_FF), D_MODEL ** -0.5),
        'w_down': nrm(ks[17], (DEPTH, D_FF, D_MODEL), D_FF ** -0.5),
        'norm_final': 1.0 + nrm(ks[18], (D_MODEL,), 0.02),
    }


def reference(x_prompt, x_sample, cache_k_swa, cache_v_swa, state_hgrn, state_gla,
              norm_mix, w_in, hgrn_lb_logits, hgrn_norm, swa_sinks, gla_w_alpha2, gla_b_alpha,
              gla_norm, w_out, norm_ffn, w_gate_up, w_down, norm_final):
    probs = jax.nn.softmax(hgrn_lb_logits.astype(jnp.float32), axis=0)
    lower_bounds = jnp.cumsum(probs, axis=0) - probs[0:1]

    n_p, t_p = x_prompt.shape[0], x_prompt.shape[1]
    n_s, t_s = x_sample.shape[0], x_sample.shape[1]
    pos_p = jnp.arange(t_p)
    pos_s = PAST_LEN + jnp.arange(t_s)
    zeros_a = jnp.zeros((n_p, A_HEADS, A_DK, A_DV), jnp.float32)
    zeros_c = jnp.zeros((n_p, C_HEADS, C_DK, C_DV), jnp.float32)

    xp, xs = x_prompt, x_sample
    kp_l, vp_l, sap_l, scp_l = [], [], [], []
    ks_l, vs_l, sas_l, scs_l = [], [], [], []
    for l in range(DEPTH):
        w = (lower_bounds[l], norm_mix[l], w_in[l], hgrn_norm[l], swa_sinks[l],
             gla_w_alpha2[l], gla_b_alpha[l], gla_norm[l], w_out[l], norm_ffn[l], w_gate_up[l], w_down[l])
        xp, kr, vr, sa, sc = hybrid_layer(xp, pos_p, CHUNK, None, zeros_a, zeros_c, *w)
        kp_l.append(kr); vp_l.append(vr); sap_l.append(sa); scp_l.append(sc)
        xs, kr, vr, sa, sc = hybrid_layer(xs, pos_s, t_s, (cache_k_swa[l], cache_v_swa[l]),
                                          state_hgrn[l], state_gla[l], *w)
        ks_l.append(kr); vs_l.append(vr); sas_l.append(sa); scs_l.append(sc)

    y_prompt = rmsnorm(xp, norm_final)
    y_sample = rmsnorm(xs, norm_final)
    return (y_prompt, y_sample,
            jnp.stack(kp_l), jnp.stack(vp_l), jnp.stack(sap_l), jnp.stack(scp_l),
            jnp.stack(ks_l), jnp.stack(vs_l), jnp.stack(sas_l), jnp.stack(scs_l))
```

```python
import functools

import jax
import jax.numpy as jnp
import numpy as np
from jax import lax
from jax.experimental import pallas as pl
from jax.experimental.pallas import tpu as pltpu

F32 = jnp.float32
BF16 = jnp.bfloat16

DEPTH = 2
PAST_LEN = 4096
CHUNK = 64
EPS = 1e-6
NEG_BIG = -1e30
LB_FLOOR = 1e-30
A_HEADS = 8
A_DK = 128
A_DV = 128
A_WIDTH = A_HEADS * A_DV
B_HEADS = 32
B_KV_HEADS = 4
B_GROUP = B_HEADS // B_KV_HEADS
B_HD = 64
B_WIDTH = B_HEADS * B_HD
B_SCALE = B_HD ** -0.5
WINDOW = 128
WINDOW_CHUNKS = WINDOW // CHUNK
ROPE_THETA = 10000.0
C_HEADS = 8
C_DK = 64
C_DV = 128
C_WIDTH = C_HEADS * C_DV
GLA_RANK = 16
GLA_NORMALIZER = 16.0

LANES = 128

COL_A_Q = 0
COL_A_F = 8
COL_A_I = 16
COL_A_G = 24
COL_B_Q = 32
COL_B_K = 48
COL_B_V = 50
COL_C_Q = 52
COL_C_K = 56
COL_C_V = 60
COL_C_R = 68
COL_C_A = 76
IN_COLS = 9744
IN_COLS_PAD = 9984
MM_TN_IN = 768
VMEM_LIMIT = 56 * 1024 * 1024


def _cparams(sem):
    return pltpu.CompilerParams(dimension_semantics=sem, vmem_limit_bytes=VMEM_LIMIT)


def _dot(a, b):
    return jnp.dot(a, b, preferred_element_type=F32)


def _dot_nt(a, b):
    return lax.dot_general(a, b, (((1,), (1,)), ((), ())), preferred_element_type=F32)


def _dot_tn(a, b):
    return lax.dot_general(a, b, (((0,), (0,)), ((), ())), preferred_element_type=F32)


def _rmsnorm_kernel(x_ref, g_ref, o_ref):
    x = x_ref[...]
    var = jnp.mean(x * x, axis=-1, keepdims=True)
    o_ref[...] = (x * lax.rsqrt(var + EPS) * g_ref[...]).astype(o_ref.dtype)


def _rmsnorm(x, g, out_dtype, tm):
    m, d = x.shape
    return pl.pallas_call(
        _rmsnorm_kernel,
        grid=(m // tm,),
        in_specs=[pl.BlockSpec((tm, d), lambda i: (i, 0)),
                  pl.BlockSpec((1, d), lambda i: (0, 0))],
        out_specs=pl.BlockSpec((tm, d), lambda i: (i, 0)),
        out_shape=jax.ShapeDtypeStruct((m, d), out_dtype),
        compiler_params=_cparams(("parallel",)),
        name="rmsnorm",
    )(x, g.reshape(1, d))


def _mm_kernel(x_ref, w_ref, o_ref):
    o_ref[...] = _dot(x_ref[...], w_ref[...]).astype(o_ref.dtype)


def _matmul(x, w, tm, tn, out_dtype):
    m, k = x.shape
    n = w.shape[1]
    return pl.pallas_call(
        _mm_kernel,
        grid=(m // tm, n // tn),
        in_specs=[pl.BlockSpec((tm, k), lambda i, j: (i, 0)),
                  pl.BlockSpec((k, tn), lambda i, j: (0, j))],
        out_specs=pl.BlockSpec((tm, tn), lambda i, j: (i, j)),
        out_shape=jax.ShapeDtypeStruct((m, n), out_dtype),
        compiler_params=_cparams(("parallel", "arbitrary")),
        name="mm_in",
    )(x, w)


def _mm_out_kernel(oa_ref, ob_ref, oc_ref, w_ref, r_ref, o_ref, cat_ref):
    @pl.when(pl.program_id(1) == 0)
    def _():
        cat_ref[:, 0:A_WIDTH] = oa_ref[...]
        cat_ref[:, A_WIDTH:A_WIDTH + B_WIDTH] = ob_ref[...]
        cat_ref[:, A_WIDTH + B_WIDTH:] = oc_ref[...]

    o_ref[...] = r_ref[...] + _dot(cat_ref[...], w_ref[...])


def _matmul_out(oa, ob, oc, w, resid, tm, tn):
    m = oa.shape[0]
    k, n = w.shape
    return pl.pallas_call(
        _mm_out_kernel,
        grid=(m // tm, n // tn),
        in_specs=[pl.BlockSpec((tm, A_WIDTH), lambda i, j: (i, 0)),
                  pl.BlockSpec((tm, B_WIDTH), lambda i, j: (i, 0)),
                  pl.BlockSpec((tm, C_WIDTH), lambda i, j: (i, 0)),
                  pl.BlockSpec((k, tn), lambda i, j: (0, j)),
                  pl.BlockSpec((tm, tn), lambda i, j: (i, j))],
        out_specs=pl.BlockSpec((tm, tn), lambda i, j: (i, j)),
        out_shape=jax.ShapeDtypeStruct((m, n), F32),
        scratch_shapes=[pltpu.VMEM((tm, k), BF16)],
        compiler_params=_cparams(("parallel", "arbitrary")),
        name="mm_out",
    )(oa, ob, oc, w, resid)


def _mm_gate_up_kernel(x_ref, wg_ref, wu_ref, o_ref):
    x = x_ref[...]
    gate = _dot(x, wg_ref[...])
    up = _dot(x, wu_ref[...])
    o_ref[...] = (gate * (1.0 / (1.0 + jnp.exp(-gate))) * up).astype(o_ref.dtype)


def _matmul_gate_up(x, wg, wu, tm, tn):
    m, k = x.shape
    n = wg.shape[1]
    return pl.pallas_call(
        _mm_gate_up_kernel,
        grid=(m // tm, n // tn),
        in_specs=[pl.BlockSpec((tm, k), lambda i, j: (i, 0)),
                  pl.BlockSpec((k, tn), lambda i, j: (0, j)),
                  pl.BlockSpec((k, tn), lambda i, j: (0, j))],
        out_specs=pl.BlockSpec((tm, tn), lambda i, j: (i, j)),
        out_shape=jax.ShapeDtypeStruct((m, n), BF16),
        compiler_params=_cparams(("parallel", "arbitrary")),
        name="mm_gate_up",
    )(x, wg, wu)


def _mm_down_kernel(x_ref, w_ref, r_ref, o_ref):
    kk = pl.program_id(2)

    @pl.when(kk == 0)
    def _():
        o_ref[...] = r_ref[...] + _dot(x_ref[...], w_ref[...])

    @pl.when(kk != 0)
    def _():
        o_ref[...] += _dot(x_ref[...], w_ref[...])


def _matmul_down(x, w, resid, tm, tn, tk):
    m, k = x.shape
    n = w.shape[1]
    return pl.pallas_call(
        _mm_down_kernel,
        grid=(m // tm, n // tn, k // tk),
        in_specs=[pl.BlockSpec((tm, tk), lambda i, j, kk: (i, kk)),
                  pl.BlockSpec((tk, tn), lambda i, j, kk: (kk, j)),
                  pl.BlockSpec((tm, tn), lambda i, j, kk: (i, j))],
        out_specs=pl.BlockSpec((tm, tn), lambda i, j, kk: (i, j)),
        out_shape=jax.ShapeDtypeStruct((m, n), F32),
        compiler_params=_cparams(("parallel", "parallel", "arbitrary")),
        name="mm_down",
    )(x, w, resid)


def _levels(c, d0):
    out, b = [], d0
    while b < c:
        out.append(b)
        b *= 2
    return out


def _prefix_matrix(c, d0):
    t = np.arange(c)[:, None]
    s = np.arange(c)[None, :]
    low = s <= t
    mats = [low, ~low]
    for b in _levels(c, d0):
        same = (t // b) == (s // b)
        mats.append(low & same)
        mats.append((~low) & same)
    return jnp.asarray(np.concatenate(mats, axis=0).astype(np.float32), dtype=BF16)


def _glr_consts(c, d0, hl):
    t = lax.broadcasted_iota(jnp.int32, (c, c), 0)
    s = lax.broadcasted_iota(jnp.int32, (c, c), 1)
    lvl_masks = []
    for b in _levels(c, d0):
        lg = b.bit_length() - 1
        tb = lax.shift_right_logical(t, lg)
        sb = lax.shift_right_logical(s, lg)
        m = ((lax.shift_right_logical(tb, 1) == lax.shift_right_logical(sb, 1))
             & ((tb & 1) == 1) & ((sb & 1) == 0))
        lvl_masks.append(m)
    lg0 = d0.bit_length() - 1
    dm = s - lax.shift_left(lax.shift_right_logical(t, lg0), lg0)
    tmod = t & (d0 - 1)
    diag_valid = (dm >= 0) & (dm <= tmod)
    if hl == 2:
        lvl_masks = [jnp.concatenate([m, m], axis=0) for m in lvl_masks]
        dm = jnp.concatenate([dm, dm], axis=0)
        diag_valid = jnp.concatenate([diag_valid, diag_valid], axis=0)
    lane = lax.broadcasted_iota(jnp.int32, (1, LANES), 1)
    head_masks = [(lane < C_DK).astype(F32), (lane >= C_DK).astype(F32)]
    return dict(lvl_masks=lvl_masks, dm=dm, diag_valid=diag_valid,
                head_masks=head_masks, lane=lane)


def _glr_core(q, k, lf, vs, st, pm, cst, *, c, d0, hl):
    levels = _levels(c, d0)
    hm = cst["head_masks"]
    hi = lf.astype(BF16)
    r1 = lf - hi.astype(F32)
    mid = r1.astype(BF16)
    lo = (r1 - mid.astype(F32)).astype(BF16)
    pref = _dot(pm, hi) + _dot(pm, mid) + _dot(pm, lo)
    g_all = pref[0:c]
    r_all = pref[c:2 * c]
    qhat = q * jnp.exp(g_all)
    kdec = (k * jnp.exp(r_all)).astype(BF16)

    a = jnp.zeros((hl * c, c), F32)
    gin0 = None
    for i, b in enumerate(levels):
        gin = pref[(2 + 2 * i) * c:(3 + 2 * i) * c]
        rin = pref[(3 + 2 * i) * c:(4 + 2 * i) * c]
        if i == 0:
            gin0 = gin
        qb = q * jnp.exp(gin)
        kb = (k * jnp.exp(rin)).astype(BF16)
        if hl == 2:
            qs = jnp.concatenate([qb * hm[0], qb * hm[1]], axis=0)
        else:
            qs = qb
        a = jnp.where(cst["lvl_masks"][i], _dot_nt(qs.astype(BF16), kb), a)

    nb = c // d0
    k3 = k.reshape(nb, d0, LANES)
    g3 = gin0.reshape(nb, d0, LANES)
    ad = jnp.zeros((hl * c, c), F32)
    for sp in range(d0):
        kb = jnp.broadcast_to(k3[:, sp:sp + 1, :], (nb, d0, LANES)).reshape(c, LANES)
        gb = jnp.broadcast_to(g3[:, sp:sp + 1, :], (nb, d0, LANES)).reshape(c, LANES)
        p = q * kb * jnp.exp(jnp.minimum(gin0 - gb, 0.0))
        if hl == 2:
            col = jnp.concatenate([jnp.sum(p * hm[0], axis=-1, keepdims=True),
                                   jnp.sum(p * hm[1], axis=-1, keepdims=True)], axis=0)
        else:
            col = jnp.sum(p, axis=-1, keepdims=True)
        ad = jnp.where(cst["dm"] == sp, col, ad)
    a = jnp.where(cst["diag_valid"], ad, a).astype(BF16)

    stb = st.astype(BF16)
    outs, upd = [], []
    for h in range(hl):
        qh = qhat * hm[h] if hl == 2 else qhat
        vb = vs[h].astype(BF16)
        outs.append(_dot(a[h * c:(h + 1) * c], vb) + _dot_nt(qh.astype(BF16), stb))
        upd.append(_dot_tn(vb, kdec))
    u = jnp.where(cst["lane"] < C_DK, upd[0], upd[1]) if hl == 2 else upd[0]
    st_new = st * jnp.exp(g_all[c - 1:c, :]) + u
    return outs, st_new


def _gated_norm(o, norm, gate):
    var = jnp.mean(o * o, axis=-1, keepdims=True)
    return o * lax.rsqrt(var + EPS) * norm * (gate * (1.0 / (1.0 + jnp.exp(-gate))))


def _hgrn_kernel(*refs, c, d0, nchunks, layer, has_s0):
    if has_s0:
        pm_ref, lbl_ref, norm_ref, q_ref, z_ref, v_ref, g_ref, s0_ref, o_ref, s_ref = refs
    else:
        pm_ref, lbl_ref, norm_ref, q_ref, z_ref, v_ref, g_ref, o_ref, s_ref = refs
    logits = lbl_ref[...]
    e = jnp.exp(logits - jnp.max(logits, axis=0, keepdims=True))
    probs = e / jnp.sum(e, axis=0, keepdims=True)
    lb = jnp.sum(probs[0:layer + 1], axis=0, keepdims=True) - probs[0:1]
    a0 = jnp.log(jnp.maximum(lb, LB_FLOOR))
    b0 = jnp.log1p(-lb)
    oml = 1.0 - lb
    norm = norm_ref[...]
    pm = pm_ref[...]
    cst = _glr_consts(c, d0, 1)

    def body(n, st):
        r0 = pl.multiple_of(n * c, c)
        q = q_ref[pl.ds(r0, c), :]
        z = z_ref[pl.ds(r0, c), :]
        v = v_ref[pl.ds(r0, c), :]
        g = g_ref[pl.ds(r0, c), :]
        ez = jnp.exp(-jnp.abs(z))
        ls = jnp.minimum(z, 0.0) - jnp.log1p(ez)
        y = b0 + ls
        lf = jnp.maximum(a0, y) + jnp.log1p(jnp.exp(-jnp.abs(a0 - y)))
        r = 1.0 / (1.0 + ez)
        k = oml * jnp.where(z >= 0.0, ez * r, r)
        outs, st2 = _glr_core(q, k, lf, [v], st, pm, cst, c=c, d0=d0, hl=1)
        o_ref[pl.ds(r0, c), :] = _gated_norm(outs[0], norm, g).astype(o_ref.dtype)
        return st2

    if has_s0:
        st0 = s0_ref[0, 0].T
    else:
        st0 = jnp.zeros((A_DV, A_DK), F32)
    st = lax.fori_loop(0, nchunks, body, st0)
    s_ref[0, 0] = st.T


def _hgrn(proj, lbl, norm, s0, layer, nb, t, c, d0):
    pm = _prefix_matrix(c, d0)
    nchunks = t // c
    has_s0 = s0 is not None

    def col(off):
        return pl.BlockSpec((t, LANES), lambda b, h: (b, off + h))

    in_specs = [pl.BlockSpec(pm.shape, lambda b, h: (0, 0)),
                pl.BlockSpec((DEPTH, LANES), lambda b, h: (0, h)),
                pl.BlockSpec((1, LANES), lambda b, h: (0, 0)),
                col(COL_A_Q), col(COL_A_F), col(COL_A_I), col(COL_A_G)]
    args = [pm, lbl, norm.reshape(1, LANES), proj, proj, proj, proj]
    if has_s0:
        in_specs.append(pl.BlockSpec((1, 1, A_DK, A_DV), lambda b, h: (b, h, 0, 0)))
        args.append(s0)
    return pl.pallas_call(
        functools.partial(_hgrn_kernel, c=c, d0=d0, nchunks=nchunks, layer=layer, has_s0=has_s0),
        grid=(nb, A_HEADS),
        in_specs=in_specs,
        out_specs=[pl.BlockSpec((t, LANES), lambda b, h: (b, h)),
                   pl.BlockSpec((1, 1, A_DK, A_DV), lambda b, h: (b, h, 0, 0))],
        out_shape=[jax.ShapeDtypeStruct((nb * t, A_WIDTH), BF16),
                   jax.ShapeDtypeStruct((nb, A_HEADS, A_DK, A_DV), F32)],
        compiler_params=_cparams(("parallel", "parallel")),
        name="hgrn",
    )(*args)


def _gla_kernel(*refs, c, d0, nchunks, has_s0):
    if has_s0:
        (pm_ref, wa_ref, ba_ref, norm_ref, q_ref, k_ref, v_ref, r_ref, ca_ref,
         s0_ref, o_ref, s_ref) = refs
    else:
        (pm_ref, wa_ref, ba_ref, norm_ref, q_ref, k_ref, v_ref, r_ref, ca_ref,
         o_ref, s_ref) = refs
    wa = wa_ref[...]
    ba = ba_ref[...]
    norm = norm_ref[...]
    pm = pm_ref[...]
    cst = _glr_consts(c, d0, 2)

    def body(n, st):
        r0 = pl.multiple_of(n * c, c)
        q = q_ref[pl.ds(r0, c), :] * (C_DK ** -0.5)
        k = k_ref[pl.ds(r0, c), :]
        v = v_ref[pl.ds(r0, c), :]
        r = r_ref[pl.ds(r0, c), :]
        x = _dot(ca_ref[pl.ds(r0, c), :].astype(BF16), wa) + ba
        lf = (jnp.minimum(x, 0.0) - jnp.log1p(jnp.exp(-jnp.abs(x)))) * (1.0 / GLA_NORMALIZER)
        vs = [v[:, 0:C_DV], v[:, C_DV:2 * C_DV]]
        outs, st2 = _glr_core(q, k, lf, vs, st, pm, cst, c=c, d0=d0, hl=2)
        for h in range(2):
            gate = r[:, h * C_DV:(h + 1) * C_DV]
            o_ref[pl.ds(r0, c), h * C_DV:(h + 1) * C_DV] = (
                _gated_norm(outs[h], norm, gate).astype(o_ref.dtype))
        return st2

    if has_s0:
        st0 = s0_ref[0, 0].T
    else:
        st0 = jnp.zeros((C_DV, LANES), F32)
    st = lax.fori_loop(0, nchunks, body, st0)
    s_ref[0, 0] = st.T


def _gla(proj, wa2, ba, norm, s0, nb, t, c, d0):
    pm = _prefix_matrix(c, d0)
    nchunks = t // c
    has_s0 = s0 is not None
    npair = C_HEADS // 2

    def col(off, width=1):
        return pl.BlockSpec((t, width * LANES), lambda b, p: (b, (off // width) + p))

    in_specs = [pl.BlockSpec(pm.shape, lambda b, p: (0, 0)),
                pl.BlockSpec((LANES, LANES), lambda b, p: (0, p)),
                pl.BlockSpec((1, LANES), lambda b, p: (0, p)),
                pl.BlockSpec((1, LANES), lambda b, p: (0, 0)),
                col(COL_C_Q), col(COL_C_K), col(COL_C_V, 2), col(COL_C_R, 2),
                pl.BlockSpec((t, LANES), lambda b, p: (b, COL_C_A))]
    args = [pm, wa2, ba.reshape(1, -1), norm.reshape(1, LANES), proj, proj, proj, proj, proj]
    if has_s0:
        in_specs.append(pl.BlockSpec((1, 1, LANES, C_DV), lambda b, p: (b, p, 0, 0)))
        args.append(s0.reshape(nb, npair, 2 * C_DK, C_DV))
    o, s = pl.pallas_call(
        functools.partial(_gla_kernel, c=c, d0=d0, nchunks=nchunks, has_s0=has_s0),
        grid=(nb, npair),
        in_specs=in_specs,
        out_specs=[pl.BlockSpec((t, 2 * C_DV), lambda b, p: (b, p)),
                   pl.BlockSpec((1, 1, LANES, C_DV), lambda b, p: (b, p, 0, 0))],
        out_shape=[jax.ShapeDtypeStruct((nb * t, C_WIDTH), BF16),
                   jax.ShapeDtypeStruct((nb, npair, 2 * C_DK, C_DV), F32)],
        compiler_params=_cparams(("parallel", "parallel")),
        name="gla",
    )(*args)
    return o, s.reshape(nb, C_HEADS, C_DK, C_DV)


def _rope_tables(pos):
    half = B_HD // 2
    inv = ROPE_THETA ** (-jnp.arange(half, dtype=F32) / half)
    ang = pos.astype(F32)[:, None] * inv[None, :]
    cos = jnp.cos(ang)
    sin = jnp.sin(ang)
    cos_t = jnp.tile(cos, (1, 2 * LANES // B_HD))
    sin_t = jnp.tile(jnp.concatenate([-sin, sin], axis=-1), (1, LANES // B_HD))
    return cos_t, sin_t


def _rope(x, cos_t, sin_t):
    n = x.shape[1]
    half = B_HD // 2
    lane = lax.broadcasted_iota(jnp.int32, (1, n), 1)
    first = (lane & (B_HD - 1)) < half
    rot = jnp.where(first, pltpu.roll(x, n - half, axis=1), pltpu.roll(x, half, axis=1))
    reps = n // LANES
    if reps > 1:
        cos_t = jnp.concatenate([cos_t] * reps, axis=1)
        sin_t = jnp.concatenate([sin_t] * reps, axis=1)
    return x * cos_t + rot * sin_t


def _attend(qs, kb, vb, sink_col, valid):
    s = _dot_nt(qs, kb) * B_SCALE
    if valid is not None:
        s = jnp.where(valid, s, NEG_BIG)
    m = jnp.maximum(jnp.max(s, axis=-1, keepdims=True), sink_col)
    p = jnp.exp(s - m)
    den = jnp.sum(p, axis=-1, keepdims=True) + jnp.exp(sink_col - m)
    return _dot(p.astype(BF16), vb) * (1.0 / den)


def _sink_column(sink_ref, base, t):
    rows = lax.broadcasted_iota(jnp.int32, (B_GROUP * t, 1), 0)
    col = jnp.zeros((B_GROUP * t, 1), F32)
    for g in range(B_GROUP):
        col = jnp.where((rows >= g * t) & (rows < (g + 1) * t), sink_ref[base + g], col)
    return col


def _swa_prompt_kernel(sink_ref, q_ref, k_ref, v_ref, cq_ref, sq_ref, ck_ref, sk_ref,
                       o_ref, kr_ref, vr_ref, kro_ref, *, t, layer):
    khp = pl.program_id(1)
    cidx = pl.program_id(2)

    @pl.when(cidx == 0)
    def _():
        kr = _rope(k_ref[...], ck_ref[...], sk_ref[...])
        kro_ref[...] = kr.astype(BF16)
        kr_ref[0] = kr[t - WINDOW:, :]
        vr_ref[0] = v_ref[t - WINDOW:, :]

    band = (WINDOW_CHUNKS + 1) * CHUNK
    s0 = pl.multiple_of(jnp.maximum(cidx - WINDOW_CHUNKS, 0) * CHUNK, CHUNK)
    qr = _rope(q_ref[...], cq_ref[...], sq_ref[...]).astype(BF16)
    kband = kro_ref[pl.ds(s0, band), :]
    vband = v_ref[pl.ds(s0, band), :].astype(BF16)
    key_pos = s0 + lax.broadcasted_iota(jnp.int32, (1, band), 1)
    valid = key_pos < (cidx + 1) * CHUNK
    for kv in range(2):
        base = kv * B_GROUP * B_HD
        qs = jnp.concatenate(
            [qr[:, base + g * B_HD:base + (g + 1) * B_HD] for g in range(B_GROUP)], axis=0)
        sink_col = _sink_column(sink_ref, layer * B_HEADS + (khp * 2 + kv) * B_GROUP, CHUNK)
        o = _attend(qs, kband[:, kv * B_HD:(kv + 1) * B_HD], vband[:, kv * B_HD:(kv + 1) * B_HD],
                    sink_col, valid)
        for g in range(0, B_GROUP, 2):
            o_ref[:, base + g * B_HD:base + (g + 2) * B_HD] = jnp.concatenate(
                [o[g * CHUNK:(g + 1) * CHUNK], o[(g + 1) * CHUNK:(g + 2) * CHUNK]],
                axis=1).astype(o_ref.dtype)


def _swa_prompt(proj, sinks, layer, nb, t):
    nc = t // CHUNK
    cos_t, sin_t = _rope_tables(jnp.arange(t))
    qw = 2 * B_GROUP * B_HD
    qoff = COL_B_Q * LANES // qw
    kernel = functools.partial(_swa_prompt_kernel, t=t, layer=layer)
    return pl.pallas_call(
        kernel,
        grid=(nb, B_KV_HEADS // 2, nc),
        in_specs=[pl.BlockSpec(memory_space=pltpu.SMEM),
                  pl.BlockSpec((CHUNK, qw), lambda b, p, c: (b * nc + c, qoff + p)),
                  pl.BlockSpec((t, LANES), lambda b, p, c: (b, COL_B_K + p)),
                  pl.BlockSpec((t, LANES), lambda b, p, c: (b, COL_B_V + p)),
                  pl.BlockSpec((CHUNK, LANES), lambda b, p, c: (c, 0)),
                  pl.BlockSpec((CHUNK, LANES), lambda b, p, c: (c, 0)),
                  pl.BlockSpec((t, LANES), lambda b, p, c: (0, 0)),
                  pl.BlockSpec((t, LANES), lambda b, p, c: (0, 0))],
        out_specs=[pl.BlockSpec((CHUNK, qw), lambda b, p, c: (b * nc + c, p)),
                   pl.BlockSpec((1, WINDOW, LANES), lambda b, p, c: (b, 0, p)),
                   pl.BlockSpec((1, WINDOW, LANES), lambda b, p, c: (b, 0, p))],
        out_shape=[jax.ShapeDtypeStruct((nb * t, B_WIDTH), BF16),
                   jax.ShapeDtypeStruct((nb, WINDOW, B_KV_HEADS * B_HD), F32),
                   jax.ShapeDtypeStruct((nb, WINDOW, B_KV_HEADS * B_HD), F32)],
        scratch_shapes=[pltpu.VMEM((t, LANES), BF16)],
        compiler_params=_cparams(("parallel", "parallel", "arbitrary")),
        name="swa_prompt",
    )(sinks.reshape(-1), proj, proj, proj, cos_t, sin_t, cos_t, sin_t)


def _swa_sample_kernel(sink_ref, q_ref, k_ref, v_ref, ckc_ref, cvc_ref, cos_ref, sin_ref,
                       o_ref, kr_ref, vr_ref, *, t, layer):
    cos_t = cos_ref[...]
    sin_t = sin_ref[...]
    kr = _rope(k_ref[...], cos_t, sin_t)
    v = v_ref[...]
    kr_ref[...] = kr
    vr_ref[...] = v
    qr = _rope(q_ref[...], cos_t, sin_t).astype(BF16)
    ka = jnp.concatenate([ckc_ref[0], kr], axis=0).astype(BF16)
    va = jnp.concatenate([cvc_ref[0], v], axis=0).astype(BF16)
    for kh in range(B_KV_HEADS):
        base = kh * B_GROUP * B_HD
        qs = jnp.concatenate(
            [qr[:, base + g * B_HD:base + (g + 1) * B_HD] for g in range(B_GROUP)], axis=0)
        sink_col = _sink_column(sink_ref, layer * B_HEADS + kh * B_GROUP, t)
        o = _attend(qs, ka[:, kh * B_HD:(kh + 1) * B_HD], va[:, kh * B_HD:(kh + 1) * B_HD],
                    sink_col, None)
        for g in range(0, B_GROUP, 2):
            o_ref[:, base + g * B_HD:base + (g + 2) * B_HD] = jnp.concatenate(
                [o[g * t:(g + 1) * t], o[(g + 1) * t:(g + 2) * t]], axis=1).astype(o_ref.dtype)


def _swa_sample(proj, cache_k, cache_v, sinks, layer, nb, t):
    cos_t, sin_t = _rope_tables(PAST_LEN + jnp.arange(t))
    kvw = B_KV_HEADS * B_HD
    kernel = functools.partial(_swa_sample_kernel, t=t, layer=layer)
    return pl.pallas_call(
        kernel,
        grid=(nb,),
        in_specs=[pl.BlockSpec(memory_space=pltpu.SMEM),
                  pl.BlockSpec((t, B_WIDTH), lambda b: (b, COL_B_Q * LANES // B_WIDTH)),
                  pl.BlockSpec((t, kvw), lambda b: (b, COL_B_K * LANES // kvw)),
                  pl.BlockSpec((t, kvw), lambda b: (b, COL_B_V * LANES // kvw)),
                  pl.BlockSpec((1, WINDOW, kvw), lambda b: (b, 0, 0)),
                  pl.BlockSpec((1, WINDOW, kvw), lambda b: (b, 0, 0)),
                  pl.BlockSpec((t, LANES), lambda b: (0, 0)),
                  pl.BlockSpec((t, LANES), lambda b: (0, 0))],
        out_specs=[pl.BlockSpec((t, B_WIDTH), lambda b: (b, 0)),
                   pl.BlockSpec((t, kvw), lambda b: (b, 0)),
                   pl.BlockSpec((t, kvw), lambda b: (b, 0))],
        out_shape=[jax.ShapeDtypeStruct((nb * t, B_WIDTH), BF16),
                   jax.ShapeDtypeStruct((nb * t, kvw), F32),
                   jax.ShapeDtypeStruct((nb * t, kvw), F32)],
        compiler_params=_cparams(("parallel",)),
        name="swa_sample",
    )(sinks.reshape(-1), proj, proj, proj,
      cache_k.reshape(nb, WINDOW, kvw), cache_v.reshape(nb, WINDOW, kvw), cos_t, sin_t)


def _tiles(m):
    return min(m, 1024)


def _layer(x, w, layer, nb, t, cache, s_a, s_c, c, d0):
    m, d = x.shape
    tm = _tiles(m)
    h = _rmsnorm(x, w["norm_mix"][layer], BF16, min(m, 256))
    proj = _matmul(h, w["w_in"][layer], tm, MM_TN_IN, F32)
    o_a, s_a_new = _hgrn(proj, w["lb_logits"], w["hgrn_norm"][layer], s_a, layer, nb, t, c, d0)
    if cache is None:
        o_b, k_rows, v_rows = _swa_prompt(proj, w["sinks"], layer, nb, t)
        k_rows = k_rows.reshape(nb, WINDOW, B_KV_HEADS, B_HD)
        v_rows = v_rows.reshape(nb, WINDOW, B_KV_HEADS, B_HD)
    else:
        o_b, k_rows, v_rows = _swa_sample(proj, cache[0], cache[1], w["sinks"], layer, nb, t)
        k_rows = k_rows.reshape(nb, t, B_KV_HEADS, B_HD)
        v_rows = v_rows.reshape(nb, t, B_KV_HEADS, B_HD)
    o_c, s_c_new = _gla(proj, w["w_a2"][layer], w["b_a"][layer], w["gla_norm"][layer],
                        s_c, nb, t, c, d0)
    x = _matmul_out(o_a, o_b, o_c, w["w_out"][layer], x, tm, min(d, 512))
    h = _rmsnorm(x, w["norm_ffn"][layer], BF16, min(m, 256))
    ff = w["w_gate"].shape[2]
    mid = _matmul_gate_up(h, w["w_gate"][layer], w["w_up"][layer], tm, 512)
    x = _matmul_down(mid, w["w_down"][layer], x, tm, min(d, 1024), ff // 4)
    return x, k_rows, v_rows, s_a_new, s_c_new


def kernel(x_prompt, x_sample, cache_k_swa, cache_v_swa, state_hgrn, state_gla, norm_mix, w_in,
           hgrn_lb_logits, hgrn_norm, swa_sinks, gla_w_alpha2, gla_b_alpha, gla_norm, w_out,
           norm_ffn, w_gate_up, w_down, norm_final):
    n_p, t_p, d = x_prompt.shape
    n_s, t_s, _ = x_sample.shape
    d_ff = w_down.shape[1]
    ff_pad = -(-d_ff // 1024) * 1024
    w = dict(
        norm_mix=norm_mix, norm_ffn=norm_ffn, hgrn_norm=hgrn_norm, gla_norm=gla_norm,
        lb_logits=hgrn_lb_logits, sinks=swa_sinks, b_a=gla_b_alpha,
        w_in=jnp.pad(w_in, ((0, 0), (0, 0), (0, IN_COLS_PAD - IN_COLS))).astype(BF16),
        w_a2=jnp.pad(gla_w_alpha2, ((0, 0), (0, LANES - GLA_RANK), (0, 0))).astype(BF16),
        w_out=w_out.astype(BF16),
        w_gate=jnp.pad(w_gate_up[:, :, :d_ff], ((0, 0), (0, 0), (0, ff_pad - d_ff))).astype(BF16),
        w_up=jnp.pad(w_gate_up[:, :, d_ff:], ((0, 0), (0, 0), (0, ff_pad - d_ff))).astype(BF16),
        w_down=jnp.pad(w_down, ((0, 0), (0, ff_pad - d_ff), (0, 0))).astype(BF16),
    )
    xp = x_prompt.reshape(n_p * t_p, d)
    xs = x_sample.reshape(n_s * t_s, d)
    outs_p, outs_s = [], []
    for layer in range(DEPTH):
        xp, *rest = _layer(xp, w, layer, n_p, t_p, None, None, None, CHUNK, 16)
        outs_p.append(rest)
        xs, *rest = _layer(xs, w, layer, n_s, t_s, (cache_k_swa[layer], cache_v_swa[layer]),
                           state_hgrn[layer], state_gla[layer], t_s, 8)
        outs_s.append(rest)
    y_p = _rmsnorm(xp, norm_final, F32, min(xp.shape[0], 256)).reshape(n_p, t_p, d)
    y_s = _rmsnorm(xs, norm_final, F32, min(xs.shape[0], 256)).reshape(n_s, t_s, d)
    stack = lambda outs, i: jnp.stack([o[i] for o in outs])
    return (y_p, y_s,
            stack(outs_p, 0), stack(outs_p, 1), stack(outs_p, 2), stack(outs_p, 3),
            stack(outs_s, 0), stack(outs_s, 1), stack(outs_s, 2), stack(outs_s, 3))
```

```python
import functools

import jax
import jax.numpy as jnp
import numpy as np
from jax import lax
from jax.experimental import pallas as pl
from jax.experimental.pallas import tpu as pltpu

F32 = jnp.float32
BF16 = jnp.bfloat16

DEPTH = 2
PAST_LEN = 4096
CHUNK = 64
EPS = 1e-6
NEG_BIG = -1e30
LB_FLOOR = 1e-30
A_HEADS = 8
A_DK = 128
A_DV = 128
A_WIDTH = A_HEADS * A_DV
B_HEADS = 32
B_KV_HEADS = 4
B_GROUP = B_HEADS // B_KV_HEADS
B_HD = 64
B_WIDTH = B_HEADS * B_HD
B_SCALE = B_HD ** -0.5
WINDOW = 128
WINDOW_CHUNKS = WINDOW // CHUNK
ROPE_THETA = 10000.0
C_HEADS = 8
C_DK = 64
C_DV = 128
C_WIDTH = C_HEADS * C_DV
GLA_RANK = 16
GLA_NORMALIZER = 16.0

LANES = 128

COL_A_Q = 0
COL_A_F = 8
COL_A_I = 16
COL_A_G = 24
COL_B_Q = 32
COL_B_K = 48
COL_B_V = 50
COL_C_Q = 52
COL_C_K = 56
COL_C_V = 60
COL_C_R = 68
COL_C_A = 76
IN_COLS = 9744
IN_COLS_PAD = 9984
MM_TN_IN = 768
VMEM_LIMIT = 56 * 1024 * 1024


def _cparams(sem):
    return pltpu.CompilerParams(dimension_semantics=sem, vmem_limit_bytes=VMEM_LIMIT)


def _dot(a, b):
    return jnp.dot(a, b, preferred_element_type=F32)


def _dot_nt(a, b):
    return lax.dot_general(a, b, (((1,), (1,)), ((), ())), preferred_element_type=F32)


def _dot_tn(a, b):
    return lax.dot_general(a, b, (((0,), (0,)), ((), ())), preferred_element_type=F32)


def _rmsnorm_kernel(x_ref, g_ref, o_ref):
    x = x_ref[...]
    var = jnp.mean(x * x, axis=-1, keepdims=True)
    o_ref[...] = (x * lax.rsqrt(var + EPS) * g_ref[...]).astype(o_ref.dtype)


def _rmsnorm(x, g, out_dtype, tm):
    m, d = x.shape
    return pl.pallas_call(
        _rmsnorm_kernel,
        grid=(m // tm,),
        in_specs=[pl.BlockSpec((tm, d), lambda i: (i, 0)),
                  pl.BlockSpec((1, d), lambda i: (0, 0))],
        out_specs=pl.BlockSpec((tm, d), lambda i: (i, 0)),
        out_shape=jax.ShapeDtypeStruct((m, d), out_dtype),
        compiler_params=_cparams(("parallel",)),
        name="rmsnorm",
    )(x, g.reshape(1, d))


def _mm_kernel(x_ref, w_ref, o_ref):
    o_ref[...] = _dot(x_ref[...], w_ref[...]).astype(o_ref.dtype)


def _matmul(x, w, tm, tn, out_dtype):
    m, k = x.shape
    n = w.shape[1]
    return pl.pallas_call(
        _mm_kernel,
        grid=(m // tm, n // tn),
        in_specs=[pl.BlockSpec((tm, k), lambda i, j: (i, 0)),
                  pl.BlockSpec((k, tn), lambda i, j: (0, j))],
        out_specs=pl.BlockSpec((tm, tn), lambda i, j: (i, j)),
        out_shape=jax.ShapeDtypeStruct((m, n), out_dtype),
        compiler_params=_cparams(("parallel", "arbitrary")),
        name="mm_in",
    )(x, w)


def _mm_out_kernel(oa_ref, ob_ref, oc_ref, w_ref, r_ref, o_ref, cat_ref):
    @pl.when(pl.program_id(1) == 0)
    def _():
        cat_ref[:, 0:A_WIDTH] = oa_ref[...]
        cat_ref[:, A_WIDTH:A_WIDTH + B_WIDTH] = ob_ref[...]
        cat_ref[:, A_WIDTH + B_WIDTH:] = oc_ref[...]

    o_ref[...] = r_ref[...] + _dot(cat_ref[...], w_ref[...])


def _matmul_out(oa, ob, oc, w, resid, tm, tn):
    m = oa.shape[0]
    k, n = w.shape
    return pl.pallas_call(
        _mm_out_kernel,
        grid=(m // tm, n // tn),
        in_specs=[pl.BlockSpec((tm, A_WIDTH), lambda i, j: (i, 0)),
                  pl.BlockSpec((tm, B_WIDTH), lambda i, j: (i, 0)),
                  pl.BlockSpec((tm, C_WIDTH), lambda i, j: (i, 0)),
                  pl.BlockSpec((k, tn), lambda i, j: (0, j)),
                  pl.BlockSpec((tm, tn), lambda i, j: (i, j))],
        out_specs=pl.BlockSpec((tm, tn), lambda i, j: (i, j)),
        out_shape=jax.ShapeDtypeStruct((m, n), F32),
        scratch_shapes=[pltpu.VMEM((tm, k), BF16)],
        compiler_params=_cparams(("parallel", "arbitrary")),
        name="mm_out",
    )(oa, ob, oc, w, resid)


def _mm_gate_up_kernel(x_ref, wg_ref, wu_ref, o_ref):
    x = x_ref[...]
    gate = _dot(x, wg_ref[...])
    up = _dot(x, wu_ref[...])
    o_ref[...] = (gate * (1.0 / (1.0 + jnp.exp(-gate))) * up).astype(o_ref.dtype)


def _matmul_gate_up(x, wg, wu, tm, tn):
    m, k = x.shape
    n = wg.shape[1]
    return pl.pallas_call(
        _mm_gate_up_kernel,
        grid=(m // tm, n // tn),
        in_specs=[pl.BlockSpec((tm, k), lambda i, j: (i, 0)),
                  pl.BlockSpec((k, tn), lambda i, j: (0, j)),
                  pl.BlockSpec((k, tn), lambda i, j: (0, j))],
        out_specs=pl.BlockSpec((tm, tn), lambda i, j: (i, j)),
        out_shape=jax.ShapeDtypeStruct((m, n), BF16),
        compiler_params=_cparams(("parallel", "arbitrary")),
        name="mm_gate_up",
    )(x, wg, wu)


def _mm_down_kernel(x_ref, w_ref, r_ref, o_ref):
    kk = pl.program_id(2)

    @pl.when(kk == 0)
    def _():
        o_ref[...] = r_ref[...] + _dot(x_ref[...], w_ref[...])

    @pl.when(kk != 0)
    def _():
        o_ref[...] += _dot(x_ref[...], w_ref[...])


def _matmul_down(x, w, resid, tm, tn, tk):
    m, k = x.shape
    n = w.shape[1]
    return pl.pallas_call(
        _mm_down_kernel,
        grid=(m // tm, n // tn, k // tk),
        in_specs=[pl.BlockSpec((tm, tk), lambda i, j, kk: (i, kk)),
                  pl.BlockSpec((tk, tn), lambda i, j, kk: (kk, j)),
                  pl.BlockSpec((tm, tn), lambda i, j, kk: (i, j))],
        out_specs=pl.BlockSpec((tm, tn), lambda i, j, kk: (i, j)),
        out_shape=jax.ShapeDtypeStruct((m, n), F32),
        compiler_params=_cparams(("parallel", "parallel", "arbitrary")),
        name="mm_down",
    )(x, w, resid)


LOG2E = 1.4426950408889634
GLR_D0 = 8
GLR_NH = 2
GLR_UNROLL = 4


def _levels(c):
    out, b = [], GLR_D0
    while b < c:
        out.append(b)
        b *= 2
    return out


def _prefix_matrix(c):
    t = np.arange(c)[:, None]
    s = np.arange(c)[None, :]
    low = s <= t
    same0 = (t // GLR_D0) == (s // GLR_D0)
    mats = [low, ~low, low & same0]
    for b in _levels(c):
        same = (t // b) == (s // b)
        odd = ((t // b) % 2) == 1
        mats.append(np.where(odd, low & same, (~low) & same))
    m = np.concatenate(mats, axis=0).astype(np.float32)
    return jnp.asarray(np.concatenate([m, m, m], axis=1), dtype=BF16)


def _diag_matrix(c, hl):
    kd = LANES // hl
    sp = np.arange(GLR_D0)[:, None, None]
    lane = np.arange(LANES)[None, :, None]
    col = np.arange(hl * c)[None, None, :]
    r = ((col // c) == (lane // kd)) & ((col % GLR_D0) == sp)
    return jnp.asarray(r.reshape(GLR_D0 * LANES, hl * c).astype(np.float32), dtype=BF16)


def _glr_consts(c, hl):
    t = lax.broadcasted_iota(jnp.int32, (c, hl * c), 0)
    s = lax.broadcasted_iota(jnp.int32, (c, hl * c), 1) & (c - 1)
    lvl_masks = []
    for b in _levels(c):
        lg = b.bit_length() - 1
        tb = lax.shift_right_logical(t, lg)
        sb = lax.shift_right_logical(s, lg)
        lvl_masks.append((lax.shift_right_logical(tb, 1) == lax.shift_right_logical(sb, 1))
                         & ((tb & 1) == 1) & ((sb & 1) == 0))
    lg0 = GLR_D0.bit_length() - 1
    diag_valid = ((lax.shift_right_logical(t, lg0) == lax.shift_right_logical(s, lg0))
                  & ((s & (GLR_D0 - 1)) <= (t & (GLR_D0 - 1))))
    lane = lax.broadcasted_iota(jnp.int32, (1, LANES), 1)
    head_masks = [(lane < C_DK).astype(F32), (lane >= C_DK).astype(F32)]
    return dict(lvl_masks=lvl_masks, diag_valid=diag_valid, head_masks=head_masks, lane=lane)


def _glr_prefix(lf2, pm):
    hi = lf2.astype(BF16)
    r1 = lf2 - hi.astype(F32)
    mid = r1.astype(BF16)
    lo = (r1 - mid.astype(F32)).astype(BF16)
    return _dot(pm, jnp.concatenate([hi, mid, lo], axis=0))


def _glr_intra(q, k, pref, rd, cst, *, c, hl):
    hm = cst["head_masks"]
    g_all = pref[0:c]
    qhat = (q * jnp.exp2(g_all)).astype(BF16)
    kdec = (k * jnp.exp2(pref[c:2 * c])).astype(BF16)
    gin0 = pref[2 * c:3 * c]

    nb = c // GLR_D0
    k3 = k.reshape(nb, GLR_D0, LANES)
    g3 = gin0.reshape(nb, GLR_D0, LANES)
    ps = []
    for sp in range(GLR_D0):
        kb = jnp.broadcast_to(k3[:, sp:sp + 1, :], (nb, GLR_D0, LANES)).reshape(c, LANES)
        gb = jnp.broadcast_to(g3[:, sp:sp + 1, :], (nb, GLR_D0, LANES)).reshape(c, LANES)
        ps.append((q * kb * jnp.exp2(jnp.minimum(gin0 - gb, 0.0))).astype(BF16))
    a = jnp.where(cst["diag_valid"], _dot(jnp.concatenate(ps, axis=1), rd), 0.0)
    for i in range(len(cst["lvl_masks"])):
        eb = jnp.exp2(pref[(3 + i) * c:(4 + i) * c])
        qb = (q * eb).astype(BF16)
        kb = k * eb
        if hl == 2:
            kb = jnp.concatenate([kb * hm[0], kb * hm[1]], axis=0)
        a = jnp.where(cst["lvl_masks"][i], _dot_nt(qb, kb.astype(BF16)), a)
    return a.astype(BF16), qhat, kdec, jnp.exp2(g_all[c - 1:c, :])


def _glr_state(a, qhat, kdec, dec, vs, st, cst, *, hl):
    stb = st.astype(BF16)
    first = cst["lane"] < C_DK
    outs, upd = [], []
    for h in range(hl):
        vb = vs[h].astype(BF16)
        if hl == 2:
            zero = jnp.zeros_like(vb)
            vz = jnp.concatenate([vb, zero] if h == 0 else [zero, vb], axis=0)
            qh = jnp.where(first if h == 0 else ~first, qhat, jnp.zeros_like(qhat))
        else:
            vz, qh = vb, qhat
        outs.append(_dot(a, vz) + _dot_nt(qh, stb))
        upd.append(_dot_tn(vb, kdec))
    u = jnp.where(first, upd[0], upd[1]) if hl == 2 else upd[0]
    return outs, st * dec + u


def _gated_norm(o, norm, gate):
    var = jnp.mean(o * o, axis=-1, keepdims=True)
    return o * lax.rsqrt(var + EPS) * norm * (gate * (1.0 / (1.0 + jnp.exp(-gate))))


def _glr_scratch(t, c, hl):
    w = GLR_NH * LANES
    return [pltpu.VMEM((GLR_NH, t, hl * c), BF16), pltpu.VMEM((t, w), BF16),
            pltpu.VMEM((t, w), BF16), pltpu.VMEM((t // c, GLR_D0, w), F32)]


def _hgrn_kernel(*refs, c, nchunks, layer, has_s0):
    if has_s0:
        (pm_ref, rd_ref, lbl_ref, norm_ref, q_ref, z_ref, v_ref, g_ref, s0_ref, o_ref, s_ref,
         a_s, qh_s, kd_s, dec_s) = refs
    else:
        (pm_ref, rd_ref, lbl_ref, norm_ref, q_ref, z_ref, v_ref, g_ref, o_ref, s_ref,
         a_s, qh_s, kd_s, dec_s) = refs
    logits = lbl_ref[...]
    e = jnp.exp(logits - jnp.max(logits, axis=0, keepdims=True))
    probs = e / jnp.sum(e, axis=0, keepdims=True)
    lb = jnp.sum(probs[0:layer + 1], axis=0, keepdims=True) - probs[0:1]
    lb_floor = jnp.maximum(lb, LB_FLOOR)
    oml = 1.0 - lb
    norm = norm_ref[...]
    pm = pm_ref[...]
    rd = rd_ref[...]
    cst = _glr_consts(c, 1)

    def intra(n, carry):
        rows = pl.ds(pl.multiple_of(n * c, c), c)
        q = q_ref[rows, :]
        z = z_ref[rows, :]
        ez = jnp.exp(-jnp.abs(z))
        r = 1.0 / (1.0 + ez)
        pos = z >= 0.0
        lf2 = jnp.log2(lb_floor + oml * jnp.where(pos, r, ez * r))
        k = oml * jnp.where(pos, ez * r, r)
        pref = _glr_prefix(lf2, pm)
        for j in range(GLR_NH):
            sl = slice(j * LANES, (j + 1) * LANES)
            a, qh, kd, dec = _glr_intra(q[:, sl], k[:, sl], pref[:, sl], rd, cst, c=c, hl=1)
            a_s[j, rows, :] = a
            qh_s[rows, sl] = qh
            kd_s[rows, sl] = kd
            dec_s[n, :, sl] = jnp.broadcast_to(dec, (GLR_D0, LANES))
        return carry

    lax.fori_loop(0, nchunks, intra, 0, unroll=min(GLR_UNROLL, nchunks))

    def state(n, sts):
        rows = pl.ds(pl.multiple_of(n * c, c), c)
        v = v_ref[rows, :]
        g = g_ref[rows, :]
        new, outs = [], []
        for j in range(GLR_NH):
            sl = slice(j * LANES, (j + 1) * LANES)
            o, st2 = _glr_state(a_s[j, rows, :], qh_s[rows, sl], kd_s[rows, sl],
                                dec_s[n, 0:1, sl], [v[:, sl]], sts[j], cst, hl=1)
            outs.append(_gated_norm(o[0], norm, g[:, sl]))
            new.append(st2)
        o_ref[rows, :] = jnp.concatenate(outs, axis=1).astype(o_ref.dtype)
        return tuple(new)

    if has_s0:
        st0 = tuple(s0_ref[0, j].T for j in range(GLR_NH))
    else:
        st0 = tuple(jnp.zeros((A_DV, A_DK), F32) for _ in range(GLR_NH))
    sts = lax.fori_loop(0, nchunks, state, st0, unroll=min(GLR_UNROLL, nchunks))
    for j in range(GLR_NH):
        s_ref[0, j] = sts[j].T


def _hgrn(proj, lbl, norm, s0, layer, nb, t, c):
    pm = _prefix_matrix(c)
    rd = _diag_matrix(c, 1)
    nchunks = t // c
    has_s0 = s0 is not None
    w = GLR_NH * LANES

    def col(off):
        return pl.BlockSpec((t, w), lambda b, h: (b, off // GLR_NH + h))

    st_spec = pl.BlockSpec((1, GLR_NH, A_DK, A_DV), lambda b, h: (b, h, 0, 0))
    in_specs = [pl.BlockSpec(pm.shape, lambda b, h: (0, 0)),
                pl.BlockSpec(rd.shape, lambda b, h: (0, 0)),
                pl.BlockSpec((DEPTH, w), lambda b, h: (0, h)),
                pl.BlockSpec((1, LANES), lambda b, h: (0, 0)),
                col(COL_A_Q), col(COL_A_F), col(COL_A_I), col(COL_A_G)]
    args = [pm, rd, lbl, norm.reshape(1, LANES), proj, proj, proj, proj]
    if has_s0:
        in_specs.append(st_spec)
        args.append(s0)
    return pl.pallas_call(
        functools.partial(_hgrn_kernel, c=c, nchunks=nchunks, layer=layer, has_s0=has_s0),
        grid=(nb, A_HEADS // GLR_NH),
        in_specs=in_specs,
        out_specs=[pl.BlockSpec((t, w), lambda b, h: (b, h)), st_spec],
        out_shape=[jax.ShapeDtypeStruct((nb * t, A_WIDTH), BF16),
                   jax.ShapeDtypeStruct((nb, A_HEADS, A_DK, A_DV), F32)],
        scratch_shapes=_glr_scratch(t, c, 1),
        compiler_params=_cparams(("parallel", "parallel")),
        name="hgrn",
    )(*args)


def _gla_kernel(*refs, c, nchunks, has_s0):
    if has_s0:
        (pm_ref, rd_ref, wa_ref, ba_ref, norm_ref, q_ref, k_ref, v_ref, r_ref, ca_ref,
         s0_ref, o_ref, s_ref, a_s, qh_s, kd_s, dec_s) = refs
    else:
        (pm_ref, rd_ref, wa_ref, ba_ref, norm_ref, q_ref, k_ref, v_ref, r_ref, ca_ref,
         o_ref, s_ref, a_s, qh_s, kd_s, dec_s) = refs
    wa = wa_ref[...]
    ba = ba_ref[...]
    norm = norm_ref[...]
    pm = pm_ref[...]
    rd = rd_ref[...]
    cst = _glr_consts(c, 2)

    def intra(n, carry):
        rows = pl.ds(pl.multiple_of(n * c, c), c)
        x = _dot(ca_ref[rows, :].astype(BF16), wa) + ba
        lf2 = (jnp.minimum(x, 0.0) - jnp.log1p(jnp.exp(-jnp.abs(x)))) * (LOG2E / GLA_NORMALIZER)
        q = q_ref[rows, :] * (C_DK ** -0.5)
        k = k_ref[rows, :]
        pref = _glr_prefix(lf2, pm)
        for j in range(GLR_NH):
            sl = slice(j * LANES, (j + 1) * LANES)
            a, qh, kd, dec = _glr_intra(q[:, sl], k[:, sl], pref[:, sl], rd, cst, c=c, hl=2)
            a_s[j, rows, :] = a
            qh_s[rows, sl] = qh
            kd_s[rows, sl] = kd
            dec_s[n, :, sl] = jnp.broadcast_to(dec, (GLR_D0, LANES))
        return carry

    lax.fori_loop(0, nchunks, intra, 0, unroll=min(GLR_UNROLL, nchunks))

    def state(n, sts):
        rows = pl.ds(pl.multiple_of(n * c, c), c)
        v = v_ref[rows, :]
        gate = r_ref[rows, :]
        new, outs = [], []
        for j in range(GLR_NH):
            sl = slice(j * LANES, (j + 1) * LANES)
            hs = [slice((2 * j + h) * C_DV, (2 * j + h + 1) * C_DV) for h in range(2)]
            o, st2 = _glr_state(a_s[j, rows, :], qh_s[rows, sl], kd_s[rows, sl],
                                dec_s[n, 0:1, sl], [v[:, hs[0]], v[:, hs[1]]], sts[j], cst, hl=2)
            outs += [_gated_norm(o[h], norm, gate[:, hs[h]]) for h in range(2)]
            new.append(st2)
        o_ref[rows, :] = jnp.concatenate(outs, axis=1).astype(o_ref.dtype)
        return tuple(new)

    if has_s0:
        st0 = tuple(s0_ref[0, j].T for j in range(GLR_NH))
    else:
        st0 = tuple(jnp.zeros((C_DV, LANES), F32) for _ in range(GLR_NH))
    sts = lax.fori_loop(0, nchunks, state, st0, unroll=min(GLR_UNROLL, nchunks))
    for j in range(GLR_NH):
        s_ref[0, j] = sts[j].T


def _gla(proj, wa2, ba, norm, s0, nb, t, c):
    pm = _prefix_matrix(c)
    rd = _diag_matrix(c, 2)
    nchunks = t // c
    has_s0 = s0 is not None
    npair = C_HEADS // 2
    w = GLR_NH * LANES

    def col(off, width):
        return pl.BlockSpec((t, width), lambda b, p: (b, off * LANES // width + p))

    st_spec = pl.BlockSpec((1, GLR_NH, LANES, C_DV), lambda b, p: (b, p, 0, 0))
    in_specs = [pl.BlockSpec(pm.shape, lambda b, p: (0, 0)),
                pl.BlockSpec(rd.shape, lambda b, p: (0, 0)),
                pl.BlockSpec((LANES, w), lambda b, p: (0, p)),
                pl.BlockSpec((1, w), lambda b, p: (0, p)),
                pl.BlockSpec((1, LANES), lambda b, p: (0, 0)),
                col(COL_C_Q, w), col(COL_C_K, w), col(COL_C_V, 2 * w), col(COL_C_R, 2 * w),
                pl.BlockSpec((t, LANES), lambda b, p: (b, COL_C_A))]
    args = [pm, rd, wa2, ba.reshape(1, -1), norm.reshape(1, LANES), proj, proj, proj, proj, proj]
    if has_s0:
        in_specs.append(st_spec)
        args.append(s0.reshape(nb, npair, 2 * C_DK, C_DV))
    o, s = pl.pallas_call(
        functools.partial(_gla_kernel, c=c, nchunks=nchunks, has_s0=has_s0),
        grid=(nb, npair // GLR_NH),
        in_specs=in_specs,
        out_specs=[pl.BlockSpec((t, 2 * w), lambda b, p: (b, p)), st_spec],
        out_shape=[jax.ShapeDtypeStruct((nb * t, C_WIDTH), BF16),
                   jax.ShapeDtypeStruct((nb, npair, 2 * C_DK, C_DV), F32)],
        scratch_shapes=_glr_scratch(t, c, 2),
        compiler_params=_cparams(("parallel", "parallel")),
        name="gla",
    )(*args)
    return o, s.reshape(nb, C_HEADS, C_DK, C_DV)


def _rope_tables(pos):
    half = B_HD // 2
    inv = ROPE_THETA ** (-jnp.arange(half, dtype=F32) / half)
    ang = pos.astype(F32)[:, None] * inv[None, :]
    cos = jnp.cos(ang)
    sin = jnp.sin(ang)
    cos_t = jnp.tile(cos, (1, 2 * LANES // B_HD))
    sin_t = jnp.tile(jnp.concatenate([-sin, sin], axis=-1), (1, LANES // B_HD))
    return cos_t, sin_t


def _rope(x, cos_t, sin_t):
    n = x.shape[1]
    half = B_HD // 2
    lane = lax.broadcasted_iota(jnp.int32, (1, n), 1)
    first = (lane & (B_HD - 1)) < half
    rot = jnp.where(first, pltpu.roll(x, n - half, axis=1), pltpu.roll(x, half, axis=1))
    reps = n // LANES
    if reps > 1:
        cos_t = jnp.concatenate([cos_t] * reps, axis=1)
        sin_t = jnp.concatenate([sin_t] * reps, axis=1)
    return x * cos_t + rot * sin_t


def _attend(qs, kb, vb, sink_col, valid):
    s = _dot_nt(qs, kb) * B_SCALE
    if valid is not None:
        s = jnp.where(valid, s, NEG_BIG)
    m = jnp.maximum(jnp.max(s, axis=-1, keepdims=True), sink_col)
    p = jnp.exp(s - m)
    den = jnp.sum(p, axis=-1, keepdims=True) + jnp.exp(sink_col - m)
    return _dot(p.astype(BF16), vb) * (1.0 / den)


def _sink_column(sink_ref, base, t):
    rows = lax.broadcasted_iota(jnp.int32, (B_GROUP * t, 1), 0)
    col = jnp.zeros((B_GROUP * t, 1), F32)
    for g in range(B_GROUP):
        col = jnp.where((rows >= g * t) & (rows < (g + 1) * t), sink_ref[base + g], col)
    return col


def _swa_prompt_kernel(sink_ref, q_ref, k_ref, v_ref, cq_ref, sq_ref, ck_ref, sk_ref,
                       o_ref, kr_ref, vr_ref, kro_ref, *, t, layer):
    khp = pl.program_id(1)
    cidx = pl.program_id(2)

    @pl.when(cidx == 0)
    def _():
        kr = _rope(k_ref[...], ck_ref[...], sk_ref[...])
        kro_ref[...] = kr.astype(BF16)
        kr_ref[0] = kr[t - WINDOW:, :]
        vr_ref[0] = v_ref[t - WINDOW:, :]

    band = (WINDOW_CHUNKS + 1) * CHUNK
    s0 = pl.multiple_of(jnp.maximum(cidx - WINDOW_CHUNKS, 0) * CHUNK, CHUNK)
    qr = _rope(q_ref[...], cq_ref[...], sq_ref[...]).astype(BF16)
    kband = kro_ref[pl.ds(s0, band), :]
    vband = v_ref[pl.ds(s0, band), :].astype(BF16)
    key_pos = s0 + lax.broadcasted_iota(jnp.int32, (1, band), 1)
    valid = key_pos < (cidx + 1) * CHUNK
    for kv in range(2):
        base = kv * B_GROUP * B_HD
        qs = jnp.concatenate(
            [qr[:, base + g * B_HD:base + (g + 1) * B_HD] for g in range(B_GROUP)], axis=0)
        sink_col = _sink_column(sink_ref, layer * B_HEADS + (khp * 2 + kv) * B_GROUP, CHUNK)
        o = _attend(qs, kband[:, kv * B_HD:(kv + 1) * B_HD], vband[:, kv * B_HD:(kv + 1) * B_HD],
                    sink_col, valid)
        for g in range(0, B_GROUP, 2):
            o_ref[:, base + g * B_HD:base + (g + 2) * B_HD] = jnp.concatenate(
                [o[g * CHUNK:(g + 1) * CHUNK], o[(g + 1) * CHUNK:(g + 2) * CHUNK]],
                axis=1).astype(o_ref.dtype)


def _swa_prompt(proj, sinks, layer, nb, t):
    nc = t // CHUNK
    cos_t, sin_t = _rope_tables(jnp.arange(t))
    qw = 2 * B_GROUP * B_HD
    qoff = COL_B_Q * LANES // qw
    kernel = functools.partial(_swa_prompt_kernel, t=t, layer=layer)
    return pl.pallas_call(
        kernel,
        grid=(nb, B_KV_HEADS // 2, nc),
        in_specs=[pl.BlockSpec(memory_space=pltpu.SMEM),
                  pl.BlockSpec((CHUNK, qw), lambda b, p, c: (b * nc + c, qoff + p)),
                  pl.BlockSpec((t, LANES), lambda b, p, c: (b, COL_B_K + p)),
                  pl.BlockSpec((t, LANES), lambda b, p, c: (b, COL_B_V + p)),
                  pl.BlockSpec((CHUNK, LANES), lambda b, p, c: (c, 0)),
                  pl.BlockSpec((CHUNK, LANES), lambda b, p, c: (c, 0)),
                  pl.BlockSpec((t, LANES), lambda b, p, c: (0, 0)),
                  pl.BlockSpec((t, LANES), lambda b, p, c: (0, 0))],
        out_specs=[pl.BlockSpec((CHUNK, qw), lambda b, p, c: (b * nc + c, p)),
                   pl.BlockSpec((1, WINDOW, LANES), lambda b, p, c: (b, 0, p)),
                   pl.BlockSpec((1, WINDOW, LANES), lambda b, p, c: (b, 0, p))],
        out_shape=[jax.ShapeDtypeStruct((nb * t, B_WIDTH), BF16),
                   jax.ShapeDtypeStruct((nb, WINDOW, B_KV_HEADS * B_HD), F32),
                   jax.ShapeDtypeStruct((nb, WINDOW, B_KV_HEADS * B_HD), F32)],
        scratch_shapes=[pltpu.VMEM((t, LANES), BF16)],
        compiler_params=_cparams(("parallel", "parallel", "arbitrary")),
        name="swa_prompt",
    )(sinks.reshape(-1), proj, proj, proj, cos_t, sin_t, cos_t, sin_t)


def _swa_sample_kernel(sink_ref, q_ref, k_ref, v_ref, ckc_ref, cvc_ref, cos_ref, sin_ref,
                       o_ref, kr_ref, vr_ref, *, t, layer):
    cos_t = cos_ref[...]
    sin_t = sin_ref[...]
    kr = _rope(k_ref[...], cos_t, sin_t)
    v = v_ref[...]
    kr_ref[...] = kr
    vr_ref[...] = v
    qr = _rope(q_ref[...], cos_t, sin_t).astype(BF16)
    ka = jnp.concatenate([ckc_ref[0], kr], axis=0).astype(BF16)
    va = jnp.concatenate([cvc_ref[0], v], axis=0).astype(BF16)
    for kh in range(B_KV_HEADS):
        base = kh * B_GROUP * B_HD
        qs = jnp.concatenate(
            [qr[:, base + g * B_HD:base + (g + 1) * B_HD] for g in range(B_GROUP)], axis=0)
        sink_col = _sink_column(sink_ref, layer * B_HEADS + kh * B_GROUP, t)
        o = _attend(qs, ka[:, kh * B_HD:(kh + 1) * B_HD], va[:, kh * B_HD:(kh + 1) * B_HD],
                    sink_col, None)
        for g in range(0, B_GROUP, 2):
            o_ref[:, base + g * B_HD:base + (g + 2) * B_HD] = jnp.concatenate(
                [o[g * t:(g + 1) * t], o[(g + 1) * t:(g + 2) * t]], axis=1).astype(o_ref.dtype)


def _swa_sample(proj, cache_k, cache_v, sinks, layer, nb, t):
    cos_t, sin_t = _rope_tables(PAST_LEN + jnp.arange(t))
    kvw = B_KV_HEADS * B_HD
    kernel = functools.partial(_swa_sample_kernel, t=t, layer=layer)
    return pl.pallas_call(
        kernel,
        grid=(nb,),
        in_specs=[pl.BlockSpec(memory_space=pltpu.SMEM),
                  pl.BlockSpec((t, B_WIDTH), lambda b: (b, COL_B_Q * LANES // B_WIDTH)),
                  pl.BlockSpec((t, kvw), lambda b: (b, COL_B_K * LANES // kvw)),
                  pl.BlockSpec((t, kvw), lambda b: (b, COL_B_V * LANES // kvw)),
                  pl.BlockSpec((1, WINDOW, kvw), lambda b: (b, 0, 0)),
                  pl.BlockSpec((1, WINDOW, kvw), lambda b: (b, 0, 0)),
                  pl.BlockSpec((t, LANES), lambda b: (0, 0)),
                  pl.BlockSpec((t, LANES), lambda b: (0, 0))],
        out_specs=[pl.BlockSpec((t, B_WIDTH), lambda b: (b, 0)),
                   pl.BlockSpec((t, kvw), lambda b: (b, 0)),
                   pl.BlockSpec((t, kvw), lambda b: (b, 0))],
        out_shape=[jax.ShapeDtypeStruct((nb * t, B_WIDTH), BF16),
                   jax.ShapeDtypeStruct((nb * t, kvw), F32),
                   jax.ShapeDtypeStruct((nb * t, kvw), F32)],
        compiler_params=_cparams(("parallel",)),
        name="swa_sample",
    )(sinks.reshape(-1), proj, proj, proj,
      cache_k.reshape(nb, WINDOW, kvw), cache_v.reshape(nb, WINDOW, kvw), cos_t, sin_t)


def _tiles(m):
    return min(m, 1024)


def _layer(x, w, layer, nb, t, cache, s_a, s_c, c):
    m, d = x.shape
    tm = _tiles(m)
    h = _rmsnorm(x, w["norm_mix"][layer], BF16, min(m, 256))
    proj = _matmul(h, w["w_in"][layer], tm, MM_TN_IN, F32)
    o_a, s_a_new = _hgrn(proj, w["lb_logits"], w["hgrn_norm"][layer], s_a, layer, nb, t, c)
    if cache is None:
        o_b, k_rows, v_rows = _swa_prompt(proj, w["sinks"], layer, nb, t)
        k_rows = k_rows.reshape(nb, WINDOW, B_KV_HEADS, B_HD)
        v_rows = v_rows.reshape(nb, WINDOW, B_KV_HEADS, B_HD)
    else:
        o_b, k_rows, v_rows = _swa_sample(proj, cache[0], cache[1], w["sinks"], layer, nb, t)
        k_rows = k_rows.reshape(nb, t, B_KV_HEADS, B_HD)
        v_rows = v_rows.reshape(nb, t, B_KV_HEADS, B_HD)
    o_c, s_c_new = _gla(proj, w["w_a2"][layer], w["b_a"][layer], w["gla_norm"][layer],
                        s_c, nb, t, c)
    x = _matmul_out(o_a, o_b, o_c, w["w_out"][layer], x, tm, min(d, 512))
    h = _rmsnorm(x, w["norm_ffn"][layer], BF16, min(m, 256))
    ff = w["w_gate"].shape[2]
    mid = _matmul_gate_up(h, w["w_gate"][layer], w["w_up"][layer], tm, 512)
    x = _matmul_down(mid, w["w_down"][layer], x, tm, min(d, 1024), ff // 4)
    return x, k_rows, v_rows, s_a_new, s_c_new


def kernel(x_prompt, x_sample, cache_k_swa, cache_v_swa, state_hgrn, state_gla, norm_mix, w_in,
           hgrn_lb_logits, hgrn_norm, swa_sinks, gla_w_alpha2, gla_b_alpha, gla_norm, w_out,
           norm_ffn, w_gate_up, w_down, norm_final):
    n_p, t_p, d = x_prompt.shape
    n_s, t_s, _ = x_sample.shape
    d_ff = w_down.shape[1]
    ff_pad = -(-d_ff // 1024) * 1024
    w = dict(
        norm_mix=norm_mix, norm_ffn=norm_ffn, hgrn_norm=hgrn_norm, gla_norm=gla_norm,
        lb_logits=hgrn_lb_logits, sinks=swa_sinks, b_a=gla_b_alpha,
        w_in=jnp.pad(w_in, ((0, 0), (0, 0), (0, IN_COLS_PAD - IN_COLS))).astype(BF16),
        w_a2=jnp.pad(gla_w_alpha2, ((0, 0), (0, LANES - GLA_RANK), (0, 0))).astype(BF16),
        w_out=w_out.astype(BF16),
        w_gate=jnp.pad(w_gate_up[:, :, :d_ff], ((0, 0), (0, 0), (0, ff_pad - d_ff))).astype(BF16),
        w_up=jnp.pad(w_gate_up[:, :, d_ff:], ((0, 0), (0, 0), (0, ff_pad - d_ff))).astype(BF16),
        w_down=jnp.pad(w_down, ((0, 0), (0, ff_pad - d_ff), (0, 0))).astype(BF16),
    )
    xp = x_prompt.reshape(n_p * t_p, d)
    xs = x_sample.reshape(n_s * t_s, d)
    outs_p, outs_s = [], []
    for layer in range(DEPTH):
        xp, *rest = _layer(xp, w, layer, n_p, t_p, None, None, None, CHUNK)
        outs_p.append(rest)
        xs, *rest = _layer(xs, w, layer, n_s, t_s, (cache_k_swa[layer], cache_v_swa[layer]),
                           state_hgrn[layer], state_gla[layer], t_s)
        outs_s.append(rest)
    y_p = _rmsnorm(xp, norm_final, F32, min(xp.shape[0], 256)).reshape(n_p, t_p, d)
    y_s = _rmsnorm(xs, norm_final, F32, min(xs.shape[0], 256)).reshape(n_s, t_s, d)
    stack = lambda outs, i: jnp.stack([o[i] for o in outs])
    return (y_p, y_s,
            stack(outs_p, 0), stack(outs_p, 1), stack(outs_p, 2), stack(outs_p, 3),
            stack(outs_s, 0), stack(outs_s, 1), stack(outs_s, 2), stack(outs_s, 3))
```

```python
import functools

import jax
import jax.numpy as jnp
import numpy as np
from jax import lax
from jax.experimental import pallas as pl
from jax.experimental.pallas import tpu as pltpu

F32 = jnp.float32
BF16 = jnp.bfloat16

DEPTH = 2
PAST_LEN = 4096
CHUNK = 64
EPS = 1e-6
NEG_BIG = -1e30
LB_FLOOR = 1e-30
A_HEADS = 8
A_DK = 128
A_DV = 128
A_WIDTH = A_HEADS * A_DV
B_HEADS = 32
B_KV_HEADS = 4
B_GROUP = B_HEADS // B_KV_HEADS
B_HD = 64
B_WIDTH = B_HEADS * B_HD
B_SCALE = B_HD ** -0.5
WINDOW = 128
WINDOW_CHUNKS = WINDOW // CHUNK
ROPE_THETA = 10000.0
C_HEADS = 8
C_DK = 64
C_DV = 128
C_WIDTH = C_HEADS * C_DV
GLA_RANK = 16
GLA_NORMALIZER = 16.0

LANES = 128

COL_A_Q = 0
COL_A_F = 8
COL_A_I = 16
COL_A_G = 24
COL_B_Q = 32
COL_B_K = 48
COL_B_V = 50
COL_C_Q = 52
COL_C_K = 56
COL_C_V = 60
COL_C_R = 68
COL_C_A = 76
IN_COLS = 9744
IN_COLS_PAD = 9984
MM_TN_IN = 768
VMEM_LIMIT = 56 * 1024 * 1024


def _cparams(sem):
    return pltpu.CompilerParams(dimension_semantics=sem, vmem_limit_bytes=VMEM_LIMIT)


def _dot(a, b):
    return jnp.dot(a, b, preferred_element_type=F32)


def _dot_nt(a, b):
    return lax.dot_general(a, b, (((1,), (1,)), ((), ())), preferred_element_type=F32)


def _dot_tn(a, b):
    return lax.dot_general(a, b, (((0,), (0,)), ((), ())), preferred_element_type=F32)


def _rmsnorm_kernel(x_ref, g_ref, o_ref):
    x = x_ref[...]
    var = jnp.mean(x * x, axis=-1, keepdims=True)
    o_ref[...] = (x * lax.rsqrt(var + EPS) * g_ref[...]).astype(o_ref.dtype)


def _rmsnorm(x, g, out_dtype, tm):
    m, d = x.shape
    return pl.pallas_call(
        _rmsnorm_kernel,
        grid=(m // tm,),
        in_specs=[pl.BlockSpec((tm, d), lambda i: (i, 0)),
                  pl.BlockSpec((1, d), lambda i: (0, 0))],
        out_specs=pl.BlockSpec((tm, d), lambda i: (i, 0)),
        out_shape=jax.ShapeDtypeStruct((m, d), out_dtype),
        compiler_params=_cparams(("parallel",)),
        name="rmsnorm",
    )(x, g.reshape(1, d))


def _mm_kernel(x_ref, w_ref, o_ref):
    o_ref[...] = _dot(x_ref[...], w_ref[...]).astype(o_ref.dtype)


def _matmul(x, w, tm, tn, out_dtype):
    m, k = x.shape
    n = w.shape[1]
    return pl.pallas_call(
        _mm_kernel,
        grid=(m // tm, n // tn),
        in_specs=[pl.BlockSpec((tm, k), lambda i, j: (i, 0)),
                  pl.BlockSpec((k, tn), lambda i, j: (0, j))],
        out_specs=pl.BlockSpec((tm, tn), lambda i, j: (i, j)),
        out_shape=jax.ShapeDtypeStruct((m, n), out_dtype),
        compiler_params=_cparams(("parallel", "arbitrary")),
        name="mm_in",
    )(x, w)


def _mm_out_kernel(oa_ref, ob_ref, oc_ref, w_ref, r_ref, o_ref, cat_ref):
    @pl.when(pl.program_id(1) == 0)
    def _():
        cat_ref[:, 0:A_WIDTH] = oa_ref[...]
        cat_ref[:, A_WIDTH:A_WIDTH + B_WIDTH] = ob_ref[...]
        cat_ref[:, A_WIDTH + B_WIDTH:] = oc_ref[...]

    o_ref[...] = r_ref[...] + _dot(cat_ref[...], w_ref[...])


def _matmul_out(oa, ob, oc, w, resid, tm, tn):
    m = oa.shape[0]
    k, n = w.shape
    return pl.pallas_call(
        _mm_out_kernel,
        grid=(m // tm, n // tn),
        in_specs=[pl.BlockSpec((tm, A_WIDTH), lambda i, j: (i, 0)),
                  pl.BlockSpec((tm, B_WIDTH), lambda i, j: (i, 0)),
                  pl.BlockSpec((tm, C_WIDTH), lambda i, j: (i, 0)),
                  pl.BlockSpec((k, tn), lambda i, j: (0, j)),
                  pl.BlockSpec((tm, tn), lambda i, j: (i, j))],
        out_specs=pl.BlockSpec((tm, tn), lambda i, j: (i, j)),
        out_shape=jax.ShapeDtypeStruct((m, n), F32),
        scratch_shapes=[pltpu.VMEM((tm, k), BF16)],
        compiler_params=_cparams(("parallel", "arbitrary")),
        name="mm_out",
    )(oa, ob, oc, w, resid)


def _mm_gate_up_kernel(x_ref, wg_ref, wu_ref, o_ref):
    x = x_ref[...]
    gate = _dot(x, wg_ref[...])
    up = _dot(x, wu_ref[...])
    o_ref[...] = (gate * (1.0 / (1.0 + jnp.exp(-gate))) * up).astype(o_ref.dtype)


def _matmul_gate_up(x, wg, wu, tm, tn):
    m, k = x.shape
    n = wg.shape[1]
    return pl.pallas_call(
        _mm_gate_up_kernel,
        grid=(m // tm, n // tn),
        in_specs=[pl.BlockSpec((tm, k), lambda i, j: (i, 0)),
                  pl.BlockSpec((k, tn), lambda i, j: (0, j)),
                  pl.BlockSpec((k, tn), lambda i, j: (0, j))],
        out_specs=pl.BlockSpec((tm, tn), lambda i, j: (i, j)),
        out_shape=jax.ShapeDtypeStruct((m, n), BF16),
        compiler_params=_cparams(("parallel", "arbitrary")),
        name="mm_gate_up",
    )(x, wg, wu)


def _mm_down_kernel(x_ref, w_ref, r_ref, o_ref):
    kk = pl.program_id(2)

    @pl.when(kk == 0)
    def _():
        o_ref[...] = r_ref[...] + _dot(x_ref[...], w_ref[...])

    @pl.when(kk != 0)
    def _():
        o_ref[...] += _dot(x_ref[...], w_ref[...])


def _matmul_down(x, w, resid, tm, tn, tk):
    m, k = x.shape
    n = w.shape[1]
    return pl.pallas_call(
        _mm_down_kernel,
        grid=(m // tm, n // tn, k // tk),
        in_specs=[pl.BlockSpec((tm, tk), lambda i, j, kk: (i, kk)),
                  pl.BlockSpec((tk, tn), lambda i, j, kk: (kk, j)),
                  pl.BlockSpec((tm, tn), lambda i, j, kk: (i, j))],
        out_specs=pl.BlockSpec((tm, tn), lambda i, j, kk: (i, j)),
        out_shape=jax.ShapeDtypeStruct((m, n), F32),
        compiler_params=_cparams(("parallel", "parallel", "arbitrary")),
        name="mm_down",
    )(x, w, resid)


CAST_TN_GATE = 256
CAST_TK_DOWN = 512


def _masked_bf16(w, first, n_valid, axis):
    idx = first + lax.broadcasted_iota(jnp.int32, w.shape, axis)
    return jnp.where(idx < n_valid, w, 0.0).astype(BF16)


def _mm_in_cast_kernel(x_ref, w_ref, o_ref, wb_ref, *, n_valid, tn):
    wb = _masked_bf16(w_ref[...], pl.program_id(0) * tn, n_valid, 1)
    wb_ref[...] = wb
    o_ref[...] = _dot(x_ref[...], wb)


def _mm_in_cast(x, w3, layer, n_pad, tn):
    m, k = x.shape
    n = w3.shape[2]
    return pl.pallas_call(
        functools.partial(_mm_in_cast_kernel, n_valid=n, tn=tn),
        grid=(n_pad // tn,),
        in_specs=[pl.BlockSpec((m, k), lambda j: (0, 0)),
                  pl.BlockSpec((None, k, tn), lambda j: (layer, 0, j))],
        out_specs=[pl.BlockSpec((m, tn), lambda j: (0, j)),
                   pl.BlockSpec((k, tn), lambda j: (0, j))],
        out_shape=[jax.ShapeDtypeStruct((m, n_pad), F32),
                   jax.ShapeDtypeStruct((k, n_pad), BF16)],
        compiler_params=_cparams(("arbitrary",)),
        name="mm_in_cast",
    )(x, w3)


def _mm_out_cast_kernel(oa_ref, ob_ref, oc_ref, w_ref, r_ref, o_ref, wb_ref, cat_ref):
    @pl.when(pl.program_id(0) == 0)
    def _():
        cat_ref[:, 0:A_WIDTH] = oa_ref[...]
        cat_ref[:, A_WIDTH:A_WIDTH + B_WIDTH] = ob_ref[...]
        cat_ref[:, A_WIDTH + B_WIDTH:] = oc_ref[...]

    wb = w_ref[...].astype(BF16)
    wb_ref[...] = wb
    o_ref[...] = r_ref[...] + _dot(cat_ref[...], wb)


def _mm_out_cast(oa, ob, oc, w3, resid, layer, tn):
    m = oa.shape[0]
    k, n = w3.shape[1:]
    return pl.pallas_call(
        _mm_out_cast_kernel,
        grid=(n // tn,),
        in_specs=[pl.BlockSpec((m, A_WIDTH), lambda j: (0, 0)),
                  pl.BlockSpec((m, B_WIDTH), lambda j: (0, 0)),
                  pl.BlockSpec((m, C_WIDTH), lambda j: (0, 0)),
                  pl.BlockSpec((None, k, tn), lambda j: (layer, 0, j)),
                  pl.BlockSpec((m, tn), lambda j: (0, j))],
        out_specs=[pl.BlockSpec((m, tn), lambda j: (0, j)),
                   pl.BlockSpec((k, tn), lambda j: (0, j))],
        out_shape=[jax.ShapeDtypeStruct((m, n), F32),
                   jax.ShapeDtypeStruct((k, n), BF16)],
        scratch_shapes=[pltpu.VMEM((m, k), BF16)],
        compiler_params=_cparams(("arbitrary",)),
        name="mm_out_cast",
    )(oa, ob, oc, w3, resid)


def _gate_up_cast_kernel(x_ref, wg_ref, wu_ref, o_ref, wgb_ref, wub_ref, *, n_tiles):
    valid = pl.program_id(0) < n_tiles
    wg = jnp.where(valid, wg_ref[...], 0.0).astype(BF16)
    wu = jnp.where(valid, wu_ref[...], 0.0).astype(BF16)
    wgb_ref[...] = wg
    wub_ref[...] = wu
    x = x_ref[...]
    gate = _dot(x, wg)
    o_ref[...] = (gate * (1.0 / (1.0 + jnp.exp(-gate))) * _dot(x, wu)).astype(o_ref.dtype)


def _gate_up_cast(x, wgu3, layer, ff_pad):
    m, k = x.shape
    tn = CAST_TN_GATE
    n_tiles = wgu3.shape[2] // 2 // tn
    assert n_tiles * tn * 2 == wgu3.shape[2]
    return pl.pallas_call(
        functools.partial(_gate_up_cast_kernel, n_tiles=n_tiles),
        grid=(ff_pad // tn,),
        in_specs=[pl.BlockSpec((m, k), lambda j: (0, 0)),
                  pl.BlockSpec((None, k, tn), lambda j: (layer, 0, jnp.minimum(j, n_tiles - 1))),
                  pl.BlockSpec((None, k, tn),
                               lambda j: (layer, 0, n_tiles + jnp.minimum(j, n_tiles - 1)))],
        out_specs=[pl.BlockSpec((m, tn), lambda j: (0, j)),
                   pl.BlockSpec((k, tn), lambda j: (0, j)),
                   pl.BlockSpec((k, tn), lambda j: (0, j))],
        out_shape=[jax.ShapeDtypeStruct((m, ff_pad), BF16),
                   jax.ShapeDtypeStruct((k, ff_pad), BF16),
                   jax.ShapeDtypeStruct((k, ff_pad), BF16)],
        compiler_params=_cparams(("arbitrary",)),
        name="gate_up_cast",
    )(x, wgu3, wgu3)


def _down_cast_kernel(x_ref, w_ref, r_ref, o_ref, wb_ref, *, k_valid, tk):
    kk = pl.program_id(0)
    wb = _masked_bf16(w_ref[...], kk * tk, k_valid, 0)
    wb_ref[...] = wb

    @pl.when(kk == 0)
    def _():
        o_ref[...] = r_ref[...] + _dot(x_ref[...], wb)

    @pl.when(kk != 0)
    def _():
        o_ref[...] += _dot(x_ref[...], wb)


def _down_cast(x, wd3, resid, layer):
    m, ff_pad = x.shape
    ff, d = wd3.shape[1:]
    tk = CAST_TK_DOWN
    last = (ff - 1) // tk
    return pl.pallas_call(
        functools.partial(_down_cast_kernel, k_valid=ff, tk=tk),
        grid=(ff_pad // tk,),
        in_specs=[pl.BlockSpec((m, tk), lambda kk: (0, kk)),
                  pl.BlockSpec((None, tk, d), lambda kk: (layer, jnp.minimum(kk, last), 0)),
                  pl.BlockSpec((m, d), lambda kk: (0, 0))],
        out_specs=[pl.BlockSpec((m, d), lambda kk: (0, 0)),
                   pl.BlockSpec((tk, d), lambda kk: (kk, 0))],
        out_shape=[jax.ShapeDtypeStruct((m, d), F32),
                   jax.ShapeDtypeStruct((ff_pad, d), BF16)],
        compiler_params=_cparams(("arbitrary",)),
        name="down_cast",
    )(x, wd3, resid)


LOG2E = 1.4426950408889634
GLR_D0 = 8
GLR_NH = 2
GLR_UNROLL = 4


def _levels(c):
    out, b = [], GLR_D0
    while b < c:
        out.append(b)
        b *= 2
    return out


def _prefix_matrix(c):
    t = np.arange(c)[:, None]
    s = np.arange(c)[None, :]
    low = s <= t
    same0 = (t // GLR_D0) == (s // GLR_D0)
    mats = [low, ~low, low & same0]
    for b in _levels(c):
        same = (t // b) == (s // b)
        odd = ((t // b) % 2) == 1
        mats.append(np.where(odd, low & same, (~low) & same))
    m = np.concatenate(mats, axis=0).astype(np.float32)
    return jnp.asarray(np.concatenate([m, m, m], axis=1), dtype=BF16)


def _diag_matrix(c, hl):
    kd = LANES // hl
    sp = np.arange(GLR_D0)[:, None, None]
    lane = np.arange(LANES)[None, :, None]
    col = np.arange(hl * c)[None, None, :]
    r = ((col // c) == (lane // kd)) & ((col % GLR_D0) == sp)
    return jnp.asarray(r.reshape(GLR_D0 * LANES, hl * c).astype(np.float32), dtype=BF16)


def _glr_consts(c, hl):
    t = lax.broadcasted_iota(jnp.int32, (c, hl * c), 0)
    s = lax.broadcasted_iota(jnp.int32, (c, hl * c), 1) & (c - 1)
    lvl_masks = []
    for b in _levels(c):
        lg = b.bit_length() - 1
        tb = lax.shift_right_logical(t, lg)
        sb = lax.shift_right_logical(s, lg)
        lvl_masks.append((lax.shift_right_logical(tb, 1) == lax.shift_right_logical(sb, 1))
                         & ((tb & 1) == 1) & ((sb & 1) == 0))
    lg0 = GLR_D0.bit_length() - 1
    diag_valid = ((lax.shift_right_logical(t, lg0) == lax.shift_right_logical(s, lg0))
                  & ((s & (GLR_D0 - 1)) <= (t & (GLR_D0 - 1))))
    lane = lax.broadcasted_iota(jnp.int32, (1, LANES), 1)
    head_masks = [(lane < C_DK).astype(F32), (lane >= C_DK).astype(F32)]
    return dict(lvl_masks=lvl_masks, diag_valid=diag_valid, head_masks=head_masks, lane=lane)


def _glr_prefix(lf2, pm):
    hi = lf2.astype(BF16)
    r1 = lf2 - hi.astype(F32)
    mid = r1.astype(BF16)
    lo = (r1 - mid.astype(F32)).astype(BF16)
    return _dot(pm, jnp.concatenate([hi, mid, lo], axis=0))


def _glr_intra(q, k, pref, rd, cst, *, c, hl):
    hm = cst["head_masks"]
    g_all = pref[0:c]
    qhat = (q * jnp.exp2(g_all)).astype(BF16)
    kdec = (k * jnp.exp2(pref[c:2 * c])).astype(BF16)
    gin0 = pref[2 * c:3 * c]

    nb = c // GLR_D0
    k3 = k.reshape(nb, GLR_D0, LANES)
    g3 = gin0.reshape(nb, GLR_D0, LANES)
    ps = []
    for sp in range(GLR_D0):
        kb = jnp.broadcast_to(k3[:, sp:sp + 1, :], (nb, GLR_D0, LANES)).reshape(c, LANES)
        gb = jnp.broadcast_to(g3[:, sp:sp + 1, :], (nb, GLR_D0, LANES)).reshape(c, LANES)
        ps.append((q * kb * jnp.exp2(jnp.minimum(gin0 - gb, 0.0))).astype(BF16))
    a = jnp.where(cst["diag_valid"], _dot(jnp.concatenate(ps, axis=1), rd), 0.0)
    for i in range(len(cst["lvl_masks"])):
        eb = jnp.exp2(pref[(3 + i) * c:(4 + i) * c])
        qb = (q * eb).astype(BF16)
        kb = k * eb
        if hl == 2:
            kb = jnp.concatenate([kb * hm[0], kb * hm[1]], axis=0)
        a = jnp.where(cst["lvl_masks"][i], _dot_nt(qb, kb.astype(BF16)), a)
    return a.astype(BF16), qhat, kdec, jnp.exp2(g_all[c - 1:c, :])


def _glr_state(a, qhat, kdec, dec, vs, st, cst, *, hl):
    stb = st.astype(BF16)
    first = cst["lane"] < C_DK
    outs, upd = [], []
    for h in range(hl):
        vb = vs[h].astype(BF16)
        if hl == 2:
            zero = jnp.zeros_like(vb)
            vz = jnp.concatenate([vb, zero] if h == 0 else [zero, vb], axis=0)
            qh = jnp.where(first if h == 0 else ~first, qhat, jnp.zeros_like(qhat))
        else:
            vz, qh = vb, qhat
        outs.append(_dot(a, vz) + _dot_nt(qh, stb))
        upd.append(_dot_tn(vb, kdec))
    u = jnp.where(first, upd[0], upd[1]) if hl == 2 else upd[0]
    return outs, st * dec + u


def _gated_norm(o, norm, gate):
    var = jnp.mean(o * o, axis=-1, keepdims=True)
    return o * lax.rsqrt(var + EPS) * norm * (gate * (1.0 / (1.0 + jnp.exp(-gate))))


def _glr_scratch(t, c, hl):
    w = GLR_NH * LANES
    return [pltpu.VMEM((GLR_NH, t, hl * c), BF16), pltpu.VMEM((t, w), BF16),
            pltpu.VMEM((t, w), BF16), pltpu.VMEM((t // c, GLR_D0, w), F32)]


def _hgrn_kernel(*refs, c, nchunks, layer, has_s0):
    if has_s0:
        (pm_ref, rd_ref, lbl_ref, norm_ref, q_ref, z_ref, v_ref, g_ref, s0_ref, o_ref, s_ref,
         a_s, qh_s, kd_s, dec_s) = refs
    else:
        (pm_ref, rd_ref, lbl_ref, norm_ref, q_ref, z_ref, v_ref, g_ref, o_ref, s_ref,
         a_s, qh_s, kd_s, dec_s) = refs
    logits = lbl_ref[...]
    e = jnp.exp(logits - jnp.max(logits, axis=0, keepdims=True))
    probs = e / jnp.sum(e, axis=0, keepdims=True)
    lb = jnp.sum(probs[0:layer + 1], axis=0, keepdims=True) - probs[0:1]
    lb_floor = jnp.maximum(lb, LB_FLOOR)
    oml = 1.0 - lb
    norm = norm_ref[...]
    pm = pm_ref[...]
    rd = rd_ref[...]
    cst = _glr_consts(c, 1)

    def intra(n, carry):
        rows = pl.ds(pl.multiple_of(n * c, c), c)
        q = q_ref[rows, :]
        z = z_ref[rows, :]
        ez = jnp.exp(-jnp.abs(z))
        r = 1.0 / (1.0 + ez)
        pos = z >= 0.0
        lf2 = jnp.log2(lb_floor + oml * jnp.where(pos, r, ez * r))
        k = oml * jnp.where(pos, ez * r, r)
        pref = _glr_prefix(lf2, pm)
        for j in range(GLR_NH):
            sl = slice(j * LANES, (j + 1) * LANES)
            a, qh, kd, dec = _glr_intra(q[:, sl], k[:, sl], pref[:, sl], rd, cst, c=c, hl=1)
            a_s[j, rows, :] = a
            qh_s[rows, sl] = qh
            kd_s[rows, sl] = kd
            dec_s[n, :, sl] = jnp.broadcast_to(dec, (GLR_D0, LANES))
        return carry

    lax.fori_loop(0, nchunks, intra, 0, unroll=min(GLR_UNROLL, nchunks))

    def state(n, sts):
        rows = pl.ds(pl.multiple_of(n * c, c), c)
        v = v_ref[rows, :]
        g = g_ref[rows, :]
        new, outs = [], []
        for j in range(GLR_NH):
            sl = slice(j * LANES, (j + 1) * LANES)
            o, st2 = _glr_state(a_s[j, rows, :], qh_s[rows, sl], kd_s[rows, sl],
                                dec_s[n, 0:1, sl], [v[:, sl]], sts[j], cst, hl=1)
            outs.append(_gated_norm(o[0], norm, g[:, sl]))
            new.append(st2)
        o_ref[rows, :] = jnp.concatenate(outs, axis=1).astype(o_ref.dtype)
        return tuple(new)

    if has_s0:
        st0 = tuple(s0_ref[0, j].T for j in range(GLR_NH))
    else:
        st0 = tuple(jnp.zeros((A_DV, A_DK), F32) for _ in range(GLR_NH))
    sts = lax.fori_loop(0, nchunks, state, st0, unroll=min(GLR_UNROLL, nchunks))
    for j in range(GLR_NH):
        s_ref[0, j] = sts[j].T


def _hgrn(proj, lbl, norm, s0, layer, nb, t, c):
    pm = _prefix_matrix(c)
    rd = _diag_matrix(c, 1)
    nchunks = t // c
    has_s0 = s0 is not None
    w = GLR_NH * LANES

    def col(off):
        return pl.BlockSpec((t, w), lambda b, h: (b, off // GLR_NH + h))

    st_spec = pl.BlockSpec((1, GLR_NH, A_DK, A_DV), lambda b, h: (b, h, 0, 0))
    in_specs = [pl.BlockSpec(pm.shape, lambda b, h: (0, 0)),
                pl.BlockSpec(rd.shape, lambda b, h: (0, 0)),
                pl.BlockSpec((DEPTH, w), lambda b, h: (0, h)),
                pl.BlockSpec((1, LANES), lambda b, h: (0, 0)),
                col(COL_A_Q), col(COL_A_F), col(COL_A_I), col(COL_A_G)]
    args = [pm, rd, lbl, norm.reshape(1, LANES), proj, proj, proj, proj]
    if has_s0:
        in_specs.append(st_spec)
        args.append(s0)
    return pl.pallas_call(
        functools.partial(_hgrn_kernel, c=c, nchunks=nchunks, layer=layer, has_s0=has_s0),
        grid=(nb, A_HEADS // GLR_NH),
        in_specs=in_specs,
        out_specs=[pl.BlockSpec((t, w), lambda b, h: (b, h)), st_spec],
        out_shape=[jax.ShapeDtypeStruct((nb * t, A_WIDTH), BF16),
                   jax.ShapeDtypeStruct((nb, A_HEADS, A_DK, A_DV), F32)],
        scratch_shapes=_glr_scratch(t, c, 1),
        compiler_params=_cparams(("parallel", "parallel")),
        name="hgrn",
    )(*args)


def _gla_kernel(*refs, c, nchunks, has_s0):
    if has_s0:
        (pm_ref, rd_ref, wa_ref, ba_ref, norm_ref, q_ref, k_ref, v_ref, r_ref, ca_ref,
         s0_ref, o_ref, s_ref, a_s, qh_s, kd_s, dec_s) = refs
    else:
        (pm_ref, rd_ref, wa_ref, ba_ref, norm_ref, q_ref, k_ref, v_ref, r_ref, ca_ref,
         o_ref, s_ref, a_s, qh_s, kd_s, dec_s) = refs
    wa = wa_ref[...]
    ba = ba_ref[...]
    norm = norm_ref[...]
    pm = pm_ref[...]
    rd = rd_ref[...]
    cst = _glr_consts(c, 2)

    def intra(n, carry):
        rows = pl.ds(pl.multiple_of(n * c, c), c)
        x = _dot(ca_ref[rows, :].astype(BF16), wa) + ba
        lf2 = (jnp.minimum(x, 0.0) - jnp.log1p(jnp.exp(-jnp.abs(x)))) * (LOG2E / GLA_NORMALIZER)
        q = q_ref[rows, :] * (C_DK ** -0.5)
        k = k_ref[rows, :]
        pref = _glr_prefix(lf2, pm)
        for j in range(GLR_NH):
            sl = slice(j * LANES, (j + 1) * LANES)
            a, qh, kd, dec = _glr_intra(q[:, sl], k[:, sl], pref[:, sl], rd, cst, c=c, hl=2)
            a_s[j, rows, :] = a
            qh_s[rows, sl] = qh
            kd_s[rows, sl] = kd
            dec_s[n, :, sl] = jnp.broadcast_to(dec, (GLR_D0, LANES))
        return carry

    lax.fori_loop(0, nchunks, intra, 0, unroll=min(GLR_UNROLL, nchunks))

    def state(n, sts):
        rows = pl.ds(pl.multiple_of(n * c, c), c)
        v = v_ref[rows, :]
        gate = r_ref[rows, :]
        new, outs = [], []
        for j in range(GLR_NH):
            sl = slice(j * LANES, (j + 1) * LANES)
            hs = [slice((2 * j + h) * C_DV, (2 * j + h + 1) * C_DV) for h in range(2)]
            o, st2 = _glr_state(a_s[j, rows, :], qh_s[rows, sl], kd_s[rows, sl],
                                dec_s[n, 0:1, sl], [v[:, hs[0]], v[:, hs[1]]], sts[j], cst, hl=2)
            outs += [_gated_norm(o[h], norm, gate[:, hs[h]]) for h in range(2)]
            new.append(st2)
        o_ref[rows, :] = jnp.concatenate(outs, axis=1).astype(o_ref.dtype)
        return tuple(new)

    if has_s0:
        st0 = tuple(s0_ref[0, j].T for j in range(GLR_NH))
    else:
        st0 = tuple(jnp.zeros((C_DV, LANES), F32) for _ in range(GLR_NH))
    sts = lax.fori_loop(0, nchunks, state, st0, unroll=min(GLR_UNROLL, nchunks))
    for j in range(GLR_NH):
        s_ref[0, j] = sts[j].T


def _gla(proj, wa2, ba, norm, s0, nb, t, c):
    pm = _prefix_matrix(c)
    rd = _diag_matrix(c, 2)
    nchunks = t // c
    has_s0 = s0 is not None
    npair = C_HEADS // 2
    w = GLR_NH * LANES

    def col(off, width):
        return pl.BlockSpec((t, width), lambda b, p: (b, off * LANES // width + p))

    st_spec = pl.BlockSpec((1, GLR_NH, LANES, C_DV), lambda b, p: (b, p, 0, 0))
    in_specs = [pl.BlockSpec(pm.shape, lambda b, p: (0, 0)),
                pl.BlockSpec(rd.shape, lambda b, p: (0, 0)),
                pl.BlockSpec((LANES, w), lambda b, p: (0, p)),
                pl.BlockSpec((1, w), lambda b, p: (0, p)),
                pl.BlockSpec((1, LANES), lambda b, p: (0, 0)),
                col(COL_C_Q, w), col(COL_C_K, w), col(COL_C_V, 2 * w), col(COL_C_R, 2 * w),
                pl.BlockSpec((t, LANES), lambda b, p: (b, COL_C_A))]
    args = [pm, rd, wa2, ba.reshape(1, -1), norm.reshape(1, LANES), proj, proj, proj, proj, proj]
    if has_s0:
        in_specs.append(st_spec)
        args.append(s0.reshape(nb, npair, 2 * C_DK, C_DV))
    o, s = pl.pallas_call(
        functools.partial(_gla_kernel, c=c, nchunks=nchunks, has_s0=has_s0),
        grid=(nb, npair // GLR_NH),
        in_specs=in_specs,
        out_specs=[pl.BlockSpec((t, 2 * w), lambda b, p: (b, p)), st_spec],
        out_shape=[jax.ShapeDtypeStruct((nb * t, C_WIDTH), BF16),
                   jax.ShapeDtypeStruct((nb, npair, 2 * C_DK, C_DV), F32)],
        scratch_shapes=_glr_scratch(t, c, 2),
        compiler_params=_cparams(("parallel", "parallel")),
        name="gla",
    )(*args)
    return o, s.reshape(nb, C_HEADS, C_DK, C_DV)


def _rope_tables(pos):
    half = B_HD // 2
    inv = ROPE_THETA ** (-jnp.arange(half, dtype=F32) / half)
    ang = pos.astype(F32)[:, None] * inv[None, :]
    cos = jnp.cos(ang)
    sin = jnp.sin(ang)
    cos_t = jnp.tile(cos, (1, 2 * LANES // B_HD))
    sin_t = jnp.tile(jnp.concatenate([-sin, sin], axis=-1), (1, LANES // B_HD))
    return cos_t, sin_t


def _rope(x, cos_t, sin_t):
    n = x.shape[1]
    half = B_HD // 2
    lane = lax.broadcasted_iota(jnp.int32, (1, n), 1)
    first = (lane & (B_HD - 1)) < half
    rot = jnp.where(first, pltpu.roll(x, n - half, axis=1), pltpu.roll(x, half, axis=1))
    reps = n // LANES
    if reps > 1:
        cos_t = jnp.concatenate([cos_t] * reps, axis=1)
        sin_t = jnp.concatenate([sin_t] * reps, axis=1)
    return x * cos_t + rot * sin_t


def _attend(qs, kb, vb, sink_col, valid):
    s = _dot_nt(qs, kb) * B_SCALE
    if valid is not None:
        s = jnp.where(valid, s, NEG_BIG)
    m = jnp.maximum(jnp.max(s, axis=-1, keepdims=True), sink_col)
    p = jnp.exp(s - m)
    den = jnp.sum(p, axis=-1, keepdims=True) + jnp.exp(sink_col - m)
    return _dot(p.astype(BF16), vb) * (1.0 / den)


def _sink_column(sink_ref, base, t):
    rows = lax.broadcasted_iota(jnp.int32, (B_GROUP * t, 1), 0)
    col = jnp.zeros((B_GROUP * t, 1), F32)
    for g in range(B_GROUP):
        col = jnp.where((rows >= g * t) & (rows < (g + 1) * t), sink_ref[base + g], col)
    return col


def _swa_prompt_kernel(sink_ref, q_ref, k_ref, v_ref, cq_ref, sq_ref, ck_ref, sk_ref,
                       o_ref, kr_ref, vr_ref, kro_ref, *, t, layer):
    khp = pl.program_id(1)
    cidx = pl.program_id(2)

    @pl.when(cidx == 0)
    def _():
        kr = _rope(k_ref[...], ck_ref[...], sk_ref[...])
        kro_ref[...] = kr.astype(BF16)
        kr_ref[0] = kr[t - WINDOW:, :]
        vr_ref[0] = v_ref[t - WINDOW:, :]

    band = (WINDOW_CHUNKS + 1) * CHUNK
    s0 = pl.multiple_of(jnp.maximum(cidx - WINDOW_CHUNKS, 0) * CHUNK, CHUNK)
    qr = _rope(q_ref[...], cq_ref[...], sq_ref[...]).astype(BF16)
    kband = kro_ref[pl.ds(s0, band), :]
    vband = v_ref[pl.ds(s0, band), :].astype(BF16)
    key_pos = s0 + lax.broadcasted_iota(jnp.int32, (1, band), 1)
    valid = key_pos < (cidx + 1) * CHUNK
    for kv in range(2):
        base = kv * B_GROUP * B_HD
        qs = jnp.concatenate(
            [qr[:, base + g * B_HD:base + (g + 1) * B_HD] for g in range(B_GROUP)], axis=0)
        sink_col = _sink_column(sink_ref, layer * B_HEADS + (khp * 2 + kv) * B_GROUP, CHUNK)
        o = _attend(qs, kband[:, kv * B_HD:(kv + 1) * B_HD], vband[:, kv * B_HD:(kv + 1) * B_HD],
                    sink_col, valid)
        for g in range(0, B_GROUP, 2):
            o_ref[:, base + g * B_HD:base + (g + 2) * B_HD] = jnp.concatenate(
                [o[g * CHUNK:(g + 1) * CHUNK], o[(g + 1) * CHUNK:(g + 2) * CHUNK]],
                axis=1).astype(o_ref.dtype)


def _swa_prompt(proj, sinks, layer, nb, t):
    nc = t // CHUNK
    cos_t, sin_t = _rope_tables(jnp.arange(t))
    qw = 2 * B_GROUP * B_HD
    qoff = COL_B_Q * LANES // qw
    kernel = functools.partial(_swa_prompt_kernel, t=t, layer=layer)
    return pl.pallas_call(
        kernel,
        grid=(nb, B_KV_HEADS // 2, nc),
        in_specs=[pl.BlockSpec(memory_space=pltpu.SMEM),
                  pl.BlockSpec((CHUNK, qw), lambda b, p, c: (b * nc + c, qoff + p)),
                  pl.BlockSpec((t, LANES), lambda b, p, c: (b, COL_B_K + p)),
                  pl.BlockSpec((t, LANES), lambda b, p, c: (b, COL_B_V + p)),
                  pl.BlockSpec((CHUNK, LANES), lambda b, p, c: (c, 0)),
                  pl.BlockSpec((CHUNK, LANES), lambda b, p, c: (c, 0)),
                  pl.BlockSpec((t, LANES), lambda b, p, c: (0, 0)),
                  pl.BlockSpec((t, LANES), lambda b, p, c: (0, 0))],
        out_specs=[pl.BlockSpec((CHUNK, qw), lambda b, p, c: (b * nc + c, p)),
                   pl.BlockSpec((1, WINDOW, LANES), lambda b, p, c: (b, 0, p)),
                   pl.BlockSpec((1, WINDOW, LANES), lambda b, p, c: (b, 0, p))],
        out_shape=[jax.ShapeDtypeStruct((nb * t, B_WIDTH), BF16),
                   jax.ShapeDtypeStruct((nb, WINDOW, B_KV_HEADS * B_HD), F32),
                   jax.ShapeDtypeStruct((nb, WINDOW, B_KV_HEADS * B_HD), F32)],
        scratch_shapes=[pltpu.VMEM((t, LANES), BF16)],
        compiler_params=_cparams(("parallel", "parallel", "arbitrary")),
        name="swa_prompt",
    )(sinks.reshape(-1), proj, proj, proj, cos_t, sin_t, cos_t, sin_t)


def _swa_sample_kernel(sink_ref, q_ref, k_ref, v_ref, ckc_ref, cvc_ref, cos_ref, sin_ref,
                       o_ref, kr_ref, vr_ref, *, t, layer):
    cos_t = cos_ref[...]
    sin_t = sin_ref[...]
    kr = _rope(k_ref[...], cos_t, sin_t)
    v = v_ref[...]
    kr_ref[...] = kr
    vr_ref[...] = v
    qr = _rope(q_ref[...], cos_t, sin_t).astype(BF16)
    ka = jnp.concatenate([ckc_ref[0], kr], axis=0).astype(BF16)
    va = jnp.concatenate([cvc_ref[0], v], axis=0).astype(BF16)
    for kh in range(B_KV_HEADS):
        base = kh * B_GROUP * B_HD
        qs = jnp.concatenate(
            [qr[:, base + g * B_HD:base + (g + 1) * B_HD] for g in range(B_GROUP)], axis=0)
        sink_col = _sink_column(sink_ref, layer * B_HEADS + kh * B_GROUP, t)
        o = _attend(qs, ka[:, kh * B_HD:(kh + 1) * B_HD], va[:, kh * B_HD:(kh + 1) * B_HD],
                    sink_col, None)
        for g in range(0, B_GROUP, 2):
            o_ref[:, base + g * B_HD:base + (g + 2) * B_HD] = jnp.concatenate(
                [o[g * t:(g + 1) * t], o[(g + 1) * t:(g + 2) * t]], axis=1).astype(o_ref.dtype)


def _swa_sample(proj, cache_k, cache_v, sinks, layer, nb, t):
    cos_t, sin_t = _rope_tables(PAST_LEN + jnp.arange(t))
    kvw = B_KV_HEADS * B_HD
    kernel = functools.partial(_swa_sample_kernel, t=t, layer=layer)
    return pl.pallas_call(
        kernel,
        grid=(nb,),
        in_specs=[pl.BlockSpec(memory_space=pltpu.SMEM),
                  pl.BlockSpec((t, B_WIDTH), lambda b: (b, COL_B_Q * LANES // B_WIDTH)),
                  pl.BlockSpec((t, kvw), lambda b: (b, COL_B_K * LANES // kvw)),
                  pl.BlockSpec((t, kvw), lambda b: (b, COL_B_V * LANES // kvw)),
                  pl.BlockSpec((1, WINDOW, kvw), lambda b: (b, 0, 0)),
                  pl.BlockSpec((1, WINDOW, kvw), lambda b: (b, 0, 0)),
                  pl.BlockSpec((t, LANES), lambda b: (0, 0)),
                  pl.BlockSpec((t, LANES), lambda b: (0, 0))],
        out_specs=[pl.BlockSpec((t, B_WIDTH), lambda b: (b, 0)),
                   pl.BlockSpec((t, kvw), lambda b: (b, 0)),
                   pl.BlockSpec((t, kvw), lambda b: (b, 0))],
        out_shape=[jax.ShapeDtypeStruct((nb * t, B_WIDTH), BF16),
                   jax.ShapeDtypeStruct((nb * t, kvw), F32),
                   jax.ShapeDtypeStruct((nb * t, kvw), F32)],
        compiler_params=_cparams(("parallel",)),
        name="swa_sample",
    )(sinks.reshape(-1), proj, proj, proj,
      cache_k.reshape(nb, WINDOW, kvw), cache_v.reshape(nb, WINDOW, kvw), cos_t, sin_t)


def _layer_sample(x, w, layer, nb, t, cache, s_a, s_c, ff_pad):
    m, d = x.shape
    h = _rmsnorm(x, w["norm_mix"][layer], BF16, m)
    proj, wb_in = _mm_in_cast(h, w["w_in"], layer, IN_COLS_PAD, MM_TN_IN)
    o_a, s_a_new = _hgrn(proj, w["lb_logits"], w["hgrn_norm"][layer], s_a, layer, nb, t, t)
    o_b, k_rows, v_rows = _swa_sample(proj, cache[0], cache[1], w["sinks"], layer, nb, t)
    o_c, s_c_new = _gla(proj, w["w_a2"][layer], w["b_a"][layer], w["gla_norm"][layer],
                        s_c, nb, t, t)
    x, wb_out = _mm_out_cast(o_a, o_b, o_c, w["w_out"], x, layer, min(d, 512))
    h = _rmsnorm(x, w["norm_ffn"][layer], BF16, m)
    mid, wb_gate, wb_up = _gate_up_cast(h, w["w_gate_up"], layer, ff_pad)
    x, wb_down = _down_cast(mid, w["w_down"], x, layer)
    outs = (k_rows.reshape(nb, t, B_KV_HEADS, B_HD), v_rows.reshape(nb, t, B_KV_HEADS, B_HD),
            s_a_new, s_c_new)
    return x, outs, dict(w_in=wb_in, w_out=wb_out, w_gate=wb_gate, w_up=wb_up, w_down=wb_down)


def _layer_prompt(x, w, wb, layer, nb, t):
    m, d = x.shape
    tm = min(m, 1024)
    h = _rmsnorm(x, w["norm_mix"][layer], BF16, min(m, 256))
    proj = _matmul(h, wb["w_in"], tm, MM_TN_IN, F32)
    o_a, s_a_new = _hgrn(proj, w["lb_logits"], w["hgrn_norm"][layer], None, layer, nb, t, CHUNK)
    o_b, k_rows, v_rows = _swa_prompt(proj, w["sinks"], layer, nb, t)
    o_c, s_c_new = _gla(proj, w["w_a2"][layer], w["b_a"][layer], w["gla_norm"][layer],
                        None, nb, t, CHUNK)
    x = _matmul_out(o_a, o_b, o_c, wb["w_out"], x, tm, min(d, 512))
    h = _rmsnorm(x, w["norm_ffn"][layer], BF16, min(m, 256))
    mid = _matmul_gate_up(h, wb["w_gate"], wb["w_up"], tm, 512)
    x = _matmul_down(mid, wb["w_down"], x, tm, min(d, 1024), wb["w_down"].shape[0] // 4)
    outs = (k_rows.reshape(nb, WINDOW, B_KV_HEADS, B_HD),
            v_rows.reshape(nb, WINDOW, B_KV_HEADS, B_HD), s_a_new, s_c_new)
    return x, outs


def kernel(x_prompt, x_sample, cache_k_swa, cache_v_swa, state_hgrn, state_gla, norm_mix, w_in,
           hgrn_lb_logits, hgrn_norm, swa_sinks, gla_w_alpha2, gla_b_alpha, gla_norm, w_out,
           norm_ffn, w_gate_up, w_down, norm_final):
    n_p, t_p, d = x_prompt.shape
    n_s, t_s, _ = x_sample.shape
    d_ff = w_down.shape[1]
    ff_pad = -(-d_ff // 1024) * 1024
    w = dict(
        norm_mix=norm_mix, norm_ffn=norm_ffn, hgrn_norm=hgrn_norm, gla_norm=gla_norm,
        lb_logits=hgrn_lb_logits, sinks=swa_sinks, b_a=gla_b_alpha,
        w_in=w_in, w_out=w_out, w_gate_up=w_gate_up, w_down=w_down,
        w_a2=jnp.pad(gla_w_alpha2, ((0, 0), (0, LANES - GLA_RANK), (0, 0))).astype(BF16),
    )
    xp = x_prompt.reshape(n_p * t_p, d)
    xs = x_sample.reshape(n_s * t_s, d)
    outs_p, outs_s = [], []
    for layer in range(DEPTH):
        xs, rest, wb = _layer_sample(xs, w, layer, n_s, t_s,
                                     (cache_k_swa[layer], cache_v_swa[layer]),
                                     state_hgrn[layer], state_gla[layer], ff_pad)
        outs_s.append(rest)
        xp, rest = _layer_prompt(xp, w, wb, layer, n_p, t_p)
        outs_p.append(rest)
    y_p = _rmsnorm(xp, norm_final, F32, min(xp.shape[0], 256)).reshape(n_p, t_p, d)
    y_s = _rmsnorm(xs, norm_final, F32, min(xs.shape[0], 256)).reshape(n_s, t_s, d)
    stack = lambda outs, i: jnp.stack([o[i] for o in outs])
    return (y_p, y_s,
            stack(outs_p, 0), stack(outs_p, 1), stack(outs_p, 2), stack(outs_p, 3),
            stack(outs_s, 0), stack(outs_s, 1), stack(outs_s, 2), stack(outs_s, 3))
```

```python
import functools

import jax
import jax.numpy as jnp
import numpy as np
from jax import lax
from jax.experimental import pallas as pl
from jax.experimental.pallas import tpu as pltpu

F32 = jnp.float32
BF16 = jnp.bfloat16

DEPTH = 2
PAST_LEN = 4096
CHUNK = 64
EPS = 1e-6
NEG_BIG = -1e30
LB_FLOOR = 1e-30
A_HEADS = 8
A_DK = 128
A_DV = 128
A_WIDTH = A_HEADS * A_DV
B_HEADS = 32
B_KV_HEADS = 4
B_GROUP = B_HEADS // B_KV_HEADS
B_HD = 64
B_WIDTH = B_HEADS * B_HD
B_SCALE = B_HD ** -0.5
WINDOW = 128
WINDOW_CHUNKS = WINDOW // CHUNK
ROPE_THETA = 10000.0
C_HEADS = 8
C_DK = 64
C_DV = 128
C_WIDTH = C_HEADS * C_DV
GLA_RANK = 16
GLA_NORMALIZER = 16.0

LANES = 128

COL_A_Q = 0
COL_A_F = 8
COL_A_I = 16
COL_A_G = 24
COL_B_Q = 32
COL_B_K = 48
COL_B_V = 50
COL_C_Q = 52
COL_C_K = 56
COL_C_V = 60
COL_C_R = 68
COL_C_A = 76
IN_COLS = 9744
IN_COLS_PAD = 9984
MM_TN_IN = 768
VMEM_LIMIT = 56 * 1024 * 1024


def _cparams(sem):
    return pltpu.CompilerParams(dimension_semantics=sem, vmem_limit_bytes=VMEM_LIMIT)


def _dot(a, b):
    return jnp.dot(a, b, preferred_element_type=F32)


def _dot_nt(a, b):
    return lax.dot_general(a, b, (((1,), (1,)), ((), ())), preferred_element_type=F32)


def _dot_tn(a, b):
    return lax.dot_general(a, b, (((0,), (0,)), ((), ())), preferred_element_type=F32)


def _rmsnorm_kernel(x_ref, g_ref, o_ref):
    x = x_ref[...]
    var = jnp.mean(x * x, axis=-1, keepdims=True)
    o_ref[...] = (x * lax.rsqrt(var + EPS) * g_ref[...]).astype(o_ref.dtype)


def _rmsnorm(x, g, out_dtype, tm):
    m, d = x.shape
    return pl.pallas_call(
        _rmsnorm_kernel,
        grid=(m // tm,),
        in_specs=[pl.BlockSpec((tm, d), lambda i: (i, 0)),
                  pl.BlockSpec((1, d), lambda i: (0, 0))],
        out_specs=pl.BlockSpec((tm, d), lambda i: (i, 0)),
        out_shape=jax.ShapeDtypeStruct((m, d), out_dtype),
        compiler_params=_cparams(("parallel",)),
        name="rmsnorm",
    )(x, g.reshape(1, d))


def _mm_kernel(x_ref, w_ref, o_ref):
    o_ref[...] = _dot(x_ref[...], w_ref[...]).astype(o_ref.dtype)


def _matmul(x, w, tm, tn, out_dtype):
    m, k = x.shape
    n = w.shape[1]
    return pl.pallas_call(
        _mm_kernel,
        grid=(m // tm, n // tn),
        in_specs=[pl.BlockSpec((tm, k), lambda i, j: (i, 0)),
                  pl.BlockSpec((k, tn), lambda i, j: (0, j))],
        out_specs=pl.BlockSpec((tm, tn), lambda i, j: (i, j)),
        out_shape=jax.ShapeDtypeStruct((m, n), out_dtype),
        compiler_params=_cparams(("parallel", "arbitrary")),
        name="mm_in",
    )(x, w)


def _mm_out_kernel(oa_ref, ob_ref, oc_ref, w_ref, r_ref, o_ref, cat_ref):
    @pl.when(pl.program_id(1) == 0)
    def _():
        cat_ref[:, 0:A_WIDTH] = oa_ref[...]
        cat_ref[:, A_WIDTH:A_WIDTH + B_WIDTH] = ob_ref[...]
        cat_ref[:, A_WIDTH + B_WIDTH:] = oc_ref[...]

    o_ref[...] = r_ref[...] + _dot(cat_ref[...], w_ref[...])


def _matmul_out(oa, ob, oc, w, resid, tm, tn):
    m = oa.shape[0]
    k, n = w.shape
    return pl.pallas_call(
        _mm_out_kernel,
        grid=(m // tm, n // tn),
        in_specs=[pl.BlockSpec((tm, A_WIDTH), lambda i, j: (i, 0)),
                  pl.BlockSpec((tm, B_WIDTH), lambda i, j: (i, 0)),
                  pl.BlockSpec((tm, C_WIDTH), lambda i, j: (i, 0)),
                  pl.BlockSpec((k, tn), lambda i, j: (0, j)),
                  pl.BlockSpec((tm, tn), lambda i, j: (i, j))],
        out_specs=pl.BlockSpec((tm, tn), lambda i, j: (i, j)),
        out_shape=jax.ShapeDtypeStruct((m, n), F32),
        scratch_shapes=[pltpu.VMEM((tm, k), BF16)],
        compiler_params=_cparams(("parallel", "arbitrary")),
        name="mm_out",
    )(oa, ob, oc, w, resid)


def _mm_gate_up_kernel(x_ref, wg_ref, wu_ref, o_ref):
    x = x_ref[...]
    gate = _dot(x, wg_ref[...])
    up = _dot(x, wu_ref[...])
    o_ref[...] = (gate * (1.0 / (1.0 + jnp.exp(-gate))) * up).astype(o_ref.dtype)


def _matmul_gate_up(x, wg, wu, tm, tn):
    m, k = x.shape
    n = wg.shape[1]
    return pl.pallas_call(
        _mm_gate_up_kernel,
        grid=(m // tm, n // tn),
        in_specs=[pl.BlockSpec((tm, k), lambda i, j: (i, 0)),
                  pl.BlockSpec((k, tn), lambda i, j: (0, j)),
                  pl.BlockSpec((k, tn), lambda i, j: (0, j))],
        out_specs=pl.BlockSpec((tm, tn), lambda i, j: (i, j)),
        out_shape=jax.ShapeDtypeStruct((m, n), BF16),
        compiler_params=_cparams(("parallel", "arbitrary")),
        name="mm_gate_up",
    )(x, wg, wu)


def _mm_down_kernel(x_ref, w_ref, r_ref, o_ref):
    kk = pl.program_id(2)

    @pl.when(kk == 0)
    def _():
        o_ref[...] = r_ref[...] + _dot(x_ref[...], w_ref[...])

    @pl.when(kk != 0)
    def _():
        o_ref[...] += _dot(x_ref[...], w_ref[...])


def _matmul_down(x, w, resid, tm, tn, tk):
    m, k = x.shape
    n = w.shape[1]
    return pl.pallas_call(
        _mm_down_kernel,
        grid=(m // tm, n // tn, k // tk),
        in_specs=[pl.BlockSpec((tm, tk), lambda i, j, kk: (i, kk)),
                  pl.BlockSpec((tk, tn), lambda i, j, kk: (kk, j)),
                  pl.BlockSpec((tm, tn), lambda i, j, kk: (i, j))],
        out_specs=pl.BlockSpec((tm, tn), lambda i, j, kk: (i, j)),
        out_shape=jax.ShapeDtypeStruct((m, n), F32),
        compiler_params=_cparams(("parallel", "parallel", "arbitrary")),
        name="mm_down",
    )(x, w, resid)


CAST_TN_GATE = 256
CAST_TK_DOWN = 512


def _masked_bf16(w, first, n_valid, axis):
    idx = first + lax.broadcasted_iota(jnp.int32, w.shape, axis)
    return jnp.where(idx < n_valid, w, 0.0).astype(BF16)


def _mm_in_cast_kernel(x_ref, w_ref, o_ref, wb_ref, *, n_valid, tn):
    idx = pl.program_id(0) * tn + lax.broadcasted_iota(jnp.int32, w_ref.shape, 0)
    wb = jnp.where(idx < n_valid, w_ref[...], 0.0).T.astype(BF16)
    wb_ref[...] = wb
    o_ref[...] = _dot(x_ref[...], wb)


def _mm_in_cast(x, w3, layer, n_pad, tn):
    m, k = x.shape
    n = w3.shape[2]
    return pl.pallas_call(
        functools.partial(_mm_in_cast_kernel, n_valid=n, tn=tn),
        grid=(n_pad // tn,),
        in_specs=[pl.BlockSpec((m, k), lambda j: (0, 0)),
                  pl.BlockSpec((None, tn, k), lambda j: (layer, j, 0))],
        out_specs=[pl.BlockSpec((m, tn), lambda j: (0, j)),
                   pl.BlockSpec((k, tn), lambda j: (0, j))],
        out_shape=[jax.ShapeDtypeStruct((m, n_pad), F32),
                   jax.ShapeDtypeStruct((k, n_pad), BF16)],
        compiler_params=_cparams(("arbitrary",)),
        name="mm_in_cast",
    )(x, jnp.swapaxes(w3, 1, 2))


def _mm_out_cast_kernel(oa_ref, ob_ref, oc_ref, w_ref, r_ref, o_ref, wb_ref, cat_ref):
    @pl.when(pl.program_id(0) == 0)
    def _():
        cat_ref[:, 0:A_WIDTH] = oa_ref[...]
        cat_ref[:, A_WIDTH:A_WIDTH + B_WIDTH] = ob_ref[...]
        cat_ref[:, A_WIDTH + B_WIDTH:] = oc_ref[...]

    wb = w_ref[...].astype(BF16)
    wb_ref[...] = wb
    o_ref[...] = r_ref[...] + _dot(cat_ref[...], wb)


def _mm_out_cast(oa, ob, oc, w3, resid, layer, tn):
    m = oa.shape[0]
    k, n = w3.shape[1:]
    return pl.pallas_call(
        _mm_out_cast_kernel,
        grid=(n // tn,),
        in_specs=[pl.BlockSpec((m, A_WIDTH), lambda j: (0, 0)),
                  pl.BlockSpec((m, B_WIDTH), lambda j: (0, 0)),
                  pl.BlockSpec((m, C_WIDTH), lambda j: (0, 0)),
                  pl.BlockSpec((None, k, tn), lambda j: (layer, 0, j)),
                  pl.BlockSpec((m, tn), lambda j: (0, j))],
        out_specs=[pl.BlockSpec((m, tn), lambda j: (0, j)),
                   pl.BlockSpec((k, tn), lambda j: (0, j))],
        out_shape=[jax.ShapeDtypeStruct((m, n), F32),
                   jax.ShapeDtypeStruct((k, n), BF16)],
        scratch_shapes=[pltpu.VMEM((m, k), BF16)],
        compiler_params=_cparams(("arbitrary",)),
        name="mm_out_cast",
    )(oa, ob, oc, w3, resid)


def _gate_up_cast_kernel(x_ref, wg_ref, wu_ref, o_ref, wgb_ref, wub_ref, *, n_tiles):
    valid = pl.program_id(0) < n_tiles
    wg = jnp.where(valid, wg_ref[...], 0.0).astype(BF16)
    wu = jnp.where(valid, wu_ref[...], 0.0).astype(BF16)
    wgb_ref[...] = wg
    wub_ref[...] = wu
    x = x_ref[...]
    gate = _dot(x, wg)
    o_ref[...] = (gate * (1.0 / (1.0 + jnp.exp(-gate))) * _dot(x, wu)).astype(o_ref.dtype)


def _gate_up_cast(x, wgu3, layer, ff_pad):
    m, k = x.shape
    tn = CAST_TN_GATE
    n_tiles = wgu3.shape[2] // 2 // tn
    assert n_tiles * tn * 2 == wgu3.shape[2]
    return pl.pallas_call(
        functools.partial(_gate_up_cast_kernel, n_tiles=n_tiles),
        grid=(ff_pad // tn,),
        in_specs=[pl.BlockSpec((m, k), lambda j: (0, 0)),
                  pl.BlockSpec((None, k, tn), lambda j: (layer, 0, jnp.minimum(j, n_tiles - 1))),
                  pl.BlockSpec((None, k, tn),
                               lambda j: (layer, 0, n_tiles + jnp.minimum(j, n_tiles - 1)))],
        out_specs=[pl.BlockSpec((m, tn), lambda j: (0, j)),
                   pl.BlockSpec((k, tn), lambda j: (0, j)),
                   pl.BlockSpec((k, tn), lambda j: (0, j))],
        out_shape=[jax.ShapeDtypeStruct((m, ff_pad), BF16),
                   jax.ShapeDtypeStruct((k, ff_pad), BF16),
                   jax.ShapeDtypeStruct((k, ff_pad), BF16)],
        compiler_params=_cparams(("arbitrary",)),
        name="gate_up_cast",
    )(x, wgu3, wgu3)


def _down_cast_kernel(x_ref, w_ref, r_ref, o_ref, wb_ref, *, k_valid, tk):
    kk = pl.program_id(0)
    wb = _masked_bf16(w_ref[...], kk * tk, k_valid, 0)
    wb_ref[...] = wb

    @pl.when(kk == 0)
    def _():
        o_ref[...] = r_ref[...] + _dot(x_ref[...], wb)

    @pl.when(kk != 0)
    def _():
        o_ref[...] += _dot(x_ref[...], wb)


def _down_cast(x, wd3, resid, layer):
    m, ff_pad = x.shape
    ff, d = wd3.shape[1:]
    tk = CAST_TK_DOWN
    last = (ff - 1) // tk
    return pl.pallas_call(
        functools.partial(_down_cast_kernel, k_valid=ff, tk=tk),
        grid=(ff_pad // tk,),
        in_specs=[pl.BlockSpec((m, tk), lambda kk: (0, kk)),
                  pl.BlockSpec((None, tk, d), lambda kk: (layer, jnp.minimum(kk, last), 0)),
                  pl.BlockSpec((m, d), lambda kk: (0, 0))],
        out_specs=[pl.BlockSpec((m, d), lambda kk: (0, 0)),
                   pl.BlockSpec((tk, d), lambda kk: (kk, 0))],
        out_shape=[jax.ShapeDtypeStruct((m, d), F32),
                   jax.ShapeDtypeStruct((ff_pad, d), BF16)],
        compiler_params=_cparams(("arbitrary",)),
        name="down_cast",
    )(x, wd3, resid)


LOG2E = 1.4426950408889634
GLR_D0 = 8
GLR_NH = 2
GLR_UNROLL = 4


def _levels(c):
    out, b = [], GLR_D0
    while b < c:
        out.append(b)
        b *= 2
    return out


def _prefix_matrix(c):
    t = np.arange(c)[:, None]
    s = np.arange(c)[None, :]
    low = s <= t
    same0 = (t // GLR_D0) == (s // GLR_D0)
    mats = [low, ~low, low & same0]
    for b in _levels(c):
        same = (t // b) == (s // b)
        odd = ((t // b) % 2) == 1
        mats.append(np.where(odd, low & same, (~low) & same))
    m = np.concatenate(mats, axis=0).astype(np.float32)
    return jnp.asarray(np.concatenate([m, m, m], axis=1), dtype=BF16)


def _diag_matrix(c, hl):
    kd = LANES // hl
    sp = np.arange(GLR_D0)[:, None, None]
    lane = np.arange(LANES)[None, :, None]
    col = np.arange(hl * c)[None, None, :]
    r = ((col // c) == (lane // kd)) & ((col % GLR_D0) == sp)
    return jnp.asarray(r.reshape(GLR_D0 * LANES, hl * c).astype(np.float32), dtype=BF16)


def _glr_consts(c, hl):
    t = lax.broadcasted_iota(jnp.int32, (c, hl * c), 0)
    s = lax.broadcasted_iota(jnp.int32, (c, hl * c), 1) & (c - 1)
    lvl_masks = []
    for b in _levels(c):
        lg = b.bit_length() - 1
        tb = lax.shift_right_logical(t, lg)
        sb = lax.shift_right_logical(s, lg)
        lvl_masks.append((lax.shift_right_logical(tb, 1) == lax.shift_right_logical(sb, 1))
                         & ((tb & 1) == 1) & ((sb & 1) == 0))
    lg0 = GLR_D0.bit_length() - 1
    diag_valid = ((lax.shift_right_logical(t, lg0) == lax.shift_right_logical(s, lg0))
                  & ((s & (GLR_D0 - 1)) <= (t & (GLR_D0 - 1))))
    lane = lax.broadcasted_iota(jnp.int32, (1, LANES), 1)
    head_masks = [(lane < C_DK).astype(F32), (lane >= C_DK).astype(F32)]
    return dict(lvl_masks=lvl_masks, diag_valid=diag_valid, head_masks=head_masks, lane=lane)


def _glr_prefix(lf2, pm):
    hi = lf2.astype(BF16)
    r1 = lf2 - hi.astype(F32)
    mid = r1.astype(BF16)
    lo = (r1 - mid.astype(F32)).astype(BF16)
    return _dot(pm, jnp.concatenate([hi, mid, lo], axis=0))


def _glr_intra(q, k, pref, rd, cst, *, c, hl):
    hm = cst["head_masks"]
    g_all = pref[0:c]
    qhat = (q * jnp.exp2(g_all)).astype(BF16)
    kdec = (k * jnp.exp2(pref[c:2 * c])).astype(BF16)
    gin0 = pref[2 * c:3 * c]

    nb = c // GLR_D0
    k3 = k.reshape(nb, GLR_D0, LANES)
    g3 = gin0.reshape(nb, GLR_D0, LANES)
    ps = []
    for sp in range(GLR_D0):
        kb = jnp.broadcast_to(k3[:, sp:sp + 1, :], (nb, GLR_D0, LANES)).reshape(c, LANES)
        gb = jnp.broadcast_to(g3[:, sp:sp + 1, :], (nb, GLR_D0, LANES)).reshape(c, LANES)
        ps.append((q * kb * jnp.exp2(jnp.minimum(gin0 - gb, 0.0))).astype(BF16))
    a = jnp.where(cst["diag_valid"], _dot(jnp.concatenate(ps, axis=1), rd), 0.0)
    for i in range(len(cst["lvl_masks"])):
        eb = jnp.exp2(pref[(3 + i) * c:(4 + i) * c])
        qb = (q * eb).astype(BF16)
        kb = k * eb
        if hl == 2:
            kb = jnp.concatenate([kb * hm[0], kb * hm[1]], axis=0)
        a = jnp.where(cst["lvl_masks"][i], _dot_nt(qb, kb.astype(BF16)), a)
    return a.astype(BF16), qhat, kdec, jnp.exp2(g_all[c - 1:c, :])


def _glr_state(a, qhat, kdec, dec, vs, st, cst, *, hl):
    stb = st.astype(BF16)
    first = cst["lane"] < C_DK
    outs, upd = [], []
    for h in range(hl):
        vb = vs[h].astype(BF16)
        if hl == 2:
            zero = jnp.zeros_like(vb)
            vz = jnp.concatenate([vb, zero] if h == 0 else [zero, vb], axis=0)
            qh = jnp.where(first if h == 0 else ~first, qhat, jnp.zeros_like(qhat))
        else:
            vz, qh = vb, qhat
        outs.append(_dot(a, vz) + _dot_nt(qh, stb))
        upd.append(_dot_tn(vb, kdec))
    u = jnp.where(first, upd[0], upd[1]) if hl == 2 else upd[0]
    return outs, st * dec + u


def _gated_norm(o, norm, gate):
    var = jnp.mean(o * o, axis=-1, keepdims=True)
    return o * lax.rsqrt(var + EPS) * norm * (gate * (1.0 / (1.0 + jnp.exp(-gate))))


def _glr_scratch(t, c, hl):
    w = GLR_NH * LANES
    return [pltpu.VMEM((GLR_NH, t, hl * c), BF16), pltpu.VMEM((t, w), BF16),
            pltpu.VMEM((t, w), BF16), pltpu.VMEM((t // c, GLR_D0, w), F32)]


def _hgrn_kernel(*refs, c, nchunks, layer, has_s0):
    if has_s0:
        (pm_ref, rd_ref, lbl_ref, norm_ref, q_ref, z_ref, v_ref, g_ref, s0_ref, o_ref, s_ref,
         a_s, qh_s, kd_s, dec_s) = refs
    else:
        (pm_ref, rd_ref, lbl_ref, norm_ref, q_ref, z_ref, v_ref, g_ref, o_ref, s_ref,
         a_s, qh_s, kd_s, dec_s) = refs
    logits = lbl_ref[...]
    e = jnp.exp(logits - jnp.max(logits, axis=0, keepdims=True))
    probs = e / jnp.sum(e, axis=0, keepdims=True)
    lb = jnp.sum(probs[0:layer + 1], axis=0, keepdims=True) - probs[0:1]
    lb_floor = jnp.maximum(lb, LB_FLOOR)
    oml = 1.0 - lb
    norm = norm_ref[...]
    pm = pm_ref[...]
    rd = rd_ref[...]
    cst = _glr_consts(c, 1)

    def intra(n, carry):
        rows = pl.ds(pl.multiple_of(n * c, c), c)
        q = q_ref[rows, :]
        z = z_ref[rows, :]
        ez = jnp.exp(-jnp.abs(z))
        r = 1.0 / (1.0 + ez)
        pos = z >= 0.0
        lf2 = jnp.log2(lb_floor + oml * jnp.where(pos, r, ez * r))
        k = oml * jnp.where(pos, ez * r, r)
        pref = _glr_prefix(lf2, pm)
        for j in range(GLR_NH):
            sl = slice(j * LANES, (j + 1) * LANES)
            a, qh, kd, dec = _glr_intra(q[:, sl], k[:, sl], pref[:, sl], rd, cst, c=c, hl=1)
            a_s[j, rows, :] = a
            qh_s[rows, sl] = qh
            kd_s[rows, sl] = kd
            dec_s[n, :, sl] = jnp.broadcast_to(dec, (GLR_D0, LANES))
        return carry

    lax.fori_loop(0, nchunks, intra, 0, unroll=min(GLR_UNROLL, nchunks))

    def state(n, sts):
        rows = pl.ds(pl.multiple_of(n * c, c), c)
        v = v_ref[rows, :]
        g = g_ref[rows, :]
        new, outs = [], []
        for j in range(GLR_NH):
            sl = slice(j * LANES, (j + 1) * LANES)
            o, st2 = _glr_state(a_s[j, rows, :], qh_s[rows, sl], kd_s[rows, sl],
                                dec_s[n, 0:1, sl], [v[:, sl]], sts[j], cst, hl=1)
            outs.append(_gated_norm(o[0], norm, g[:, sl]))
            new.append(st2)
        o_ref[rows, :] = jnp.concatenate(outs, axis=1).astype(o_ref.dtype)
        return tuple(new)

    if has_s0:
        st0 = tuple(s0_ref[0, j].T for j in range(GLR_NH))
    else:
        st0 = tuple(jnp.zeros((A_DV, A_DK), F32) for _ in range(GLR_NH))
    sts = lax.fori_loop(0, nchunks, state, st0, unroll=min(GLR_UNROLL, nchunks))
    for j in range(GLR_NH):
        s_ref[0, j] = sts[j].T


def _hgrn(proj, lbl, norm, s0, layer, nb, t, c):
    pm = _prefix_matrix(c)
    rd = _diag_matrix(c, 1)
    nchunks = t // c
    has_s0 = s0 is not None
    w = GLR_NH * LANES

    def col(off):
        return pl.BlockSpec((t, w), lambda b, h: (b, off // GLR_NH + h))

    st_spec = pl.BlockSpec((1, GLR_NH, A_DK, A_DV), lambda b, h: (b, h, 0, 0))
    in_specs = [pl.BlockSpec(pm.shape, lambda b, h: (0, 0)),
                pl.BlockSpec(rd.shape, lambda b, h: (0, 0)),
                pl.BlockSpec((DEPTH, w), lambda b, h: (0, h)),
                pl.BlockSpec((1, LANES), lambda b, h: (0, 0)),
                col(COL_A_Q), col(COL_A_F), col(COL_A_I), col(COL_A_G)]
    args = [pm, rd, lbl, norm.reshape(1, LANES), proj, proj, proj, proj]
    if has_s0:
        in_specs.append(st_spec)
        args.append(s0)
    return pl.pallas_call(
        functools.partial(_hgrn_kernel, c=c, nchunks=nchunks, layer=layer, has_s0=has_s0),
        grid=(nb, A_HEADS // GLR_NH),
        in_specs=in_specs,
        out_specs=[pl.BlockSpec((t, w), lambda b, h: (b, h)), st_spec],
        out_shape=[jax.ShapeDtypeStruct((nb * t, A_WIDTH), BF16),
                   jax.ShapeDtypeStruct((nb, A_HEADS, A_DK, A_DV), F32)],
        scratch_shapes=_glr_scratch(t, c, 1),
        compiler_params=_cparams(("parallel", "parallel")),
        name="hgrn",
    )(*args)


def _gla_kernel(*refs, c, nchunks, has_s0):
    if has_s0:
        (pm_ref, rd_ref, wa_ref, ba_ref, norm_ref, q_ref, k_ref, v_ref, r_ref, ca_ref,
         s0_ref, o_ref, s_ref, a_s, qh_s, kd_s, dec_s) = refs
    else:
        (pm_ref, rd_ref, wa_ref, ba_ref, norm_ref, q_ref, k_ref, v_ref, r_ref, ca_ref,
         o_ref, s_ref, a_s, qh_s, kd_s, dec_s) = refs
    wa = wa_ref[...]
    ba = ba_ref[...]
    norm = norm_ref[...]
    pm = pm_ref[...]
    rd = rd_ref[...]
    cst = _glr_consts(c, 2)

    def intra(n, carry):
        rows = pl.ds(pl.multiple_of(n * c, c), c)
        x = _dot(ca_ref[rows, :].astype(BF16), wa) + ba
        lf2 = (jnp.minimum(x, 0.0) - jnp.log1p(jnp.exp(-jnp.abs(x)))) * (LOG2E / GLA_NORMALIZER)
        q = q_ref[rows, :] * (C_DK ** -0.5)
        k = k_ref[rows, :]
        pref = _glr_prefix(lf2, pm)
        for j in range(GLR_NH):
            sl = slice(j * LANES, (j + 1) * LANES)
            a, qh, kd, dec = _glr_intra(q[:, sl], k[:, sl], pref[:, sl], rd, cst, c=c, hl=2)
            a_s[j, rows, :] = a
            qh_s[rows, sl] = qh
            kd_s[rows, sl] = kd
            dec_s[n, :, sl] = jnp.broadcast_to(dec, (GLR_D0, LANES))
        return carry

    lax.fori_loop(0, nchunks, intra, 0, unroll=min(GLR_UNROLL, nchunks))

    def state(n, sts):
        rows = pl.ds(pl.multiple_of(n * c, c), c)
        v = v_ref[rows, :]
        gate = r_ref[rows, :]
        new, outs = [], []
        for j in range(GLR_NH):
            sl = slice(j * LANES, (j + 1) * LANES)
            hs = [slice((2 * j + h) * C_DV, (2 * j + h + 1) * C_DV) for h in range(2)]
            o, st2 = _glr_state(a_s[j, rows, :], qh_s[rows, sl], kd_s[rows, sl],
                                dec_s[n, 0:1, sl], [v[:, hs[0]], v[:, hs[1]]], sts[j], cst, hl=2)
            outs += [_gated_norm(o[h], norm, gate[:, hs[h]]) for h in range(2)]
            new.append(st2)
        o_ref[rows, :] = jnp.concatenate(outs, axis=1).astype(o_ref.dtype)
        return tuple(new)

    if has_s0:
        st0 = tuple(s0_ref[0, j].T for j in range(GLR_NH))
    else:
        st0 = tuple(jnp.zeros((C_DV, LANES), F32) for _ in range(GLR_NH))
    sts = lax.fori_loop(0, nchunks, state, st0, unroll=min(GLR_UNROLL, nchunks))
    for j in range(GLR_NH):
        s_ref[0, j] = sts[j].T


def _gla(proj, wa2, ba, norm, s0, nb, t, c):
    pm = _prefix_matrix(c)
    rd = _diag_matrix(c, 2)
    nchunks = t // c
    has_s0 = s0 is not None
    npair = C_HEADS // 2
    w = GLR_NH * LANES

    def col(off, width):
        return pl.BlockSpec((t, width), lambda b, p: (b, off * LANES // width + p))

    st_spec = pl.BlockSpec((1, GLR_NH, LANES, C_DV), lambda b, p: (b, p, 0, 0))
    in_specs = [pl.BlockSpec(pm.shape, lambda b, p: (0, 0)),
                pl.BlockSpec(rd.shape, lambda b, p: (0, 0)),
                pl.BlockSpec((LANES, w), lambda b, p: (0, p)),
                pl.BlockSpec((1, w), lambda b, p: (0, p)),
                pl.BlockSpec((1, LANES), lambda b, p: (0, 0)),
                col(COL_C_Q, w), col(COL_C_K, w), col(COL_C_V, 2 * w), col(COL_C_R, 2 * w),
                pl.BlockSpec((t, LANES), lambda b, p: (b, COL_C_A))]
    args = [pm, rd, wa2, ba.reshape(1, -1), norm.reshape(1, LANES), proj, proj, proj, proj, proj]
    if has_s0:
        in_specs.append(st_spec)
        args.append(s0.reshape(nb, npair, 2 * C_DK, C_DV))
    o, s = pl.pallas_call(
        functools.partial(_gla_kernel, c=c, nchunks=nchunks, has_s0=has_s0),
        grid=(nb, npair // GLR_NH),
        in_specs=in_specs,
        out_specs=[pl.BlockSpec((t, 2 * w), lambda b, p: (b, p)), st_spec],
        out_shape=[jax.ShapeDtypeStruct((nb * t, C_WIDTH), BF16),
                   jax.ShapeDtypeStruct((nb, npair, 2 * C_DK, C_DV), F32)],
        scratch_shapes=_glr_scratch(t, c, 2),
        compiler_params=_cparams(("parallel", "parallel")),
        name="gla",
    )(*args)
    return o, s.reshape(nb, C_HEADS, C_DK, C_DV)


SWA_CB = 4
Q_SCALE = B_SCALE * 1.4426950408889634


def _rope_tables(pos):
    half = B_HD // 2
    inv = ROPE_THETA ** (-jnp.arange(half, dtype=F32) / half)
    ang = pos.astype(F32)[:, None] * inv[None, :]
    cos = jnp.cos(ang)
    sin = jnp.sin(ang)
    cos_t = jnp.tile(cos, (1, 2 * LANES // B_HD))
    sin_t = jnp.tile(jnp.concatenate([-sin, sin], axis=-1), (1, LANES // B_HD))
    return cos_t, sin_t


def _rope(x, cos_t, sin_t):
    n = x.shape[1]
    half = B_HD // 2
    lane = lax.broadcasted_iota(jnp.int32, (1, n), 1)
    first = (lane & (B_HD - 1)) < half
    rot = jnp.where(first, pltpu.roll(x, n - half, axis=1), pltpu.roll(x, half, axis=1))
    reps = n // LANES
    if reps > 1:
        cos_t = jnp.concatenate([cos_t] * reps, axis=1)
        sin_t = jnp.concatenate([sin_t] * reps, axis=1)
    return x * cos_t + rot * sin_t


def _attend(qs, kb, vb, sink_row, valid):
    s = _dot_nt(kb, qs)
    if valid is not None:
        s = jnp.where(valid, s, NEG_BIG)
    sink2 = sink_row * LOG2E
    m = jnp.maximum(jnp.max(s, axis=0, keepdims=True), sink2)
    p = jnp.exp2(s - m)
    den = jnp.sum(p, axis=0, keepdims=True) + jnp.exp2(sink2 - m)
    o_t = _dot_tn(vb, p.astype(BF16)) * (1.0 / den)
    return o_t.T


def _sink_row(sink_ref, base, t):
    lanes = lax.broadcasted_iota(jnp.int32, (1, B_GROUP * t), 1)
    row = jnp.zeros((1, B_GROUP * t), F32)
    for g in range(B_GROUP):
        row = jnp.where((lanes >= g * t) & (lanes < (g + 1) * t), sink_ref[base + g], row)
    return row


def _swa_prompt_kernel(sink_ref, q_ref, k_ref, v_ref, cq_ref, sq_ref, ck_ref, sk_ref,
                       o_ref, kr_ref, vr_ref, kro_ref, vbo_ref, *, t, layer):
    khp = pl.program_id(1)
    step = pl.program_id(2)

    @pl.when(step == 0)
    def _():
        kr = _rope(k_ref[...], ck_ref[...], sk_ref[...])
        kro_ref[...] = kr.astype(BF16)
        vbo_ref[...] = v_ref[...].astype(BF16)
        kr_ref[0] = kr[t - WINDOW:, :]
        vr_ref[0] = v_ref[t - WINDOW:, :]

    band = (WINDOW_CHUNKS + 1) * CHUNK
    sink_rows = [_sink_row(sink_ref, layer * B_HEADS + (khp * 2 + kv) * B_GROUP, CHUNK)
                 for kv in range(2)]
    for ci in range(SWA_CB):
        cidx = step * SWA_CB + ci
        rows = slice(ci * CHUNK, (ci + 1) * CHUNK)
        s0 = pl.multiple_of(jnp.maximum(cidx - WINDOW_CHUNKS, 0) * CHUNK, CHUNK)
        qr = _rope(q_ref[rows, :], cq_ref[rows, :], sq_ref[rows, :]).astype(BF16)
        kband = kro_ref[pl.ds(s0, band), :]
        vband = vbo_ref[pl.ds(s0, band), :]
        key_pos = s0 + lax.broadcasted_iota(jnp.int32, (band, 1), 0)
        valid = key_pos < (cidx + 1) * CHUNK
        for kv in range(2):
            base = kv * B_GROUP * B_HD
            qs = jnp.concatenate(
                [qr[:, base + g * B_HD:base + (g + 1) * B_HD] for g in range(B_GROUP)], axis=0)
            o = _attend(qs, kband[:, kv * B_HD:(kv + 1) * B_HD],
                        vband[:, kv * B_HD:(kv + 1) * B_HD], sink_rows[kv], valid)
            for g in range(0, B_GROUP, 2):
                o_ref[rows, base + g * B_HD:base + (g + 2) * B_HD] = jnp.concatenate(
                    [o[g * CHUNK:(g + 1) * CHUNK], o[(g + 1) * CHUNK:(g + 2) * CHUNK]],
                    axis=1).astype(o_ref.dtype)


def _swa_prompt(proj, sinks, layer, nb, t):
    rb = SWA_CB * CHUNK
    nc = t // rb
    cos_t, sin_t = _rope_tables(jnp.arange(t))
    qw = 2 * B_GROUP * B_HD
    qoff = COL_B_Q * LANES // qw
    kernel = functools.partial(_swa_prompt_kernel, t=t, layer=layer)
    return pl.pallas_call(
        kernel,
        grid=(nb, B_KV_HEADS // 2, nc),
        in_specs=[pl.BlockSpec(memory_space=pltpu.SMEM),
                  pl.BlockSpec((rb, qw), lambda b, p, c: (b * nc + c, qoff + p)),
                  pl.BlockSpec((t, LANES), lambda b, p, c: (b, COL_B_K + p)),
                  pl.BlockSpec((t, LANES), lambda b, p, c: (b, COL_B_V + p)),
                  pl.BlockSpec((rb, LANES), lambda b, p, c: (c, 0)),
                  pl.BlockSpec((rb, LANES), lambda b, p, c: (c, 0)),
                  pl.BlockSpec((t, LANES), lambda b, p, c: (0, 0)),
                  pl.BlockSpec((t, LANES), lambda b, p, c: (0, 0))],
        out_specs=[pl.BlockSpec((rb, qw), lambda b, p, c: (b * nc + c, p)),
                   pl.BlockSpec((1, WINDOW, LANES), lambda b, p, c: (b, 0, p)),
                   pl.BlockSpec((1, WINDOW, LANES), lambda b, p, c: (b, 0, p))],
        out_shape=[jax.ShapeDtypeStruct((nb * t, B_WIDTH), BF16),
                   jax.ShapeDtypeStruct((nb, WINDOW, B_KV_HEADS * B_HD), F32),
                   jax.ShapeDtypeStruct((nb, WINDOW, B_KV_HEADS * B_HD), F32)],
        scratch_shapes=[pltpu.VMEM((t, LANES), BF16), pltpu.VMEM((t, LANES), BF16)],
        compiler_params=_cparams(("parallel", "parallel", "arbitrary")),
        name="swa_prompt",
    )(sinks.reshape(-1), proj, proj, proj, cos_t * Q_SCALE, sin_t * Q_SCALE, cos_t, sin_t)


def _swa_sample_kernel(sink_ref, q_ref, k_ref, v_ref, ckc_ref, cvc_ref, cos_ref, sin_ref,
                       o_ref, kr_ref, vr_ref, *, t, layer):
    cos_t = cos_ref[...]
    sin_t = sin_ref[...]
    kr = _rope(k_ref[...], cos_t, sin_t)
    v = v_ref[...]
    kr_ref[...] = kr
    vr_ref[...] = v
    qr = (_rope(q_ref[...], cos_t, sin_t) * Q_SCALE).astype(BF16)
    ka = jnp.concatenate([ckc_ref[0], kr], axis=0).astype(BF16)
    va = jnp.concatenate([cvc_ref[0], v], axis=0).astype(BF16)
    for kh in range(B_KV_HEADS):
        base = kh * B_GROUP * B_HD
        qs = jnp.concatenate(
            [qr[:, base + g * B_HD:base + (g + 1) * B_HD] for g in range(B_GROUP)], axis=0)
        sink_row = _sink_row(sink_ref, layer * B_HEADS + kh * B_GROUP, t)
        o = _attend(qs, ka[:, kh * B_HD:(kh + 1) * B_HD], va[:, kh * B_HD:(kh + 1) * B_HD],
                    sink_row, None)
        for g in range(0, B_GROUP, 2):
            o_ref[:, base + g * B_HD:base + (g + 2) * B_HD] = jnp.concatenate(
                [o[g * t:(g + 1) * t], o[(g + 1) * t:(g + 2) * t]], axis=1).astype(o_ref.dtype)


def _swa_sample(proj, cache_k, cache_v, sinks, layer, nb, t):
    cos_t, sin_t = _rope_tables(PAST_LEN + jnp.arange(t))
    kvw = B_KV_HEADS * B_HD
    kernel = functools.partial(_swa_sample_kernel, t=t, layer=layer)
    return pl.pallas_call(
        kernel,
        grid=(nb,),
        in_specs=[pl.BlockSpec(memory_space=pltpu.SMEM),
                  pl.BlockSpec((t, B_WIDTH), lambda b: (b, COL_B_Q * LANES // B_WIDTH)),
                  pl.BlockSpec((t, kvw), lambda b: (b, COL_B_K * LANES // kvw)),
                  pl.BlockSpec((t, kvw), lambda b: (b, COL_B_V * LANES // kvw)),
                  pl.BlockSpec((1, WINDOW, kvw), lambda b: (b, 0, 0)),
                  pl.BlockSpec((1, WINDOW, kvw), lambda b: (b, 0, 0)),
                  pl.BlockSpec((t, LANES), lambda b: (0, 0)),
                  pl.BlockSpec((t, LANES), lambda b: (0, 0))],
        out_specs=[pl.BlockSpec((t, B_WIDTH), lambda b: (b, 0)),
                   pl.BlockSpec((t, kvw), lambda b: (b, 0)),
                   pl.BlockSpec((t, kvw), lambda b: (b, 0))],
        out_shape=[jax.ShapeDtypeStruct((nb * t, B_WIDTH), BF16),
                   jax.ShapeDtypeStruct((nb * t, kvw), F32),
                   jax.ShapeDtypeStruct((nb * t, kvw), F32)],
        compiler_params=_cparams(("parallel",)),
        name="swa_sample",
    )(sinks.reshape(-1), proj, proj, proj,
      cache_k.reshape(nb, WINDOW, kvw), cache_v.reshape(nb, WINDOW, kvw), cos_t, sin_t)


def _layer_sample(x, w, layer, nb, t, cache, s_a, s_c, ff_pad):
    m, d = x.shape
    h = _rmsnorm(x, w["norm_mix"][layer], BF16, m)
    proj, wb_in = _mm_in_cast(h, w["w_in"], layer, IN_COLS_PAD, MM_TN_IN)
    o_a, s_a_new = _hgrn(proj, w["lb_logits"], w["hgrn_norm"][layer], s_a, layer, nb, t, t)
    o_b, k_rows, v_rows = _swa_sample(proj, cache[0], cache[1], w["sinks"], layer, nb, t)
    o_c, s_c_new = _gla(proj, w["w_a2"][layer], w["b_a"][layer], w["gla_norm"][layer],
                        s_c, nb, t, t)
    x, wb_out = _mm_out_cast(o_a, o_b, o_c, w["w_out"], x, layer, min(d, 512))
    h = _rmsnorm(x, w["norm_ffn"][layer], BF16, m)
    mid, wb_gate, wb_up = _gate_up_cast(h, w["w_gate_up"], layer, ff_pad)
    x, wb_down = _down_cast(mid, w["w_down"], x, layer)
    outs = (k_rows.reshape(nb, t, B_KV_HEADS, B_HD), v_rows.reshape(nb, t, B_KV_HEADS, B_HD),
            s_a_new, s_c_new)
    return x, outs, dict(w_in=wb_in, w_out=wb_out, w_gate=wb_gate, w_up=wb_up, w_down=wb_down)


def _layer_prompt(x, w, wb, layer, nb, t):
    m, d = x.shape
    tm = min(m, 1024)
    h = _rmsnorm(x, w["norm_mix"][layer], BF16, min(m, 256))
    proj = _matmul(h, wb["w_in"], tm, MM_TN_IN, F32)
    o_a, s_a_new = _hgrn(proj, w["lb_logits"], w["hgrn_norm"][layer], None, layer, nb, t, CHUNK)
    o_b, k_rows, v_rows = _swa_prompt(proj, w["sinks"], layer, nb, t)
    o_c, s_c_new = _gla(proj, w["w_a2"][layer], w["b_a"][layer], w["gla_norm"][layer],
                        None, nb, t, CHUNK)
    x = _matmul_out(o_a, o_b, o_c, wb["w_out"], x, tm, min(d, 512))
    h = _rmsnorm(x, w["norm_ffn"][layer], BF16, min(m, 256))
    mid = _matmul_gate_up(h, wb["w_gate"], wb["w_up"], tm, 512)
    x = _matmul_down(mid, wb["w_down"], x, tm, min(d, 1024), wb["w_down"].shape[0] // 4)
    outs = (k_rows.reshape(nb, WINDOW, B_KV_HEADS, B_HD),
            v_rows.reshape(nb, WINDOW, B_KV_HEADS, B_HD), s_a_new, s_c_new)
    return x, outs


def kernel(x_prompt, x_sample, cache_k_swa, cache_v_swa, state_hgrn, state_gla, norm_mix, w_in,
           hgrn_lb_logits, hgrn_norm, swa_sinks, gla_w_alpha2, gla_b_alpha, gla_norm, w_out,
           norm_ffn, w_gate_up, w_down, norm_final):
    n_p, t_p, d = x_prompt.shape
    n_s, t_s, _ = x_sample.shape
    d_ff = w_down.shape[1]
    ff_pad = -(-d_ff // 1024) * 1024
    w = dict(
        norm_mix=norm_mix, norm_ffn=norm_ffn, hgrn_norm=hgrn_norm, gla_norm=gla_norm,
        lb_logits=hgrn_lb_logits, sinks=swa_sinks, b_a=gla_b_alpha,
        w_in=w_in, w_out=w_out, w_gate_up=w_gate_up, w_down=w_down,
        w_a2=jnp.pad(gla_w_alpha2, ((0, 0), (0, LANES - GLA_RANK), (0, 0))).astype(BF16),
    )
    xp = x_prompt.reshape(n_p * t_p, d)
    xs = x_sample.reshape(n_s * t_s, d)
    outs_p, outs_s = [], []
    for layer in range(DEPTH):
        xs, rest, wb = _layer_sample(xs, w, layer, n_s, t_s,
                                     (cache_k_swa[layer], cache_v_swa[layer]),
                                     state_hgrn[layer], state_gla[layer], ff_pad)
        outs_s.append(rest)
        xp, rest = _layer_prompt(xp, w, wb, layer, n_p, t_p)
        outs_p.append(rest)
    y_p = _rmsnorm(xp, norm_final, F32, min(xp.shape[0], 256)).reshape(n_p, t_p, d)
    y_s = _rmsnorm(xs, norm_final, F32, min(xs.shape[0], 256)).reshape(n_s, t_s, d)
    stack = lambda outs, i: jnp.stack([o[i] for o in outs])
    return (y_p, y_s,
            stack(outs_p, 0), stack(outs_p, 1), stack(outs_p, 2), stack(outs_p, 3),
            stack(outs_s, 0), stack(outs_s, 1), stack(outs_s, 2), stack(outs_s, 3))
```

```python
import functools

import jax
import jax.numpy as jnp
import numpy as np
from jax import lax
from jax.experimental import pallas as pl
from jax.experimental.pallas import tpu as pltpu

F32 = jnp.float32
BF16 = jnp.bfloat16

DEPTH = 2
PAST_LEN = 4096
CHUNK = 64
EPS = 1e-6
NEG_BIG = -1e30
LB_FLOOR = 1e-30
A_HEADS = 8
A_DK = 128
A_DV = 128
A_WIDTH = A_HEADS * A_DV
B_HEADS = 32
B_KV_HEADS = 4
B_GROUP = B_HEADS // B_KV_HEADS
B_HD = 64
B_WIDTH = B_HEADS * B_HD
B_SCALE = B_HD ** -0.5
WINDOW = 128
WINDOW_CHUNKS = WINDOW // CHUNK
ROPE_THETA = 10000.0
C_HEADS = 8
C_DK = 64
C_DV = 128
C_WIDTH = C_HEADS * C_DV
GLA_RANK = 16
GLA_NORMALIZER = 16.0

LANES = 128

COL_A_Q = 0
COL_A_F = 8
COL_A_I = 16
COL_A_G = 24
COL_B_Q = 32
COL_B_K = 48
COL_B_V = 50
COL_C_Q = 52
COL_C_K = 56
COL_C_V = 60
COL_C_R = 68
COL_C_A = 76
IN_COLS = 9744
IN_COLS_PAD = 9984
MM_TN_IN = 768
VMEM_LIMIT = 56 * 1024 * 1024


def _cparams(sem):
    return pltpu.CompilerParams(dimension_semantics=sem, vmem_limit_bytes=VMEM_LIMIT)


def _dot(a, b):
    return jnp.dot(a, b, preferred_element_type=F32)


def _dot_nt(a, b):
    return lax.dot_general(a, b, (((1,), (1,)), ((), ())), preferred_element_type=F32)


def _dot_tn(a, b):
    return lax.dot_general(a, b, (((0,), (0,)), ((), ())), preferred_element_type=F32)


def _rmsnorm_kernel(x_ref, g_ref, o_ref):
    x = x_ref[...]
    var = jnp.mean(x * x, axis=-1, keepdims=True)
    o_ref[...] = (x * lax.rsqrt(var + EPS) * g_ref[...]).astype(o_ref.dtype)


def _rmsnorm(x, g, out_dtype, tm):
    m, d = x.shape
    return pl.pallas_call(
        _rmsnorm_kernel,
        grid=(m // tm,),
        in_specs=[pl.BlockSpec((tm, d), lambda i: (i, 0)),
                  pl.BlockSpec((1, d), lambda i: (0, 0))],
        out_specs=pl.BlockSpec((tm, d), lambda i: (i, 0)),
        out_shape=jax.ShapeDtypeStruct((m, d), out_dtype),
        compiler_params=_cparams(("parallel",)),
        name="rmsnorm",
    )(x, g.reshape(1, d))


def _row_scale(ssq_ref, d):
    return lax.rsqrt(jnp.sum(ssq_ref[...], axis=-1, keepdims=True) * (1.0 / d) + EPS)


def _emit_normed(x, g_ref, xg_ref, ssq_ref, first):
    xg_ref[...] = (x * g_ref[...]).astype(xg_ref.dtype)
    sq = x * x
    part = sq[:, 0:LANES]
    for c in range(1, x.shape[1] // LANES):
        part = part + sq[:, c * LANES:(c + 1) * LANES]

    @pl.when(first)
    def _():
        ssq_ref[...] = part

    @pl.when(jnp.logical_not(first))
    def _():
        ssq_ref[...] += part


def _mm_kernel(x_ref, w_ref, o_ref):
    o_ref[...] = _dot(x_ref[...], w_ref[...]).astype(o_ref.dtype)


def _mm_scaled_kernel(x_ref, s_ref, w_ref, o_ref):
    r = _row_scale(s_ref, x_ref.shape[1])
    o_ref[...] = (_dot(x_ref[...], w_ref[...]) * r).astype(o_ref.dtype)


def _matmul(x, w, tm, tn, out_dtype, ssq=None):
    m, k = x.shape
    n = w.shape[1]
    x_spec = pl.BlockSpec((tm, k), lambda i, j: (i, 0))
    w_spec = pl.BlockSpec((k, tn), lambda i, j: (0, j))
    s_spec = pl.BlockSpec((tm, LANES), lambda i, j: (i, 0))
    return pl.pallas_call(
        _mm_kernel if ssq is None else _mm_scaled_kernel,
        grid=(m // tm, n // tn),
        in_specs=[x_spec, w_spec] if ssq is None else [x_spec, s_spec, w_spec],
        out_specs=pl.BlockSpec((tm, tn), lambda i, j: (i, j)),
        out_shape=jax.ShapeDtypeStruct((m, n), out_dtype),
        compiler_params=_cparams(("parallel", "arbitrary")),
        name="mm_in",
    )(*((x, w) if ssq is None else (x, ssq, w)))


def _mm_out_kernel(oa_ref, ob_ref, oc_ref, w_ref, r_ref, g_ref, o_ref, xg_ref, ssq_ref, cat_ref):
    j = pl.program_id(1)

    @pl.when(j == 0)
    def _():
        cat_ref[:, 0:A_WIDTH] = oa_ref[...]
        cat_ref[:, A_WIDTH:A_WIDTH + B_WIDTH] = ob_ref[...]
        cat_ref[:, A_WIDTH + B_WIDTH:] = oc_ref[...]

    x = r_ref[...] + _dot(cat_ref[...], w_ref[...])
    o_ref[...] = x
    _emit_normed(x, g_ref, xg_ref, ssq_ref, j == 0)


def _matmul_out(oa, ob, oc, w, resid, gain, tm, tn):
    m = oa.shape[0]
    k, n = w.shape
    tile = pl.BlockSpec((tm, tn), lambda i, j: (i, j))
    return pl.pallas_call(
        _mm_out_kernel,
        grid=(m // tm, n // tn),
        in_specs=[pl.BlockSpec((tm, A_WIDTH), lambda i, j: (i, 0)),
                  pl.BlockSpec((tm, B_WIDTH), lambda i, j: (i, 0)),
                  pl.BlockSpec((tm, C_WIDTH), lambda i, j: (i, 0)),
                  pl.BlockSpec((k, tn), lambda i, j: (0, j)),
                  tile,
                  pl.BlockSpec((1, tn), lambda i, j: (0, j))],
        out_specs=[tile, tile, pl.BlockSpec((tm, LANES), lambda i, j: (i, 0))],
        out_shape=[jax.ShapeDtypeStruct((m, n), F32), jax.ShapeDtypeStruct((m, n), BF16),
                   jax.ShapeDtypeStruct((m, LANES), F32)],
        scratch_shapes=[pltpu.VMEM((tm, k), BF16)],
        compiler_params=_cparams(("parallel", "arbitrary")),
        name="mm_out",
    )(oa, ob, oc, w, resid, gain.reshape(1, n))


def _mm_gate_up_kernel(x_ref, s_ref, wg_ref, wu_ref, o_ref):
    x = x_ref[...]
    r = _row_scale(s_ref, x.shape[1])
    gate = _dot(x, wg_ref[...]) * r
    up = _dot(x, wu_ref[...]) * r
    o_ref[...] = (gate * (1.0 / (1.0 + jnp.exp(-gate))) * up).astype(o_ref.dtype)


def _matmul_gate_up(x, ssq, wg, wu, tm, tn):
    m, k = x.shape
    n = wg.shape[1]
    return pl.pallas_call(
        _mm_gate_up_kernel,
        grid=(m // tm, n // tn),
        in_specs=[pl.BlockSpec((tm, k), lambda i, j: (i, 0)),
                  pl.BlockSpec((tm, LANES), lambda i, j: (i, 0)),
                  pl.BlockSpec((k, tn), lambda i, j: (0, j)),
                  pl.BlockSpec((k, tn), lambda i, j: (0, j))],
        out_specs=pl.BlockSpec((tm, tn), lambda i, j: (i, j)),
        out_shape=jax.ShapeDtypeStruct((m, n), BF16),
        compiler_params=_cparams(("parallel", "arbitrary")),
        name="mm_gate_up",
    )(x, ssq, wg, wu)


def _mm_down_kernel(x_ref, w_ref, r_ref, *rest, nk, emit):
    kk = pl.program_id(2)
    o_ref = rest[1] if emit else rest[0]

    @pl.when(kk == 0)
    def _():
        o_ref[...] = r_ref[...] + _dot(x_ref[...], w_ref[...])

    @pl.when(kk != 0)
    def _():
        o_ref[...] += _dot(x_ref[...], w_ref[...])

    if emit:
        g_ref, _, xg_ref, ssq_ref = rest

        @pl.when(kk == nk - 1)
        def _():
            _emit_normed(o_ref[...], g_ref, xg_ref, ssq_ref, pl.program_id(1) == 0)


def _matmul_down(x, w, resid, tm, tn, tk, gain=None):
    m, k = x.shape
    n = w.shape[1]
    emit = gain is not None
    tile = pl.BlockSpec((tm, tn), lambda i, j, kk: (i, j))
    in_specs = [pl.BlockSpec((tm, tk), lambda i, j, kk: (i, kk)),
                pl.BlockSpec((tk, tn), lambda i, j, kk: (kk, j)),
                tile]
    out_specs, out_shape, args = [tile], [jax.ShapeDtypeStruct((m, n), F32)], [x, w, resid]
    if emit:
        in_specs.append(pl.BlockSpec((1, tn), lambda i, j, kk: (0, j)))
        args.append(gain.reshape(1, n))
        out_specs += [tile, pl.BlockSpec((tm, LANES), lambda i, j, kk: (i, 0))]
        out_shape += [jax.ShapeDtypeStruct((m, n), BF16), jax.ShapeDtypeStruct((m, LANES), F32)]
    out = pl.pallas_call(
        functools.partial(_mm_down_kernel, nk=k // tk, emit=emit),
        grid=(m // tm, n // tn, k // tk),
        in_specs=in_specs,
        out_specs=out_specs,
        out_shape=out_shape,
        compiler_params=_cparams(("parallel", "arbitrary", "arbitrary")),
        name="mm_down",
    )(*args)
    return out if emit else out[0]


CAST_TN_GATE = 256
CAST_TK_DOWN = 512


def _masked_bf16(w, first, n_valid, axis):
    idx = first + lax.broadcasted_iota(jnp.int32, w.shape, axis)
    return jnp.where(idx < n_valid, w, 0.0).astype(BF16)


def _mm_in_cast_kernel(x_ref, w_ref, o_ref, wb_ref, *, n_valid, tn):
    idx = pl.program_id(0) * tn + lax.broadcasted_iota(jnp.int32, w_ref.shape, 0)
    wb = jnp.where(idx < n_valid, w_ref[...], 0.0).T.astype(BF16)
    wb_ref[...] = wb
    o_ref[...] = _dot(x_ref[...], wb)


def _mm_in_cast(x, w3, layer, n_pad, tn):
    m, k = x.shape
    n = w3.shape[2]
    return pl.pallas_call(
        functools.partial(_mm_in_cast_kernel, n_valid=n, tn=tn),
        grid=(n_pad // tn,),
        in_specs=[pl.BlockSpec((m, k), lambda j: (0, 0)),
                  pl.BlockSpec((None, tn, k), lambda j: (layer, j, 0))],
        out_specs=[pl.BlockSpec((m, tn), lambda j: (0, j)),
                   pl.BlockSpec((k, tn), lambda j: (0, j))],
        out_shape=[jax.ShapeDtypeStruct((m, n_pad), F32),
                   jax.ShapeDtypeStruct((k, n_pad), BF16)],
        compiler_params=_cparams(("arbitrary",)),
        name="mm_in_cast",
    )(x, jnp.swapaxes(w3, 1, 2))


def _mm_out_cast_kernel(oa_ref, ob_ref, oc_ref, w_ref, r_ref, o_ref, wb_ref, cat_ref):
    @pl.when(pl.program_id(0) == 0)
    def _():
        cat_ref[:, 0:A_WIDTH] = oa_ref[...]
        cat_ref[:, A_WIDTH:A_WIDTH + B_WIDTH] = ob_ref[...]
        cat_ref[:, A_WIDTH + B_WIDTH:] = oc_ref[...]

    wb = w_ref[...].astype(BF16)
    wb_ref[...] = wb
    o_ref[...] = r_ref[...] + _dot(cat_ref[...], wb)


def _mm_out_cast(oa, ob, oc, w3, resid, layer, tn):
    m = oa.shape[0]
    k, n = w3.shape[1:]
    return pl.pallas_call(
        _mm_out_cast_kernel,
        grid=(n // tn,),
        in_specs=[pl.BlockSpec((m, A_WIDTH), lambda j: (0, 0)),
                  pl.BlockSpec((m, B_WIDTH), lambda j: (0, 0)),
                  pl.BlockSpec((m, C_WIDTH), lambda j: (0, 0)),
                  pl.BlockSpec((None, k, tn), lambda j: (layer, 0, j)),
                  pl.BlockSpec((m, tn), lambda j: (0, j))],
        out_specs=[pl.BlockSpec((m, tn), lambda j: (0, j)),
                   pl.BlockSpec((k, tn), lambda j: (0, j))],
        out_shape=[jax.ShapeDtypeStruct((m, n), F32),
                   jax.ShapeDtypeStruct((k, n), BF16)],
        scratch_shapes=[pltpu.VMEM((m, k), BF16)],
        compiler_params=_cparams(("arbitrary",)),
        name="mm_out_cast",
    )(oa, ob, oc, w3, resid)


def _gate_up_cast_kernel(x_ref, wg_ref, wu_ref, o_ref, wgb_ref, wub_ref, *, n_tiles):
    valid = pl.program_id(0) < n_tiles
    wg = jnp.where(valid, wg_ref[...], 0.0).astype(BF16)
    wu = jnp.where(valid, wu_ref[...], 0.0).astype(BF16)
    wgb_ref[...] = wg
    wub_ref[...] = wu
    x = x_ref[...]
    gate = _dot(x, wg)
    o_ref[...] = (gate * (1.0 / (1.0 + jnp.exp(-gate))) * _dot(x, wu)).astype(o_ref.dtype)


def _gate_up_cast(x, wgu3, layer, ff_pad):
    m, k = x.shape
    tn = CAST_TN_GATE
    n_tiles = wgu3.shape[2] // 2 // tn
    assert n_tiles * tn * 2 == wgu3.shape[2]
    return pl.pallas_call(
        functools.partial(_gate_up_cast_kernel, n_tiles=n_tiles),
        grid=(ff_pad // tn,),
        in_specs=[pl.BlockSpec((m, k), lambda j: (0, 0)),
                  pl.BlockSpec((None, k, tn), lambda j: (layer, 0, jnp.minimum(j, n_tiles - 1))),
                  pl.BlockSpec((None, k, tn),
                               lambda j: (layer, 0, n_tiles + jnp.minimum(j, n_tiles - 1)))],
        out_specs=[pl.BlockSpec((m, tn), lambda j: (0, j)),
                   pl.BlockSpec((k, tn), lambda j: (0, j)),
                   pl.BlockSpec((k, tn), lambda j: (0, j))],
        out_shape=[jax.ShapeDtypeStruct((m, ff_pad), BF16),
                   jax.ShapeDtypeStruct((k, ff_pad), BF16),
                   jax.ShapeDtypeStruct((k, ff_pad), BF16)],
        compiler_params=_cparams(("arbitrary",)),
        name="gate_up_cast",
    )(x, wgu3, wgu3)


def _down_cast_kernel(x_ref, w_ref, r_ref, o_ref, wb_ref, *, k_valid, tk):
    kk = pl.program_id(0)
    wb = _masked_bf16(w_ref[...], kk * tk, k_valid, 0)
    wb_ref[...] = wb

    @pl.when(kk == 0)
    def _():
        o_ref[...] = r_ref[...] + _dot(x_ref[...], wb)

    @pl.when(kk != 0)
    def _():
        o_ref[...] += _dot(x_ref[...], wb)


def _down_cast(x, wd3, resid, layer):
    m, ff_pad = x.shape
    ff, d = wd3.shape[1:]
    tk = CAST_TK_DOWN
    last = (ff - 1) // tk
    return pl.pallas_call(
        functools.partial(_down_cast_kernel, k_valid=ff, tk=tk),
        grid=(ff_pad // tk,),
        in_specs=[pl.BlockSpec((m, tk), lambda kk: (0, kk)),
                  pl.BlockSpec((None, tk, d), lambda kk: (layer, jnp.minimum(kk, last), 0)),
                  pl.BlockSpec((m, d), lambda kk: (0, 0))],
        out_specs=[pl.BlockSpec((m, d), lambda kk: (0, 0)),
                   pl.BlockSpec((tk, d), lambda kk: (kk, 0))],
        out_shape=[jax.ShapeDtypeStruct((m, d), F32),
                   jax.ShapeDtypeStruct((ff_pad, d), BF16)],
        compiler_params=_cparams(("arbitrary",)),
        name="down_cast",
    )(x, wd3, resid)


LOG2E = 1.4426950408889634
GLR_D0 = 8
GLR_NH = 2
GLR_UNROLL = 4


def _levels(c):
    out, b = [], GLR_D0
    while b < c:
        out.append(b)
        b *= 2
    return out


def _prefix_matrix(c):
    t = np.arange(c)[:, None]
    s = np.arange(c)[None, :]
    low = s <= t
    same0 = (t // GLR_D0) == (s // GLR_D0)
    mats = [low, ~low, low & same0]
    for b in _levels(c):
        same = (t // b) == (s // b)
        odd = ((t // b) % 2) == 1
        mats.append(np.where(odd, low & same, (~low) & same))
    m = np.concatenate(mats, axis=0).astype(np.float32)
    return jnp.asarray(np.concatenate([m, m, m], axis=1), dtype=BF16)


def _diag_matrix(c, hl):
    kd = LANES // hl
    sp = np.arange(GLR_D0)[:, None, None]
    lane = np.arange(LANES)[None, :, None]
    col = np.arange(hl * c)[None, None, :]
    r = ((col // c) == (lane // kd)) & ((col % GLR_D0) == sp)
    return jnp.asarray(r.reshape(GLR_D0 * LANES, hl * c).astype(np.float32), dtype=BF16)


def _glr_consts(c, hl):
    t = lax.broadcasted_iota(jnp.int32, (c, hl * c), 0)
    s = lax.broadcasted_iota(jnp.int32, (c, hl * c), 1) & (c - 1)
    lvl_masks = []
    for b in _levels(c):
        lg = b.bit_length() - 1
        tb = lax.shift_right_logical(t, lg)
        sb = lax.shift_right_logical(s, lg)
        lvl_masks.append((lax.shift_right_logical(tb, 1) == lax.shift_right_logical(sb, 1))
                         & ((tb & 1) == 1) & ((sb & 1) == 0))
    lg0 = GLR_D0.bit_length() - 1
    diag_valid = ((lax.shift_right_logical(t, lg0) == lax.shift_right_logical(s, lg0))
                  & ((s & (GLR_D0 - 1)) <= (t & (GLR_D0 - 1))))
    lane = lax.broadcasted_iota(jnp.int32, (1, LANES), 1)
    head_masks = [(lane < C_DK).astype(F32), (lane >= C_DK).astype(F32)]
    return dict(lvl_masks=lvl_masks, diag_valid=diag_valid, head_masks=head_masks, lane=lane)


def _glr_prefix(lf2, pm):
    hi = lf2.astype(BF16)
    r1 = lf2 - hi.astype(F32)
    mid = r1.astype(BF16)
    lo = (r1 - mid.astype(F32)).astype(BF16)
    return _dot(pm, jnp.concatenate([hi, mid, lo], axis=0))


def _glr_intra(q, k, pref, rd, cst, *, c, hl):
    hm = cst["head_masks"]
    g_all = pref[0:c]
    qhat = (q * jnp.exp2(g_all)).astype(BF16)
    kdec = (k * jnp.exp2(pref[c:2 * c])).astype(BF16)
    gin0 = pref[2 * c:3 * c]

    nb = c // GLR_D0
    k3 = k.reshape(nb, GLR_D0, LANES)
    g3 = gin0.reshape(nb, GLR_D0, LANES)
    ps = []
    for sp in range(GLR_D0):
        kb = jnp.broadcast_to(k3[:, sp:sp + 1, :], (nb, GLR_D0, LANES)).reshape(c, LANES)
        gb = jnp.broadcast_to(g3[:, sp:sp + 1, :], (nb, GLR_D0, LANES)).reshape(c, LANES)
        ps.append((q * kb * jnp.exp2(jnp.minimum(gin0 - gb, 0.0))).astype(BF16))
    a = jnp.where(cst["diag_valid"], _dot(jnp.concatenate(ps, axis=1), rd), 0.0)
    for i in range(len(cst["lvl_masks"])):
        eb = jnp.exp2(pref[(3 + i) * c:(4 + i) * c])
        qb = (q * eb).astype(BF16)
        kb = k * eb
        if hl == 2:
            kb = jnp.concatenate([kb * hm[0], kb * hm[1]], axis=0)
        a = jnp.where(cst["lvl_masks"][i], _dot_nt(qb, kb.astype(BF16)), a)
    return a.astype(BF16), qhat, kdec, jnp.exp2(g_all[c - 1:c, :])


def _glr_state(a, qhat, kdec, dec, vs, st, cst, *, hl):
    stb = st.astype(BF16)
    first = cst["lane"] < C_DK
    outs, upd = [], []
    for h in range(hl):
        vb = vs[h].astype(BF16)
        if hl == 2:
            zero = jnp.zeros_like(vb)
            vz = jnp.concatenate([vb, zero] if h == 0 else [zero, vb], axis=0)
            qh = jnp.where(first if h == 0 else ~first, qhat, jnp.zeros_like(qhat))
        else:
            vz, qh = vb, qhat
        outs.append(_dot(a, vz) + _dot_nt(qh, stb))
        upd.append(_dot_tn(vb, kdec))
    u = jnp.where(first, upd[0], upd[1]) if hl == 2 else upd[0]
    return outs, st * dec + u


def _gated_norm(o, norm, gate):
    var = jnp.mean(o * o, axis=-1, keepdims=True)
    return o * lax.rsqrt(var + EPS) * norm * (gate * (1.0 / (1.0 + jnp.exp(-gate))))


def _glr_scratch(t, c, hl):
    w = GLR_NH * LANES
    return [pltpu.VMEM((GLR_NH, t, hl * c), BF16), pltpu.VMEM((t, w), BF16),
            pltpu.VMEM((t, w), BF16), pltpu.VMEM((t // c, GLR_D0, w), F32)]


def _hgrn_kernel(*refs, c, nchunks, layer, has_s0):
    if has_s0:
        (pm_ref, rd_ref, lbl_ref, norm_ref, q_ref, z_ref, v_ref, g_ref, s0_ref, o_ref, s_ref,
         a_s, qh_s, kd_s, dec_s) = refs
    else:
        (pm_ref, rd_ref, lbl_ref, norm_ref, q_ref, z_ref, v_ref, g_ref, o_ref, s_ref,
         a_s, qh_s, kd_s, dec_s) = refs
    logits = lbl_ref[...]
    e = jnp.exp(logits - jnp.max(logits, axis=0, keepdims=True))
    probs = e / jnp.sum(e, axis=0, keepdims=True)
    lb = jnp.sum(probs[0:layer + 1], axis=0, keepdims=True) - probs[0:1]
    lb_floor = jnp.maximum(lb, LB_FLOOR)
    oml = 1.0 - lb
    norm = norm_ref[...]
    pm = pm_ref[...]
    rd = rd_ref[...]
    cst = _glr_consts(c, 1)

    def intra(n, carry):
        rows = pl.ds(pl.multiple_of(n * c, c), c)
        q = q_ref[rows, :]
        z = z_ref[rows, :]
        ez = jnp.exp(-jnp.abs(z))
        r = 1.0 / (1.0 + ez)
        pos = z >= 0.0
        lf2 = jnp.log2(lb_floor + oml * jnp.where(pos, r, ez * r))
        k = oml * jnp.where(pos, ez * r, r)
        pref = _glr_prefix(lf2, pm)
        for j in range(GLR_NH):
            sl = slice(j * LANES, (j + 1) * LANES)
            a, qh, kd, dec = _glr_intra(q[:, sl], k[:, sl], pref[:, sl], rd, cst, c=c, hl=1)
            a_s[j, rows, :] = a
            qh_s[rows, sl] = qh
            kd_s[rows, sl] = kd
            dec_s[n, :, sl] = jnp.broadcast_to(dec, (GLR_D0, LANES))
        return carry

    lax.fori_loop(0, nchunks, intra, 0, unroll=min(GLR_UNROLL, nchunks))

    def state(n, sts):
        rows = pl.ds(pl.multiple_of(n * c, c), c)
        v = v_ref[rows, :]
        g = g_ref[rows, :]
        new, outs = [], []
        for j in range(GLR_NH):
            sl = slice(j * LANES, (j + 1) * LANES)
            o, st2 = _glr_state(a_s[j, rows, :], qh_s[rows, sl], kd_s[rows, sl],
                                dec_s[n, 0:1, sl], [v[:, sl]], sts[j], cst, hl=1)
            outs.append(_gated_norm(o[0], norm, g[:, sl]))
            new.append(st2)
        o_ref[rows, :] = jnp.concatenate(outs, axis=1).astype(o_ref.dtype)
        return tuple(new)

    if has_s0:
        st0 = tuple(s0_ref[0, j].T for j in range(GLR_NH))
    else:
        st0 = tuple(jnp.zeros((A_DV, A_DK), F32) for _ in range(GLR_NH))
    sts = lax.fori_loop(0, nchunks, state, st0, unroll=min(GLR_UNROLL, nchunks))
    for j in range(GLR_NH):
        s_ref[0, j] = sts[j].T


def _hgrn(proj, lbl, norm, s0, layer, nb, t, c):
    pm = _prefix_matrix(c)
    rd = _diag_matrix(c, 1)
    nchunks = t // c
    has_s0 = s0 is not None
    w = GLR_NH * LANES

    def col(off):
        return pl.BlockSpec((t, w), lambda b, h: (b, off // GLR_NH + h))

    st_spec = pl.BlockSpec((1, GLR_NH, A_DK, A_DV), lambda b, h: (b, h, 0, 0))
    in_specs = [pl.BlockSpec(pm.shape, lambda b, h: (0, 0)),
                pl.BlockSpec(rd.shape, lambda b, h: (0, 0)),
                pl.BlockSpec((DEPTH, w), lambda b, h: (0, h)),
                pl.BlockSpec((1, LANES), lambda b, h: (0, 0)),
                col(COL_A_Q), col(COL_A_F), col(COL_A_I), col(COL_A_G)]
    args = [pm, rd, lbl, norm.reshape(1, LANES), proj, proj, proj, proj]
    if has_s0:
        in_specs.append(st_spec)
        args.append(s0)
    return pl.pallas_call(
        functools.partial(_hgrn_kernel, c=c, nchunks=nchunks, layer=layer, has_s0=has_s0),
        grid=(nb, A_HEADS // GLR_NH),
        in_specs=in_specs,
        out_specs=[pl.BlockSpec((t, w), lambda b, h: (b, h)), st_spec],
        out_shape=[jax.ShapeDtypeStruct((nb * t, A_WIDTH), BF16),
                   jax.ShapeDtypeStruct((nb, A_HEADS, A_DK, A_DV), F32)],
        scratch_shapes=_glr_scratch(t, c, 1),
        compiler_params=_cparams(("parallel", "parallel")),
        name="hgrn",
    )(*args)


def _gla_kernel(*refs, c, nchunks, has_s0):
    if has_s0:
        (pm_ref, rd_ref, wa_ref, ba_ref, norm_ref, q_ref, k_ref, v_ref, r_ref, ca_ref,
         s0_ref, o_ref, s_ref, a_s, qh_s, kd_s, dec_s) = refs
    else:
        (pm_ref, rd_ref, wa_ref, ba_ref, norm_ref, q_ref, k_ref, v_ref, r_ref, ca_ref,
         o_ref, s_ref, a_s, qh_s, kd_s, dec_s) = refs
    wa = wa_ref[...]
    ba = ba_ref[...]
    norm = norm_ref[...]
    pm = pm_ref[...]
    rd = rd_ref[...]
    cst = _glr_consts(c, 2)

    def intra(n, carry):
        rows = pl.ds(pl.multiple_of(n * c, c), c)
        x = _dot(ca_ref[rows, :].astype(BF16), wa) + ba
        lf2 = (jnp.minimum(x, 0.0) - jnp.log1p(jnp.exp(-jnp.abs(x)))) * (LOG2E / GLA_NORMALIZER)
        q = q_ref[rows, :] * (C_DK ** -0.5)
        k = k_ref[rows, :]
        pref = _glr_prefix(lf2, pm)
        for j in range(GLR_NH):
            sl = slice(j * LANES, (j + 1) * LANES)
            a, qh, kd, dec = _glr_intra(q[:, sl], k[:, sl], pref[:, sl], rd, cst, c=c, hl=2)
            a_s[j, rows, :] = a
            qh_s[rows, sl] = qh
            kd_s[rows, sl] = kd
            dec_s[n, :, sl] = jnp.broadcast_to(dec, (GLR_D0, LANES))
        return carry

    lax.fori_loop(0, nchunks, intra, 0, unroll=min(GLR_UNROLL, nchunks))

    def state(n, sts):
        rows = pl.ds(pl.multiple_of(n * c, c), c)
        v = v_ref[rows, :]
        gate = r_ref[rows, :]
        new, outs = [], []
        for j in range(GLR_NH):
            sl = slice(j * LANES, (j + 1) * LANES)
            hs = [slice((2 * j + h) * C_DV, (2 * j + h + 1) * C_DV) for h in range(2)]
            o, st2 = _glr_state(a_s[j, rows, :], qh_s[rows, sl], kd_s[rows, sl],
                                dec_s[n, 0:1, sl], [v[:, hs[0]], v[:, hs[1]]], sts[j], cst, hl=2)
            outs += [_gated_norm(o[h], norm, gate[:, hs[h]]) for h in range(2)]
            new.append(st2)
        o_ref[rows, :] = jnp.concatenate(outs, axis=1).astype(o_ref.dtype)
        return tuple(new)

    if has_s0:
        st0 = tuple(s0_ref[0, j].T for j in range(GLR_NH))
    else:
        st0 = tuple(jnp.zeros((C_DV, LANES), F32) for _ in range(GLR_NH))
    sts = lax.fori_loop(0, nchunks, state, st0, unroll=min(GLR_UNROLL, nchunks))
    for j in range(GLR_NH):
        s_ref[0, j] = sts[j].T


def _gla(proj, wa2, ba, norm, s0, nb, t, c):
    pm = _prefix_matrix(c)
    rd = _diag_matrix(c, 2)
    nchunks = t // c
    has_s0 = s0 is not None
    npair = C_HEADS // 2
    w = GLR_NH * LANES

    def col(off, width):
        return pl.BlockSpec((t, width), lambda b, p: (b, off * LANES // width + p))

    st_spec = pl.BlockSpec((1, GLR_NH, LANES, C_DV), lambda b, p: (b, p, 0, 0))
    in_specs = [pl.BlockSpec(pm.shape, lambda b, p: (0, 0)),
                pl.BlockSpec(rd.shape, lambda b, p: (0, 0)),
                pl.BlockSpec((LANES, w), lambda b, p: (0, p)),
                pl.BlockSpec((1, w), lambda b, p: (0, p)),
                pl.BlockSpec((1, LANES), lambda b, p: (0, 0)),
                col(COL_C_Q, w), col(COL_C_K, w), col(COL_C_V, 2 * w), col(COL_C_R, 2 * w),
                pl.BlockSpec((t, LANES), lambda b, p: (b, COL_C_A))]
    args = [pm, rd, wa2, ba.reshape(1, -1), norm.reshape(1, LANES), proj, proj, proj, proj, proj]
    if has_s0:
        in_specs.append(st_spec)
        args.append(s0.reshape(nb, npair, 2 * C_DK, C_DV))
    o, s = pl.pallas_call(
        functools.partial(_gla_kernel, c=c, nchunks=nchunks, has_s0=has_s0),
        grid=(nb, npair // GLR_NH),
        in_specs=in_specs,
        out_specs=[pl.BlockSpec((t, 2 * w), lambda b, p: (b, p)), st_spec],
        out_shape=[jax.ShapeDtypeStruct((nb * t, C_WIDTH), BF16),
                   jax.ShapeDtypeStruct((nb, npair, 2 * C_DK, C_DV), F32)],
        scratch_shapes=_glr_scratch(t, c, 2),
        compiler_params=_cparams(("parallel", "parallel")),
        name="gla",
    )(*args)
    return o, s.reshape(nb, C_HEADS, C_DK, C_DV)


SWA_CB = 4
Q_SCALE = B_SCALE * 1.4426950408889634


def _rope_tables(pos):
    half = B_HD // 2
    inv = ROPE_THETA ** (-jnp.arange(half, dtype=F32) / half)
    ang = pos.astype(F32)[:, None] * inv[None, :]
    cos = jnp.cos(ang)
    sin = jnp.sin(ang)
    cos_t = jnp.tile(cos, (1, 2 * LANES // B_HD))
    sin_t = jnp.tile(jnp.concatenate([-sin, sin], axis=-1), (1, LANES // B_HD))
    return cos_t, sin_t


def _rope(x, cos_t, sin_t):
    n = x.shape[1]
    half = B_HD // 2
    lane = lax.broadcasted_iota(jnp.int32, (1, n), 1)
    first = (lane & (B_HD - 1)) < half
    rot = jnp.where(first, pltpu.roll(x, n - half, axis=1), pltpu.roll(x, half, axis=1))
    reps = n // LANES
    if reps > 1:
        cos_t = jnp.concatenate([cos_t] * reps, axis=1)
        sin_t = jnp.concatenate([sin_t] * reps, axis=1)
    return x * cos_t + rot * sin_t


def _attend(qs, kb, vb, sink_row, valid):
    s = _dot_nt(kb, qs)
    if valid is not None:
        s = jnp.where(valid, s, NEG_BIG)
    sink2 = sink_row * LOG2E
    m = jnp.maximum(jnp.max(s, axis=0, keepdims=True), sink2)
    p = jnp.exp2(s - m)
    den = jnp.sum(p, axis=0, keepdims=True) + jnp.exp2(sink2 - m)
    o_t = _dot_tn(vb, p.astype(BF16)) * (1.0 / den)
    return o_t.T


def _sink_row(sink_ref, base, t):
    lanes = lax.broadcasted_iota(jnp.int32, (1, B_GROUP * t), 1)
    row = jnp.zeros((1, B_GROUP * t), F32)
    for g in range(B_GROUP):
        row = jnp.where((lanes >= g * t) & (lanes < (g + 1) * t), sink_ref[base + g], row)
    return row


def _swa_prompt_kernel(sink_ref, q_ref, k_ref, v_ref, cq_ref, sq_ref, ck_ref, sk_ref,
                       o_ref, kr_ref, vr_ref, kro_ref, vbo_ref, *, t, layer):
    khp = pl.program_id(1)
    step = pl.program_id(2)

    @pl.when(step == 0)
    def _():
        kr = _rope(k_ref[...], ck_ref[...], sk_ref[...])
        kro_ref[...] = kr.astype(BF16)
        vbo_ref[...] = v_ref[...].astype(BF16)
        kr_ref[0] = kr[t - WINDOW:, :]
        vr_ref[0] = v_ref[t - WINDOW:, :]

    band = (WINDOW_CHUNKS + 1) * CHUNK
    sink_rows = [_sink_row(sink_ref, layer * B_HEADS + (khp * 2 + kv) * B_GROUP, CHUNK)
                 for kv in range(2)]
    for ci in range(SWA_CB):
        cidx = step * SWA_CB + ci
        rows = slice(ci * CHUNK, (ci + 1) * CHUNK)
        s0 = pl.multiple_of(jnp.maximum(cidx - WINDOW_CHUNKS, 0) * CHUNK, CHUNK)
        qr = _rope(q_ref[rows, :], cq_ref[rows, :], sq_ref[rows, :]).astype(BF16)
        kband = kro_ref[pl.ds(s0, band), :]
        vband = vbo_ref[pl.ds(s0, band), :]
        key_pos = s0 + lax.broadcasted_iota(jnp.int32, (band, 1), 0)
        valid = key_pos < (cidx + 1) * CHUNK
        for kv in range(2):
            base = kv * B_GROUP * B_HD
            qs = jnp.concatenate(
                [qr[:, base + g * B_HD:base + (g + 1) * B_HD] for g in range(B_GROUP)], axis=0)
            o = _attend(qs, kband[:, kv * B_HD:(kv + 1) * B_HD],
                        vband[:, kv * B_HD:(kv + 1) * B_HD], sink_rows[kv], valid)
            for g in range(0, B_GROUP, 2):
                o_ref[rows, base + g * B_HD:base + (g + 2) * B_HD] = jnp.concatenate(
                    [o[g * CHUNK:(g + 1) * CHUNK], o[(g + 1) * CHUNK:(g + 2) * CHUNK]],
                    axis=1).astype(o_ref.dtype)


def _swa_prompt(proj, sinks, layer, nb, t):
    rb = SWA_CB * CHUNK
    nc = t // rb
    cos_t, sin_t = _rope_tables(jnp.arange(t))
    qw = 2 * B_GROUP * B_HD
    qoff = COL_B_Q * LANES // qw
    kernel = functools.partial(_swa_prompt_kernel, t=t, layer=layer)
    return pl.pallas_call(
        kernel,
        grid=(nb, B_KV_HEADS // 2, nc),
        in_specs=[pl.BlockSpec(memory_space=pltpu.SMEM),
                  pl.BlockSpec((rb, qw), lambda b, p, c: (b * nc + c, qoff + p)),
                  pl.BlockSpec((t, LANES), lambda b, p, c: (b, COL_B_K + p)),
                  pl.BlockSpec((t, LANES), lambda b, p, c: (b, COL_B_V + p)),
                  pl.BlockSpec((rb, LANES), lambda b, p, c: (c, 0)),
                  pl.BlockSpec((rb, LANES), lambda b, p, c: (c, 0)),
                  pl.BlockSpec((t, LANES), lambda b, p, c: (0, 0)),
                  pl.BlockSpec((t, LANES), lambda b, p, c: (0, 0))],
        out_specs=[pl.BlockSpec((rb, qw), lambda b, p, c: (b * nc + c, p)),
                   pl.BlockSpec((1, WINDOW, LANES), lambda b, p, c: (b, 0, p)),
                   pl.BlockSpec((1, WINDOW, LANES), lambda b, p, c: (b, 0, p))],
        out_shape=[jax.ShapeDtypeStruct((nb * t, B_WIDTH), BF16),
                   jax.ShapeDtypeStruct((nb, WINDOW, B_KV_HEADS * B_HD), F32),
                   jax.ShapeDtypeStruct((nb, WINDOW, B_KV_HEADS * B_HD), F32)],
        scratch_shapes=[pltpu.VMEM((t, LANES), BF16), pltpu.VMEM((t, LANES), BF16)],
        compiler_params=_cparams(("parallel", "parallel", "arbitrary")),
        name="swa_prompt",
    )(sinks.reshape(-1), proj, proj, proj, cos_t * Q_SCALE, sin_t * Q_SCALE, cos_t, sin_t)


def _swa_sample_kernel(sink_ref, q_ref, k_ref, v_ref, ckc_ref, cvc_ref, cos_ref, sin_ref,
                       o_ref, kr_ref, vr_ref, *, t, layer):
    cos_t = cos_ref[...]
    sin_t = sin_ref[...]
    kr = _rope(k_ref[...], cos_t, sin_t)
    v = v_ref[...]
    kr_ref[...] = kr
    vr_ref[...] = v
    qr = (_rope(q_ref[...], cos_t, sin_t) * Q_SCALE).astype(BF16)
    ka = jnp.concatenate([ckc_ref[0], kr], axis=0).astype(BF16)
    va = jnp.concatenate([cvc_ref[0], v], axis=0).astype(BF16)
    for kh in range(B_KV_HEADS):
        base = kh * B_GROUP * B_HD
        qs = jnp.concatenate(
            [qr[:, base + g * B_HD:base + (g + 1) * B_HD] for g in range(B_GROUP)], axis=0)
        sink_row = _sink_row(sink_ref, layer * B_HEADS + kh * B_GROUP, t)
        o = _attend(qs, ka[:, kh * B_HD:(kh + 1) * B_HD], va[:, kh * B_HD:(kh + 1) * B_HD],
                    sink_row, None)
        for g in range(0, B_GROUP, 2):
            o_ref[:, base + g * B_HD:base + (g + 2) * B_HD] = jnp.concatenate(
                [o[g * t:(g + 1) * t], o[(g + 1) * t:(g + 2) * t]], axis=1).astype(o_ref.dtype)


def _swa_sample(proj, cache_k, cache_v, sinks, layer, nb, t):
    cos_t, sin_t = _rope_tables(PAST_LEN + jnp.arange(t))
    kvw = B_KV_HEADS * B_HD
    kernel = functools.partial(_swa_sample_kernel, t=t, layer=layer)
    return pl.pallas_call(
        kernel,
        grid=(nb,),
        in_specs=[pl.BlockSpec(memory_space=pltpu.SMEM),
                  pl.BlockSpec((t, B_WIDTH), lambda b: (b, COL_B_Q * LANES // B_WIDTH)),
                  pl.BlockSpec((t, kvw), lambda b: (b, COL_B_K * LANES // kvw)),
                  pl.BlockSpec((t, kvw), lambda b: (b, COL_B_V * LANES // kvw)),
                  pl.BlockSpec((1, WINDOW, kvw), lambda b: (b, 0, 0)),
                  pl.BlockSpec((1, WINDOW, kvw), lambda b: (b, 0, 0)),
                  pl.BlockSpec((t, LANES), lambda b: (0, 0)),
                  pl.BlockSpec((t, LANES), lambda b: (0, 0))],
        out_specs=[pl.BlockSpec((t, B_WIDTH), lambda b: (b, 0)),
                   pl.BlockSpec((t, kvw), lambda b: (b, 0)),
                   pl.BlockSpec((t, kvw), lambda b: (b, 0))],
        out_shape=[jax.ShapeDtypeStruct((nb * t, B_WIDTH), BF16),
                   jax.ShapeDtypeStruct((nb * t, kvw), F32),
                   jax.ShapeDtypeStruct((nb * t, kvw), F32)],
        compiler_params=_cparams(("parallel",)),
        name="swa_sample",
    )(sinks.reshape(-1), proj, proj, proj,
      cache_k.reshape(nb, WINDOW, kvw), cache_v.reshape(nb, WINDOW, kvw), cos_t, sin_t)


def _layer_sample(x, w, layer, nb, t, cache, s_a, s_c, ff_pad):
    m, d = x.shape
    h = _rmsnorm(x, w["norm_mix"][layer], BF16, m)
    proj, wb_in = _mm_in_cast(h, w["w_in"], layer, IN_COLS_PAD, MM_TN_IN)
    o_a, s_a_new = _hgrn(proj, w["lb_logits"], w["hgrn_norm"][layer], s_a, layer, nb, t, t)
    o_b, k_rows, v_rows = _swa_sample(proj, cache[0], cache[1], w["sinks"], layer, nb, t)
    o_c, s_c_new = _gla(proj, w["w_a2"][layer], w["b_a"][layer], w["gla_norm"][layer],
                        s_c, nb, t, t)
    x, wb_out = _mm_out_cast(o_a, o_b, o_c, w["w_out"], x, layer, min(d, 512))
    h = _rmsnorm(x, w["norm_ffn"][layer], BF16, m)
    mid, wb_gate, wb_up = _gate_up_cast(h, w["w_gate_up"], layer, ff_pad)
    x, wb_down = _down_cast(mid, w["w_down"], x, layer)
    outs = (k_rows.reshape(nb, t, B_KV_HEADS, B_HD), v_rows.reshape(nb, t, B_KV_HEADS, B_HD),
            s_a_new, s_c_new)
    return x, outs, dict(w_in=wb_in, w_out=wb_out, w_gate=wb_gate, w_up=wb_up, w_down=wb_down)


def _layer_prompt(x, normed, w, wb, layer, nb, t):
    m, d = x.shape
    tm = min(m, 1024)
    if normed is None:
        h = _rmsnorm(x, w["norm_mix"][layer], BF16, min(m, 256))
        proj = _matmul(h, wb["w_in"], tm, MM_TN_IN, F32)
    else:
        proj = _matmul(normed[0], wb["w_in"], tm, MM_TN_IN, F32, normed[1])
    o_a, s_a_new = _hgrn(proj, w["lb_logits"], w["hgrn_norm"][layer], None, layer, nb, t, CHUNK)
    o_b, k_rows, v_rows = _swa_prompt(proj, w["sinks"], layer, nb, t)
    o_c, s_c_new = _gla(proj, w["w_a2"][layer], w["b_a"][layer], w["gla_norm"][layer],
                        None, nb, t, CHUNK)
    x, xg, ssq = _matmul_out(o_a, o_b, o_c, wb["w_out"], x, w["norm_ffn"][layer], tm, min(d, 512))
    mid = _matmul_gate_up(xg, ssq, wb["w_gate"], wb["w_up"], tm, 512)
    tk = wb["w_down"].shape[0] // 4
    if layer + 1 < DEPTH:
        x, xg, ssq = _matmul_down(mid, wb["w_down"], x, tm, min(d, 1024), tk,
                                  w["norm_mix"][layer + 1])
        normed = (xg, ssq)
    else:
        x = _matmul_down(mid, wb["w_down"], x, tm, min(d, 1024), tk)
        normed = None
    outs = (k_rows.reshape(nb, WINDOW, B_KV_HEADS, B_HD),
            v_rows.reshape(nb, WINDOW, B_KV_HEADS, B_HD), s_a_new, s_c_new)
    return x, normed, outs


def kernel(x_prompt, x_sample, cache_k_swa, cache_v_swa, state_hgrn, state_gla, norm_mix, w_in,
           hgrn_lb_logits, hgrn_norm, swa_sinks, gla_w_alpha2, gla_b_alpha, gla_norm, w_out,
           norm_ffn, w_gate_up, w_down, norm_final):
    n_p, t_p, d = x_prompt.shape
    n_s, t_s, _ = x_sample.shape
    d_ff = w_down.shape[1]
    ff_pad = -(-d_ff // 1024) * 1024
    w = dict(
        norm_mix=norm_mix, norm_ffn=norm_ffn, hgrn_norm=hgrn_norm, gla_norm=gla_norm,
        lb_logits=hgrn_lb_logits, sinks=swa_sinks, b_a=gla_b_alpha,
        w_in=w_in, w_out=w_out, w_gate_up=w_gate_up, w_down=w_down,
        w_a2=jnp.pad(gla_w_alpha2, ((0, 0), (0, LANES - GLA_RANK), (0, 0))).astype(BF16),
    )
    xp = x_prompt.reshape(n_p * t_p, d)
    xs = x_sample.reshape(n_s * t_s, d)
    outs_p, outs_s = [], []
    normed = None
    for layer in range(DEPTH):
        xs, rest, wb = _layer_sample(xs, w, layer, n_s, t_s,
                                     (cache_k_swa[layer], cache_v_swa[layer]),
                                     state_hgrn[layer], state_gla[layer], ff_pad)
        outs_s.append(rest)
        xp, normed, rest = _layer_prompt(xp, normed, w, wb, layer, n_p, t_p)
        outs_p.append(rest)
    y_p = _rmsnorm(xp, norm_final, F32, min(xp.shape[0], 256)).reshape(n_p, t_p, d)
    y_s = _rmsnorm(xs, norm_final, F32, min(xs.shape[0], 256)).reshape(n_s, t_s, d)
    stack = lambda outs, i: jnp.stack([o[i] for o in outs])
    return (y_p, y_s,
            stack(outs_p, 0), stack(outs_p, 1), stack(outs_p, 2), stack(outs_p, 3),
            stack(outs_s, 0), stack(outs_s, 1), stack(outs_s, 2), stack(outs_s, 3))
```

```python
import functools

import jax
import jax.numpy as jnp
import numpy as np
from jax import lax
from jax.experimental import pallas as pl
from jax.experimental.pallas import tpu as pltpu

F32 = jnp.float32
BF16 = jnp.bfloat16

DEPTH = 2
PAST_LEN = 4096
CHUNK = 64
EPS = 1e-6
NEG_BIG = -1e30
LB_FLOOR = 1e-30
A_HEADS = 8
A_DK = 128
A_DV = 128
A_WIDTH = A_HEADS * A_DV
B_HEADS = 32
B_KV_HEADS = 4
B_GROUP = B_HEADS // B_KV_HEADS
B_HD = 64
B_WIDTH = B_HEADS * B_HD
B_SCALE = B_HD ** -0.5
WINDOW = 128
WINDOW_CHUNKS = WINDOW // CHUNK
ROPE_THETA = 10000.0
C_HEADS = 8
C_DK = 64
C_DV = 128
C_WIDTH = C_HEADS * C_DV
GLA_RANK = 16
GLA_NORMALIZER = 16.0

LANES = 128

COL_A_Q = 0
COL_A_F = 8
COL_A_I = 16
COL_A_G = 24
COL_B_Q = 32
COL_B_K = 48
COL_B_V = 50
COL_C_Q = 52
COL_C_K = 56
COL_C_V = 60
COL_C_R = 68
COL_C_A = 76
IN_COLS = 9744
IN_COLS_PAD = 9984
MM_TN_IN = 768
VMEM_LIMIT = 56 * 1024 * 1024


def _cparams(sem):
    return pltpu.CompilerParams(dimension_semantics=sem, vmem_limit_bytes=VMEM_LIMIT)


def _dot(a, b):
    return jnp.dot(a, b, preferred_element_type=F32)


def _dot_nt(a, b):
    return lax.dot_general(a, b, (((1,), (1,)), ((), ())), preferred_element_type=F32)


def _dot_tn(a, b):
    return lax.dot_general(a, b, (((0,), (0,)), ((), ())), preferred_element_type=F32)


def _rmsnorm_kernel(x_ref, g_ref, o_ref):
    x = x_ref[...]
    var = jnp.mean(x * x, axis=-1, keepdims=True)
    o_ref[...] = (x * lax.rsqrt(var + EPS) * g_ref[...]).astype(o_ref.dtype)


def _rmsnorm(x, g, out_dtype, tm):
    m, d = x.shape
    return pl.pallas_call(
        _rmsnorm_kernel,
        grid=(m // tm,),
        in_specs=[pl.BlockSpec((tm, d), lambda i: (i, 0)),
                  pl.BlockSpec((1, d), lambda i: (0, 0))],
        out_specs=pl.BlockSpec((tm, d), lambda i: (i, 0)),
        out_shape=jax.ShapeDtypeStruct((m, d), out_dtype),
        compiler_params=_cparams(("parallel",)),
        name="rmsnorm",
    )(x, g.reshape(1, d))


def _row_scale(ssq_ref, d):
    return lax.rsqrt(jnp.sum(ssq_ref[...], axis=-1, keepdims=True) * (1.0 / d) + EPS)


def _lane_partial_sq(x):
    sq = x * x
    part = sq[:, 0:LANES]
    for c in range(1, x.shape[1] // LANES):
        part = part + sq[:, c * LANES:(c + 1) * LANES]
    return part


def _accumulate(ref, part, first):
    @pl.when(first)
    def _():
        ref[...] = part

    @pl.when(jnp.logical_not(first))
    def _():
        ref[...] += part


def _emit_normed(x, g_ref, xg_ref, ssq_ref, first):
    xg_ref[...] = (x * g_ref[...]).astype(xg_ref.dtype)
    _accumulate(ssq_ref, _lane_partial_sq(x), first)


def _mm_kernel(x_ref, w_ref, o_ref):
    o_ref[...] = _dot(x_ref[...], w_ref[...]).astype(o_ref.dtype)


def _mm_scaled_kernel(x_ref, s_ref, w_ref, o_ref):
    r = _row_scale(s_ref, x_ref.shape[1])
    o_ref[...] = (_dot(x_ref[...], w_ref[...]) * r).astype(o_ref.dtype)


def _matmul(x, w, tm, tn, out_dtype, ssq=None):
    m, k = x.shape
    n = w.shape[1]
    x_spec = pl.BlockSpec((tm, k), lambda i, j: (i, 0))
    w_spec = pl.BlockSpec((k, tn), lambda i, j: (0, j))
    s_spec = pl.BlockSpec((tm, LANES), lambda i, j: (i, 0))
    return pl.pallas_call(
        _mm_kernel if ssq is None else _mm_scaled_kernel,
        grid=(m // tm, n // tn),
        in_specs=[x_spec, w_spec] if ssq is None else [x_spec, s_spec, w_spec],
        out_specs=pl.BlockSpec((tm, tn), lambda i, j: (i, j)),
        out_shape=jax.ShapeDtypeStruct((m, n), out_dtype),
        compiler_params=_cparams(("parallel", "arbitrary")),
        name="mm_in",
    )(*((x, w) if ssq is None else (x, ssq, w)))


def _mm_out_kernel(oa_ref, ob_ref, oc_ref, w_ref, r_ref, g_ref, o_ref, xg_ref, ssq_ref, cat_ref):
    j = pl.program_id(1)

    @pl.when(j == 0)
    def _():
        cat_ref[:, 0:A_WIDTH] = oa_ref[...]
        cat_ref[:, A_WIDTH:A_WIDTH + B_WIDTH] = ob_ref[...]
        cat_ref[:, A_WIDTH + B_WIDTH:] = oc_ref[...]

    x = r_ref[...] + _dot(cat_ref[...], w_ref[...])
    o_ref[...] = x
    _emit_normed(x, g_ref, xg_ref, ssq_ref, j == 0)


def _matmul_out(oa, ob, oc, w, resid, gain, tm, tn):
    m = oa.shape[0]
    k, n = w.shape
    tile = pl.BlockSpec((tm, tn), lambda i, j: (i, j))
    return pl.pallas_call(
        _mm_out_kernel,
        grid=(m // tm, n // tn),
        in_specs=[pl.BlockSpec((tm, A_WIDTH), lambda i, j: (i, 0)),
                  pl.BlockSpec((tm, B_WIDTH), lambda i, j: (i, 0)),
                  pl.BlockSpec((tm, C_WIDTH), lambda i, j: (i, 0)),
                  pl.BlockSpec((k, tn), lambda i, j: (0, j)),
                  tile,
                  pl.BlockSpec((1, tn), lambda i, j: (0, j))],
        out_specs=[tile, tile, pl.BlockSpec((tm, LANES), lambda i, j: (i, 0))],
        out_shape=[jax.ShapeDtypeStruct((m, n), F32), jax.ShapeDtypeStruct((m, n), BF16),
                   jax.ShapeDtypeStruct((m, LANES), F32)],
        scratch_shapes=[pltpu.VMEM((tm, k), BF16)],
        compiler_params=_cparams(("parallel", "arbitrary")),
        name="mm_out",
    )(oa, ob, oc, w, resid, gain.reshape(1, n))


def _mm_gate_up_kernel(x_ref, s_ref, wg_ref, wu_ref, o_ref):
    x = x_ref[...]
    r = _row_scale(s_ref, x.shape[1])
    gate = _dot(x, wg_ref[...]) * r
    up = _dot(x, wu_ref[...]) * r
    o_ref[...] = (gate * (1.0 / (1.0 + jnp.exp(-gate))) * up).astype(o_ref.dtype)


def _matmul_gate_up(x, ssq, wg, wu, tm, tn):
    m, k = x.shape
    n = wg.shape[1]
    return pl.pallas_call(
        _mm_gate_up_kernel,
        grid=(m // tm, n // tn),
        in_specs=[pl.BlockSpec((tm, k), lambda i, j: (i, 0)),
                  pl.BlockSpec((tm, LANES), lambda i, j: (i, 0)),
                  pl.BlockSpec((k, tn), lambda i, j: (0, j)),
                  pl.BlockSpec((k, tn), lambda i, j: (0, j))],
        out_specs=pl.BlockSpec((tm, tn), lambda i, j: (i, j)),
        out_shape=jax.ShapeDtypeStruct((m, n), BF16),
        compiler_params=_cparams(("parallel", "arbitrary")),
        name="mm_gate_up",
    )(x, ssq, wg, wu)


def _mm_down_kernel(x_ref, w_ref, r_ref, *rest, nk, emit):
    kk = pl.program_id(2)
    o_ref = rest[1] if emit else rest[0]

    @pl.when(kk == 0)
    def _():
        o_ref[...] = r_ref[...] + _dot(x_ref[...], w_ref[...])

    @pl.when(kk != 0)
    def _():
        o_ref[...] += _dot(x_ref[...], w_ref[...])

    if emit:
        g_ref, _, xg_ref, ssq_ref = rest

        @pl.when(kk == nk - 1)
        def _():
            _emit_normed(o_ref[...], g_ref, xg_ref, ssq_ref, pl.program_id(1) == 0)


def _matmul_down(x, w, resid, tm, tn, tk, gain=None):
    m, k = x.shape
    n = w.shape[1]
    emit = gain is not None
    tile = pl.BlockSpec((tm, tn), lambda i, j, kk: (i, j))
    in_specs = [pl.BlockSpec((tm, tk), lambda i, j, kk: (i, kk)),
                pl.BlockSpec((tk, tn), lambda i, j, kk: (kk, j)),
                tile]
    out_specs, out_shape, args = [tile], [jax.ShapeDtypeStruct((m, n), F32)], [x, w, resid]
    if emit:
        in_specs.append(pl.BlockSpec((1, tn), lambda i, j, kk: (0, j)))
        args.append(gain.reshape(1, n))
        out_specs += [tile, pl.BlockSpec((tm, LANES), lambda i, j, kk: (i, 0))]
        out_shape += [jax.ShapeDtypeStruct((m, n), BF16), jax.ShapeDtypeStruct((m, LANES), F32)]
    out = pl.pallas_call(
        functools.partial(_mm_down_kernel, nk=k // tk, emit=emit),
        grid=(m // tm, n // tn, k // tk),
        in_specs=in_specs,
        out_specs=out_specs,
        out_shape=out_shape,
        compiler_params=_cparams(("parallel", "arbitrary", "arbitrary")),
        name="mm_down",
    )(*args)
    return out if emit else out[0]


CAST_TN_GATE = 256
CAST_TK_DOWN = 512


def _masked_bf16(w, first, n_valid, axis):
    idx = first + lax.broadcasted_iota(jnp.int32, w.shape, axis)
    return jnp.where(idx < n_valid, w, 0.0).astype(BF16)


def _mm_in_cast_kernel(x_ref, w_ref, o_ref, wb_ref, *, n_valid, tn):
    idx = pl.program_id(0) * tn + lax.broadcasted_iota(jnp.int32, w_ref.shape, 0)
    wb = jnp.where(idx < n_valid, w_ref[...], 0.0).T.astype(BF16)
    wb_ref[...] = wb
    o_ref[...] = _dot(x_ref[...], wb)


def _mm_in_cast(x, w3, layer, n_pad, tn):
    m, k = x.shape
    n = w3.shape[2]
    return pl.pallas_call(
        functools.partial(_mm_in_cast_kernel, n_valid=n, tn=tn),
        grid=(n_pad // tn,),
        in_specs=[pl.BlockSpec((m, k), lambda j: (0, 0)),
                  pl.BlockSpec((None, tn, k), lambda j: (layer, j, 0))],
        out_specs=[pl.BlockSpec((m, tn), lambda j: (0, j)),
                   pl.BlockSpec((k, tn), lambda j: (0, j))],
        out_shape=[jax.ShapeDtypeStruct((m, n_pad), F32),
                   jax.ShapeDtypeStruct((k, n_pad), BF16)],
        compiler_params=_cparams(("arbitrary",)),
        name="mm_in_cast",
    )(x, jnp.swapaxes(w3, 1, 2))


def _mm_out_cast_kernel(oa_ref, ob_ref, oc_ref, w_ref, r_ref, o_ref, wb_ref, cat_ref):
    @pl.when(pl.program_id(0) == 0)
    def _():
        cat_ref[:, 0:A_WIDTH] = oa_ref[...]
        cat_ref[:, A_WIDTH:A_WIDTH + B_WIDTH] = ob_ref[...]
        cat_ref[:, A_WIDTH + B_WIDTH:] = oc_ref[...]

    wb = w_ref[...].astype(BF16)
    wb_ref[...] = wb
    o_ref[...] = r_ref[...] + _dot(cat_ref[...], wb)


def _mm_out_cast(oa, ob, oc, w3, resid, layer, tn):
    m = oa.shape[0]
    k, n = w3.shape[1:]
    return pl.pallas_call(
        _mm_out_cast_kernel,
        grid=(n // tn,),
        in_specs=[pl.BlockSpec((m, A_WIDTH), lambda j: (0, 0)),
                  pl.BlockSpec((m, B_WIDTH), lambda j: (0, 0)),
                  pl.BlockSpec((m, C_WIDTH), lambda j: (0, 0)),
                  pl.BlockSpec((None, k, tn), lambda j: (layer, 0, j)),
                  pl.BlockSpec((m, tn), lambda j: (0, j))],
        out_specs=[pl.BlockSpec((m, tn), lambda j: (0, j)),
                   pl.BlockSpec((k, tn), lambda j: (0, j))],
        out_shape=[jax.ShapeDtypeStruct((m, n), F32),
                   jax.ShapeDtypeStruct((k, n), BF16)],
        scratch_shapes=[pltpu.VMEM((m, k), BF16)],
        compiler_params=_cparams(("arbitrary",)),
        name="mm_out_cast",
    )(oa, ob, oc, w3, resid)


def _gate_up_cast_kernel(x_ref, wg_ref, wu_ref, o_ref, wgb_ref, wub_ref, *, n_tiles):
    valid = pl.program_id(0) < n_tiles
    wg = jnp.where(valid, wg_ref[...], 0.0).astype(BF16)
    wu = jnp.where(valid, wu_ref[...], 0.0).astype(BF16)
    wgb_ref[...] = wg
    wub_ref[...] = wu
    x = x_ref[...]
    gate = _dot(x, wg)
    o_ref[...] = (gate * (1.0 / (1.0 + jnp.exp(-gate))) * _dot(x, wu)).astype(o_ref.dtype)


def _gate_up_cast(x, wgu3, layer, ff_pad):
    m, k = x.shape
    tn = CAST_TN_GATE
    n_tiles = wgu3.shape[2] // 2 // tn
    assert n_tiles * tn * 2 == wgu3.shape[2]
    return pl.pallas_call(
        functools.partial(_gate_up_cast_kernel, n_tiles=n_tiles),
        grid=(ff_pad // tn,),
        in_specs=[pl.BlockSpec((m, k), lambda j: (0, 0)),
                  pl.BlockSpec((None, k, tn), lambda j: (layer, 0, jnp.minimum(j, n_tiles - 1))),
                  pl.BlockSpec((None, k, tn),
                               lambda j: (layer, 0, n_tiles + jnp.minimum(j, n_tiles - 1)))],
        out_specs=[pl.BlockSpec((m, tn), lambda j: (0, j)),
                   pl.BlockSpec((k, tn), lambda j: (0, j)),
                   pl.BlockSpec((k, tn), lambda j: (0, j))],
        out_shape=[jax.ShapeDtypeStruct((m, ff_pad), BF16),
                   jax.ShapeDtypeStruct((k, ff_pad), BF16),
                   jax.ShapeDtypeStruct((k, ff_pad), BF16)],
        compiler_params=_cparams(("arbitrary",)),
        name="gate_up_cast",
    )(x, wgu3, wgu3)


def _down_cast_kernel(x_ref, w_ref, r_ref, o_ref, wb_ref, *, k_valid, tk):
    kk = pl.program_id(0)
    wb = _masked_bf16(w_ref[...], kk * tk, k_valid, 0)
    wb_ref[...] = wb

    @pl.when(kk == 0)
    def _():
        o_ref[...] = r_ref[...] + _dot(x_ref[...], wb)

    @pl.when(kk != 0)
    def _():
        o_ref[...] += _dot(x_ref[...], wb)


def _down_cast(x, wd3, resid, layer):
    m, ff_pad = x.shape
    ff, d = wd3.shape[1:]
    tk = CAST_TK_DOWN
    last = (ff - 1) // tk
    return pl.pallas_call(
        functools.partial(_down_cast_kernel, k_valid=ff, tk=tk),
        grid=(ff_pad // tk,),
        in_specs=[pl.BlockSpec((m, tk), lambda kk: (0, kk)),
                  pl.BlockSpec((None, tk, d), lambda kk: (layer, jnp.minimum(kk, last), 0)),
                  pl.BlockSpec((m, d), lambda kk: (0, 0))],
        out_specs=[pl.BlockSpec((m, d), lambda kk: (0, 0)),
                   pl.BlockSpec((tk, d), lambda kk: (kk, 0))],
        out_shape=[jax.ShapeDtypeStruct((m, d), F32),
                   jax.ShapeDtypeStruct((ff_pad, d), BF16)],
        compiler_params=_cparams(("arbitrary",)),
        name="down_cast",
    )(x, wd3, resid)


LOG2E = 1.4426950408889634
SUBLANES = 8
HGRN_W = 256
GLA_W = 256
GLR_UNROLL = 8


def _levels(c):
    out, b = [], 1
    while b < c:
        out.append(b)
        b *= 2
    return out


def _prefix_matrix(c):
    t = np.arange(c)[:, None]
    s = np.arange(c)[None, :]
    low = s <= t
    mats = [low, ~low]
    for b in _levels(c):
        same = (t // b) == (s // b)
        odd = ((t // b) % 2) == 1
        mats.append(np.where(odd, low & same, (~low) & same))
    m = np.concatenate(mats, axis=0).astype(np.float32)
    return jnp.asarray(np.concatenate([m, m, m], axis=1), dtype=BF16)


def _diag_matrix(c, nh, w):
    kd = w // nh
    lane = np.arange(w)[:, None]
    col = np.arange(nh * c)[None, :]
    return jnp.asarray(((col // c) == (lane // kd)).astype(np.float32), dtype=BF16)


def _glr_consts(c, nh, w):
    t = lax.broadcasted_iota(jnp.int32, (c, nh * c), 0)
    s = lax.broadcasted_iota(jnp.int32, (c, nh * c), 1) & (c - 1)
    lvl_masks = []
    for b in _levels(c):
        lg = b.bit_length() - 1
        tb = lax.shift_right_logical(t, lg)
        sb = lax.shift_right_logical(s, lg)
        lvl_masks.append((lax.shift_right_logical(tb, 1) == lax.shift_right_logical(sb, 1))
                         & ((tb & 1) == 1) & ((sb & 1) == 0))
    lgk = (w // nh).bit_length() - 1
    lane = lax.shift_right_logical(lax.broadcasted_iota(jnp.int32, (1, w), 1), lgk)
    srow = lax.broadcasted_iota(jnp.int32, (nh * LANES, w), 0) // LANES
    scol = lax.shift_right_logical(lax.broadcasted_iota(jnp.int32, (nh * LANES, w), 1), lgk)
    return dict(lvl_masks=lvl_masks, diag_valid=t == s,
                head_lanes=[lane == h for h in range(nh)], st_mask=srow == scol)


def _head_rows(x, cst, nh):
    if nh * LANES == x.shape[1]:
        z = jnp.zeros((x.shape[0], LANES), x.dtype)
        return jnp.concatenate(
            [jnp.concatenate([x[:, h * LANES:(h + 1) * LANES] if g == h else z
                              for g in range(nh)], axis=1) for h in range(nh)], axis=0)
    zero = jnp.zeros_like(x)
    return jnp.concatenate([jnp.where(cst["head_lanes"][h], x, zero) for h in range(nh)], axis=0)


def _glr_prefix(lf2, pm):
    hi = lf2.astype(BF16)
    r1 = lf2 - hi.astype(F32)
    mid = r1.astype(BF16)
    lo = (r1 - mid.astype(F32)).astype(BF16)
    return _dot(pm, jnp.concatenate([hi, mid, lo], axis=0))


def _glr_intra(q, k, pref, rd, cst, *, c, nh):
    g_all = pref[0:c]
    qhat = (q * jnp.exp2(g_all)).astype(BF16)
    kdec = (k * jnp.exp2(pref[c:2 * c])).astype(BF16)
    a = jnp.where(cst["diag_valid"], _dot((q * k).astype(BF16), rd), 0.0)
    for i in range(len(cst["lvl_masks"])):
        eb = jnp.exp2(pref[(2 + i) * c:(3 + i) * c])
        qb = (q * eb).astype(BF16)
        kb = _head_rows((k * eb).astype(BF16), cst, nh)
        a = jnp.where(cst["lvl_masks"][i], _dot_nt(qb, kb), a)
    return a.astype(BF16), qhat, kdec, jnp.exp2(g_all[c - 1:c, :])


def _glr_state(a, qhat, kdec, dec, vs, st, cst, *, nh):
    vb = [v.astype(BF16) for v in vs]
    z = jnp.zeros_like(vb[0])
    vbd = jnp.concatenate(
        [jnp.concatenate([vb[h] if g == h else z for g in range(nh)], axis=1) for h in range(nh)],
        axis=0)
    o = _dot(a, vbd) + _dot_nt(qhat, st.astype(BF16))
    u = _dot_tn(jnp.concatenate(vb, axis=1), kdec)
    return o, st * dec + jnp.where(cst["st_mask"], u, 0.0)


def _gated_norm(o, norm, gate):
    var = jnp.mean(o * o, axis=-1, keepdims=True)
    return o * lax.rsqrt(var + EPS) * norm * (gate * (1.0 / (1.0 + jnp.exp(-gate))))


def _glr_scratch(t, c, nh, w):
    return [pltpu.VMEM((t, nh * c), BF16), pltpu.VMEM((t, w), BF16),
            pltpu.VMEM((t, w), BF16), pltpu.VMEM((t // c, SUBLANES, w), F32)]


def _hgrn_kernel(*refs, c, nchunks, layer, has_s0):
    if has_s0:
        (pm_ref, rd_ref, lbl_ref, norm_ref, q_ref, z_ref, v_ref, g_ref, s0_ref, o_ref, s_ref,
         a_s, qh_s, kd_s, dec_s) = refs
    else:
        (pm_ref, rd_ref, lbl_ref, norm_ref, q_ref, z_ref, v_ref, g_ref, o_ref, s_ref,
         a_s, qh_s, kd_s, dec_s) = refs
    w = HGRN_W
    nh = w // A_DK
    logits = lbl_ref[...]
    e = jnp.exp(logits - jnp.max(logits, axis=0, keepdims=True))
    probs = e / jnp.sum(e, axis=0, keepdims=True)
    lb = jnp.sum(probs[0:layer + 1], axis=0, keepdims=True) - probs[0:1]
    lb_floor = jnp.maximum(lb, LB_FLOOR)
    oml = 1.0 - lb
    norm = norm_ref[...]
    pm = pm_ref[...]
    rd = rd_ref[...]
    cst = _glr_consts(c, nh, w)
    heads = [slice(h * LANES, (h + 1) * LANES) for h in range(nh)]

    def intra(n, carry):
        rows = pl.ds(pl.multiple_of(n * c, c), c)
        q = q_ref[rows, :]
        z = z_ref[rows, :]
        ez = jnp.exp(-jnp.abs(z))
        r = 1.0 / (1.0 + ez)
        pos = z >= 0.0
        lf2 = jnp.log2(lb_floor + oml * jnp.where(pos, r, ez * r))
        k = oml * jnp.where(pos, ez * r, r)
        a, qh, kd, dec = _glr_intra(q, k, _glr_prefix(lf2, pm), rd, cst, c=c, nh=nh)
        a_s[rows, :] = a
        qh_s[rows, :] = qh
        kd_s[rows, :] = kd
        dec_s[n] = jnp.broadcast_to(dec, (SUBLANES, w))
        return carry

    lax.fori_loop(0, nchunks, intra, 0, unroll=min(GLR_UNROLL, nchunks))

    def state(n, st):
        rows = pl.ds(pl.multiple_of(n * c, c), c)
        v = v_ref[rows, :]
        g = g_ref[rows, :]
        o, st2 = _glr_state(a_s[rows, :], qh_s[rows, :], kd_s[rows, :], dec_s[n, 0:1, :],
                            [v[:, sl] for sl in heads], st, cst, nh=nh)
        o_ref[rows, :] = jnp.concatenate(
            [_gated_norm(o[:, sl], norm, g[:, sl]) for sl in heads], axis=1).astype(o_ref.dtype)
        return st2

    zero = jnp.zeros((A_DV, A_DK), F32)
    if has_s0:
        st0 = jnp.concatenate(
            [jnp.concatenate([s0_ref[0, h].T if g == h else zero for g in range(nh)], axis=1)
             for h in range(nh)], axis=0)
    else:
        st0 = jnp.zeros((nh * A_DV, w), F32)
    st = lax.fori_loop(0, nchunks, state, st0, unroll=min(GLR_UNROLL, nchunks))
    for h in range(nh):
        s_ref[0, h] = st[h * A_DV:(h + 1) * A_DV, heads[h]].T


def _hgrn(proj, lbl, norm, s0, layer, nb, t, c):
    w = HGRN_W
    nh = w // A_DK
    pm = _prefix_matrix(c)
    rd = _diag_matrix(c, nh, w)
    nchunks = t // c
    has_s0 = s0 is not None

    def col(off):
        return pl.BlockSpec((t, w), lambda b, h: (b, off // nh + h))

    st_spec = pl.BlockSpec((1, nh, A_DK, A_DV), lambda b, h: (b, h, 0, 0))
    in_specs = [pl.BlockSpec(pm.shape, lambda b, h: (0, 0)),
                pl.BlockSpec(rd.shape, lambda b, h: (0, 0)),
                pl.BlockSpec((DEPTH, w), lambda b, h: (0, h)),
                pl.BlockSpec((1, LANES), lambda b, h: (0, 0)),
                col(COL_A_Q), col(COL_A_F), col(COL_A_I), col(COL_A_G)]
    args = [pm, rd, lbl, norm.reshape(1, LANES), proj, proj, proj, proj]
    if has_s0:
        in_specs.append(st_spec)
        args.append(s0)
    return pl.pallas_call(
        functools.partial(_hgrn_kernel, c=c, nchunks=nchunks, layer=layer, has_s0=has_s0),
        grid=(nb, A_HEADS // nh),
        in_specs=in_specs,
        out_specs=[pl.BlockSpec((t, w), lambda b, h: (b, h)), st_spec],
        out_shape=[jax.ShapeDtypeStruct((nb * t, A_WIDTH), BF16),
                   jax.ShapeDtypeStruct((nb, A_HEADS, A_DK, A_DV), F32)],
        scratch_shapes=_glr_scratch(t, c, nh, w),
        compiler_params=_cparams(("parallel", "parallel")),
        name="hgrn",
    )(*args)


def _gla_kernel(*refs, c, nchunks, has_s0):
    if has_s0:
        (pm_ref, rd_ref, wa_ref, ba_ref, norm_ref, q_ref, k_ref, v_ref, r_ref, ca_ref,
         s0_ref, o_ref, s_ref, a_s, qh_s, kd_s, dec_s) = refs
    else:
        (pm_ref, rd_ref, wa_ref, ba_ref, norm_ref, q_ref, k_ref, v_ref, r_ref, ca_ref,
         o_ref, s_ref, a_s, qh_s, kd_s, dec_s) = refs
    wa = wa_ref[...]
    ba = ba_ref[...]
    norm = norm_ref[...]
    pm = pm_ref[...]
    rd = rd_ref[...]
    w = GLA_W
    nh = w // C_DK
    npl = w // LANES
    cst = _glr_consts(c, nh, w)
    heads = [slice(h * C_DV, (h + 1) * C_DV) for h in range(nh)]
    lane = lax.broadcasted_iota(jnp.int32, (1, LANES), 1)
    half = [lane < C_DK, lane >= C_DK]

    def intra(n, carry):
        rows = pl.ds(pl.multiple_of(n * c, c), c)
        x = _dot(ca_ref[rows, :].astype(BF16), wa) + ba
        lf2 = (jnp.minimum(x, 0.0) - jnp.log1p(jnp.exp(-jnp.abs(x)))) * (LOG2E / GLA_NORMALIZER)
        q = q_ref[rows, :] * (C_DK ** -0.5)
        k = k_ref[rows, :]
        a, qh, kd, dec = _glr_intra(q, k, _glr_prefix(lf2, pm), rd, cst, c=c, nh=nh)
        a_s[rows, :] = a
        qh_s[rows, :] = qh
        kd_s[rows, :] = kd
        dec_s[n] = jnp.broadcast_to(dec, (SUBLANES, w))
        return carry

    lax.fori_loop(0, nchunks, intra, 0, unroll=min(GLR_UNROLL, nchunks))

    def state(n, st):
        rows = pl.ds(pl.multiple_of(n * c, c), c)
        v = v_ref[rows, :]
        gate = r_ref[rows, :]
        o, st2 = _glr_state(a_s[rows, :], qh_s[rows, :], kd_s[rows, :], dec_s[n, 0:1, :],
                            [v[:, sl] for sl in heads], st, cst, nh=nh)
        o_ref[rows, :] = jnp.concatenate(
            [_gated_norm(o[:, sl], norm, gate[:, sl]) for sl in heads], axis=1).astype(o_ref.dtype)
        return st2

    zero = jnp.zeros((C_DV, LANES), F32)
    if has_s0:
        blocks = []
        for h in range(nh):
            p, h2 = divmod(h, 2)
            own = jnp.where(half[h2], s0_ref[0, p].T, 0.0)
            blocks.append(jnp.concatenate([own if g == p else zero for g in range(npl)], axis=1))
        st0 = jnp.concatenate(blocks, axis=0)
    else:
        st0 = jnp.zeros((nh * C_DV, w), F32)
    st = lax.fori_loop(0, nchunks, state, st0, unroll=min(GLR_UNROLL, nchunks))
    for p in range(npl):
        cols = slice(p * LANES, (p + 1) * LANES)
        pair = st[2 * p * C_DV:(2 * p + 1) * C_DV, cols] + st[(2 * p + 1) * C_DV:(2 * p + 2) * C_DV, cols]
        s_ref[0, p] = pair.T


def _gla(proj, wa2, ba, norm, s0, nb, t, c):
    w = GLA_W
    nh = w // C_DK
    npl = w // LANES
    pm = _prefix_matrix(c)
    rd = _diag_matrix(c, nh, w)
    nchunks = t // c
    has_s0 = s0 is not None
    npair = C_HEADS // 2

    def col(off, width):
        return pl.BlockSpec((t, width), lambda b, p: (b, off * LANES // width + p))

    st_spec = pl.BlockSpec((1, npl, LANES, C_DV), lambda b, p: (b, p, 0, 0))
    in_specs = [pl.BlockSpec(pm.shape, lambda b, p: (0, 0)),
                pl.BlockSpec(rd.shape, lambda b, p: (0, 0)),
                pl.BlockSpec((LANES, w), lambda b, p: (0, p)),
                pl.BlockSpec((1, w), lambda b, p: (0, p)),
                pl.BlockSpec((1, LANES), lambda b, p: (0, 0)),
                col(COL_C_Q, w), col(COL_C_K, w), col(COL_C_V, 2 * w), col(COL_C_R, 2 * w),
                pl.BlockSpec((t, LANES), lambda b, p: (b, COL_C_A))]
    args = [pm, rd, wa2, ba.reshape(1, -1), norm.reshape(1, LANES), proj, proj, proj, proj, proj]
    if has_s0:
        in_specs.append(st_spec)
        args.append(s0.reshape(nb, npair, 2 * C_DK, C_DV))
    o, s = pl.pallas_call(
        functools.partial(_gla_kernel, c=c, nchunks=nchunks, has_s0=has_s0),
        grid=(nb, npair // npl),
        in_specs=in_specs,
        out_specs=[pl.BlockSpec((t, 2 * w), lambda b, p: (b, p)), st_spec],
        out_shape=[jax.ShapeDtypeStruct((nb * t, C_WIDTH), BF16),
                   jax.ShapeDtypeStruct((nb, npair, 2 * C_DK, C_DV), F32)],
        scratch_shapes=_glr_scratch(t, c, nh, w),
        compiler_params=_cparams(("parallel", "parallel")),
        name="gla",
    )(*args)
    return o, s.reshape(nb, C_HEADS, C_DK, C_DV)


SWA_CB = 4
Q_SCALE = B_SCALE * 1.4426950408889634


def _rope_tables(pos):
    half = B_HD // 2
    inv = ROPE_THETA ** (-jnp.arange(half, dtype=F32) / half)
    ang = pos.astype(F32)[:, None] * inv[None, :]
    cos = jnp.cos(ang)
    sin = jnp.sin(ang)
    cos_t = jnp.tile(cos, (1, 2 * LANES // B_HD))
    sin_t = jnp.tile(jnp.concatenate([-sin, sin], axis=-1), (1, LANES // B_HD))
    return cos_t, sin_t


def _rope(x, cos_t, sin_t):
    n = x.shape[1]
    half = B_HD // 2
    lane = lax.broadcasted_iota(jnp.int32, (1, n), 1)
    first = (lane & (B_HD - 1)) < half
    rot = jnp.where(first, pltpu.roll(x, n - half, axis=1), pltpu.roll(x, half, axis=1))
    reps = n // LANES
    if reps > 1:
        cos_t = jnp.concatenate([cos_t] * reps, axis=1)
        sin_t = jnp.concatenate([sin_t] * reps, axis=1)
    return x * cos_t + rot * sin_t


def _attend(qs, kb, vb, sink_row, valid):
    s = _dot_nt(kb, qs)
    if valid is not None:
        s = jnp.where(valid, s, NEG_BIG)
    sink2 = sink_row * LOG2E
    m = jnp.maximum(jnp.max(s, axis=0, keepdims=True), sink2)
    p = jnp.exp2(s - m)
    den = jnp.sum(p, axis=0, keepdims=True) + jnp.exp2(sink2 - m)
    o_t = _dot_tn(vb, p.astype(BF16)) * (1.0 / den)
    return o_t.T


def _sink_row(sink_ref, base, t):
    lanes = lax.broadcasted_iota(jnp.int32, (1, B_GROUP * t), 1)
    row = jnp.zeros((1, B_GROUP * t), F32)
    for g in range(B_GROUP):
        row = jnp.where((lanes >= g * t) & (lanes < (g + 1) * t), sink_ref[base + g], row)
    return row


def _swa_prompt_kernel(sink_ref, q_ref, k_ref, v_ref, cq_ref, sq_ref, ck_ref, sk_ref,
                       o_ref, kr_ref, vr_ref, kro_ref, vbo_ref, *, t, layer):
    khp = pl.program_id(1)
    step = pl.program_id(2)

    @pl.when(step == 0)
    def _():
        kr = _rope(k_ref[...], ck_ref[...], sk_ref[...])
        kro_ref[...] = kr.astype(BF16)
        vbo_ref[...] = v_ref[...].astype(BF16)
        kr_ref[0] = kr[t - WINDOW:, :]
        vr_ref[0] = v_ref[t - WINDOW:, :]

    band = (WINDOW_CHUNKS + 1) * CHUNK
    sink_rows = [_sink_row(sink_ref, layer * B_HEADS + (khp * 2 + kv) * B_GROUP, CHUNK)
                 for kv in range(2)]
    for ci in range(SWA_CB):
        cidx = step * SWA_CB + ci
        rows = slice(ci * CHUNK, (ci + 1) * CHUNK)
        s0 = pl.multiple_of(jnp.maximum(cidx - WINDOW_CHUNKS, 0) * CHUNK, CHUNK)
        qr = _rope(q_ref[rows, :], cq_ref[rows, :], sq_ref[rows, :]).astype(BF16)
        kband = kro_ref[pl.ds(s0, band), :]
        vband = vbo_ref[pl.ds(s0, band), :]
        key_pos = s0 + lax.broadcasted_iota(jnp.int32, (band, 1), 0)
        valid = key_pos < (cidx + 1) * CHUNK
        for kv in range(2):
            base = kv * B_GROUP * B_HD
            qs = jnp.concatenate(
                [qr[:, base + g * B_HD:base + (g + 1) * B_HD] for g in range(B_GROUP)], axis=0)
            o = _attend(qs, kband[:, kv * B_HD:(kv + 1) * B_HD],
                        vband[:, kv * B_HD:(kv + 1) * B_HD], sink_rows[kv], valid)
            for g in range(0, B_GROUP, 2):
                o_ref[rows, base + g * B_HD:base + (g + 2) * B_HD] = jnp.concatenate(
                    [o[g * CHUNK:(g + 1) * CHUNK], o[(g + 1) * CHUNK:(g + 2) * CHUNK]],
                    axis=1).astype(o_ref.dtype)


def _swa_prompt(proj, sinks, layer, nb, t):
    rb = SWA_CB * CHUNK
    nc = t // rb
    cos_t, sin_t = _rope_tables(jnp.arange(t))
    qw = 2 * B_GROUP * B_HD
    qoff = COL_B_Q * LANES // qw
    kernel = functools.partial(_swa_prompt_kernel, t=t, layer=layer)
    return pl.pallas_call(
        kernel,
        grid=(nb, B_KV_HEADS // 2, nc),
        in_specs=[pl.BlockSpec(memory_space=pltpu.SMEM),
                  pl.BlockSpec((rb, qw), lambda b, p, c: (b * nc + c, qoff + p)),
                  pl.BlockSpec((t, LANES), lambda b, p, c: (b, COL_B_K + p)),
                  pl.BlockSpec((t, LANES), lambda b, p, c: (b, COL_B_V + p)),
                  pl.BlockSpec((rb, LANES), lambda b, p, c: (c, 0)),
                  pl.BlockSpec((rb, LANES), lambda b, p, c: (c, 0)),
                  pl.BlockSpec((t, LANES), lambda b, p, c: (0, 0)),
                  pl.BlockSpec((t, LANES), lambda b, p, c: (0, 0))],
        out_specs=[pl.BlockSpec((rb, qw), lambda b, p, c: (b * nc + c, p)),
                   pl.BlockSpec((1, WINDOW, LANES), lambda b, p, c: (b, 0, p)),
                   pl.BlockSpec((1, WINDOW, LANES), lambda b, p, c: (b, 0, p))],
        out_shape=[jax.ShapeDtypeStruct((nb * t, B_WIDTH), BF16),
                   jax.ShapeDtypeStruct((nb, WINDOW, B_KV_HEADS * B_HD), F32),
                   jax.ShapeDtypeStruct((nb, WINDOW, B_KV_HEADS * B_HD), F32)],
        scratch_shapes=[pltpu.VMEM((t, LANES), BF16), pltpu.VMEM((t, LANES), BF16)],
        compiler_params=_cparams(("parallel", "parallel", "arbitrary")),
        name="swa_prompt",
    )(sinks.reshape(-1), proj, proj, proj, cos_t * Q_SCALE, sin_t * Q_SCALE, cos_t, sin_t)


def _swa_sample_kernel(sink_ref, q_ref, k_ref, v_ref, ckc_ref, cvc_ref, cos_ref, sin_ref,
                       o_ref, kr_ref, vr_ref, *, t, layer):
    cos_t = cos_ref[...]
    sin_t = sin_ref[...]
    kr = _rope(k_ref[...], cos_t, sin_t)
    v = v_ref[...]
    kr_ref[...] = kr
    vr_ref[...] = v
    qr = (_rope(q_ref[...], cos_t, sin_t) * Q_SCALE).astype(BF16)
    ka = jnp.concatenate([ckc_ref[0], kr], axis=0).astype(BF16)
    va = jnp.concatenate([cvc_ref[0], v], axis=0).astype(BF16)
    for kh in range(B_KV_HEADS):
        base = kh * B_GROUP * B_HD
        qs = jnp.concatenate(
            [qr[:, base + g * B_HD:base + (g + 1) * B_HD] for g in range(B_GROUP)], axis=0)
        sink_row = _sink_row(sink_ref, layer * B_HEADS + kh * B_GROUP, t)
        o = _attend(qs, ka[:, kh * B_HD:(kh + 1) * B_HD], va[:, kh * B_HD:(kh + 1) * B_HD],
                    sink_row, None)
        for g in range(0, B_GROUP, 2):
            o_ref[:, base + g * B_HD:base + (g + 2) * B_HD] = jnp.concatenate(
                [o[g * t:(g + 1) * t], o[(g + 1) * t:(g + 2) * t]], axis=1).astype(o_ref.dtype)


def _swa_sample(proj, cache_k, cache_v, sinks, layer, nb, t):
    cos_t, sin_t = _rope_tables(PAST_LEN + jnp.arange(t))
    kvw = B_KV_HEADS * B_HD
    kernel = functools.partial(_swa_sample_kernel, t=t, layer=layer)
    return pl.pallas_call(
        kernel,
        grid=(nb,),
        in_specs=[pl.BlockSpec(memory_space=pltpu.SMEM),
                  pl.BlockSpec((t, B_WIDTH), lambda b: (b, COL_B_Q * LANES // B_WIDTH)),
                  pl.BlockSpec((t, kvw), lambda b: (b, COL_B_K * LANES // kvw)),
                  pl.BlockSpec((t, kvw), lambda b: (b, COL_B_V * LANES // kvw)),
                  pl.BlockSpec((1, WINDOW, kvw), lambda b: (b, 0, 0)),
                  pl.BlockSpec((1, WINDOW, kvw), lambda b: (b, 0, 0)),
                  pl.BlockSpec((t, LANES), lambda b: (0, 0)),
                  pl.BlockSpec((t, LANES), lambda b: (0, 0))],
        out_specs=[pl.BlockSpec((t, B_WIDTH), lambda b: (b, 0)),
                   pl.BlockSpec((t, kvw), lambda b: (b, 0)),
                   pl.BlockSpec((t, kvw), lambda b: (b, 0))],
        out_shape=[jax.ShapeDtypeStruct((nb * t, B_WIDTH), BF16),
                   jax.ShapeDtypeStruct((nb * t, kvw), F32),
                   jax.ShapeDtypeStruct((nb * t, kvw), F32)],
        compiler_params=_cparams(("parallel",)),
        name="swa_sample",
    )(sinks.reshape(-1), proj, proj, proj,
      cache_k.reshape(nb, WINDOW, kvw), cache_v.reshape(nb, WINDOW, kvw), cos_t, sin_t)


def _layer_sample(x, w, layer, nb, t, cache, s_a, s_c, ff_pad):
    m, d = x.shape
    h = _rmsnorm(x, w["norm_mix"][layer], BF16, m)
    proj, wb_in = _mm_in_cast(h, w["w_in"], layer, IN_COLS_PAD, MM_TN_IN)
    o_a, s_a_new = _hgrn(proj, w["lb_logits"], w["hgrn_norm"][layer], s_a, layer, nb, t, t)
    o_b, k_rows, v_rows = _swa_sample(proj, cache[0], cache[1], w["sinks"], layer, nb, t)
    o_c, s_c_new = _gla(proj, w["w_a2"][layer], w["b_a"][layer], w["gla_norm"][layer],
                        s_c, nb, t, t)
    x, wb_out = _mm_out_cast(o_a, o_b, o_c, w["w_out"], x, layer, min(d, 512))
    h = _rmsnorm(x, w["norm_ffn"][layer], BF16, m)
    mid, wb_gate, wb_up = _gate_up_cast(h, w["w_gate_up"], layer, ff_pad)
    x, wb_down = _down_cast(mid, w["w_down"], x, layer)
    outs = (k_rows.reshape(nb, t, B_KV_HEADS, B_HD), v_rows.reshape(nb, t, B_KV_HEADS, B_HD),
            s_a_new, s_c_new)
    return x, outs, dict(w_in=wb_in, w_out=wb_out, w_gate=wb_gate, w_up=wb_up, w_down=wb_down)


def _layer_prompt(x, normed, w, wb, layer, nb, t):
    m, d = x.shape
    tm = min(m, 1024)
    if normed is None:
        h = _rmsnorm(x, w["norm_mix"][layer], BF16, min(m, 256))
        proj = _matmul(h, wb["w_in"], tm, MM_TN_IN, F32)
    else:
        proj = _matmul(normed[0], wb["w_in"], tm, MM_TN_IN, F32, normed[1])
    o_a, s_a_new = _hgrn(proj, w["lb_logits"], w["hgrn_norm"][layer], None, layer, nb, t, CHUNK)
    o_b, k_rows, v_rows = _swa_prompt(proj, w["sinks"], layer, nb, t)
    o_c, s_c_new = _gla(proj, w["w_a2"][layer], w["b_a"][layer], w["gla_norm"][layer],
                        None, nb, t, CHUNK)
    x, xg, ssq = _matmul_out(o_a, o_b, o_c, wb["w_out"], x, w["norm_ffn"][layer], tm, min(d, 512))
    mid = _matmul_gate_up(xg, ssq, wb["w_gate"], wb["w_up"], tm, 512)
    tk = wb["w_down"].shape[0] // 4
    if layer + 1 < DEPTH:
        x, xg, ssq = _matmul_down(mid, wb["w_down"], x, tm, min(d, 1024), tk,
                                  w["norm_mix"][layer + 1])
        normed = (xg, ssq)
    else:
        x = _matmul_down(mid, wb["w_down"], x, tm, min(d, 1024), tk)
        normed = None
    outs = (k_rows.reshape(nb, WINDOW, B_KV_HEADS, B_HD),
            v_rows.reshape(nb, WINDOW, B_KV_HEADS, B_HD), s_a_new, s_c_new)
    return x, normed, outs


def kernel(x_prompt, x_sample, cache_k_swa, cache_v_swa, state_hgrn, state_gla, norm_mix, w_in,
           hgrn_lb_logits, hgrn_norm, swa_sinks, gla_w_alpha2, gla_b_alpha, gla_norm, w_out,
           norm_ffn, w_gate_up, w_down, norm_final):
    n_p, t_p, d = x_prompt.shape
    n_s, t_s, _ = x_sample.shape
    d_ff = w_down.shape[1]
    ff_pad = -(-d_ff // 1024) * 1024
    w = dict(
        norm_mix=norm_mix, norm_ffn=norm_ffn, hgrn_norm=hgrn_norm, gla_norm=gla_norm,
        lb_logits=hgrn_lb_logits, sinks=swa_sinks, b_a=gla_b_alpha,
        w_in=w_in, w_out=w_out, w_gate_up=w_gate_up, w_down=w_down,
        w_a2=jnp.pad(gla_w_alpha2, ((0, 0), (0, LANES - GLA_RANK), (0, 0))).astype(BF16),
    )
    xp = x_prompt.reshape(n_p * t_p, d)
    xs = x_sample.reshape(n_s * t_s, d)
    outs_p, outs_s = [], []
    normed = None
    for layer in range(DEPTH):
        xs, rest, wb = _layer_sample(xs, w, layer, n_s, t_s,
                                     (cache_k_swa[layer], cache_v_swa[layer]),
                                     state_hgrn[layer], state_gla[layer], ff_pad)
        outs_s.append(rest)
        xp, normed, rest = _layer_prompt(xp, normed, w, wb, layer, n_p, t_p)
        outs_p.append(rest)
    y_p = _rmsnorm(xp, norm_final, F32, min(xp.shape[0], 256)).reshape(n_p, t_p, d)
    y_s = _rmsnorm(xs, norm_final, F32, min(xs.shape[0], 256)).reshape(n_s, t_s, d)
    stack = lambda outs, i: jnp.stack([o[i] for o in outs])
    return (y_p, y_s,
            stack(outs_p, 0), stack(outs_p, 1), stack(outs_p, 2), stack(outs_p, 3),
            stack(outs_s, 0), stack(outs_s, 1), stack(outs_s, 2), stack(outs_s, 3))
```

```python
import functools

import jax
import jax.numpy as jnp
import numpy as np
from jax import lax
from jax.experimental import pallas as pl
from jax.experimental.pallas import tpu as pltpu

F32 = jnp.float32
BF16 = jnp.bfloat16

DEPTH = 2
PAST_LEN = 4096
CHUNK = 64
EPS = 1e-6
NEG_BIG = -1e30
LB_FLOOR = 1e-30
A_HEADS = 8
A_DK = 128
A_DV = 128
A_WIDTH = A_HEADS * A_DV
B_HEADS = 32
B_KV_HEADS = 4
B_GROUP = B_HEADS // B_KV_HEADS
B_HD = 64
B_WIDTH = B_HEADS * B_HD
B_SCALE = B_HD ** -0.5
WINDOW = 128
WINDOW_CHUNKS = WINDOW // CHUNK
ROPE_THETA = 10000.0
C_HEADS = 8
C_DK = 64
C_DV = 128
C_WIDTH = C_HEADS * C_DV
GLA_RANK = 16
GLA_NORMALIZER = 16.0

LANES = 128

COL_A_Q = 0
COL_A_F = 8
COL_A_I = 16
COL_A_G = 24
COL_B_Q = 32
COL_B_K = 48
COL_B_V = 50
COL_C_Q = 52
COL_C_K = 56
COL_C_V = 60
COL_C_R = 68
COL_C_A = 76
IN_COLS = 9744
IN_COLS_PAD = 9984
MM_TN_IN = 768
VMEM_LIMIT = 56 * 1024 * 1024


def _cparams(sem):
    return pltpu.CompilerParams(dimension_semantics=sem, vmem_limit_bytes=VMEM_LIMIT)


def _dot(a, b):
    return jnp.dot(a, b, preferred_element_type=F32)


def _dot_nt(a, b):
    return lax.dot_general(a, b, (((1,), (1,)), ((), ())), preferred_element_type=F32)


def _dot_tn(a, b):
    return lax.dot_general(a, b, (((0,), (0,)), ((), ())), preferred_element_type=F32)


def _rmsnorm_kernel(x_ref, g_ref, o_ref):
    x = x_ref[...]
    var = jnp.mean(x * x, axis=-1, keepdims=True)
    o_ref[...] = (x * lax.rsqrt(var + EPS) * g_ref[...]).astype(o_ref.dtype)


def _rmsnorm(x, g, out_dtype, tm):
    m, d = x.shape
    return pl.pallas_call(
        _rmsnorm_kernel,
        grid=(m // tm,),
        in_specs=[pl.BlockSpec((tm, d), lambda i: (i, 0)),
                  pl.BlockSpec((1, d), lambda i: (0, 0))],
        out_specs=pl.BlockSpec((tm, d), lambda i: (i, 0)),
        out_shape=jax.ShapeDtypeStruct((m, d), out_dtype),
        compiler_params=_cparams(("parallel",)),
        name="rmsnorm",
    )(x, g.reshape(1, d))


def _row_scale(ssq_ref, d):
    return lax.rsqrt(jnp.sum(ssq_ref[...], axis=-1, keepdims=True) * (1.0 / d) + EPS)


def _lane_partial_sq(x):
    sq = x * x
    part = sq[:, 0:LANES]
    for c in range(1, x.shape[1] // LANES):
        part = part + sq[:, c * LANES:(c + 1) * LANES]
    return part


def _accumulate(ref, part, first):
    @pl.when(first)
    def _():
        ref[...] = part

    @pl.when(jnp.logical_not(first))
    def _():
        ref[...] += part


def _emit_normed(x, g_ref, xg_ref, ssq_ref, first):
    xg_ref[...] = (x * g_ref[...]).astype(xg_ref.dtype)
    _accumulate(ssq_ref, _lane_partial_sq(x), first)


def _mm_kernel(x_ref, w_ref, o_ref):
    o_ref[...] = _dot(x_ref[...], w_ref[...]).astype(o_ref.dtype)


def _mm_scaled_kernel(x_ref, s_ref, w_ref, o_ref):
    r = _row_scale(s_ref, x_ref.shape[1])
    o_ref[...] = (_dot(x_ref[...], w_ref[...]) * r).astype(o_ref.dtype)


def _matmul(x, w, tm, tn, out_dtype, ssq=None):
    m, k = x.shape
    n = w.shape[1]
    x_spec = pl.BlockSpec((tm, k), lambda i, j: (i, 0))
    w_spec = pl.BlockSpec((k, tn), lambda i, j: (0, j))
    s_spec = pl.BlockSpec((tm, LANES), lambda i, j: (i, 0))
    return pl.pallas_call(
        _mm_kernel if ssq is None else _mm_scaled_kernel,
        grid=(m // tm, n // tn),
        in_specs=[x_spec, w_spec] if ssq is None else [x_spec, s_spec, w_spec],
        out_specs=pl.BlockSpec((tm, tn), lambda i, j: (i, j)),
        out_shape=jax.ShapeDtypeStruct((m, n), out_dtype),
        compiler_params=_cparams(("parallel", "arbitrary")),
        name="mm_in",
    )(*((x, w) if ssq is None else (x, ssq, w)))


def _mm_out_kernel(oa_ref, ob_ref, oc_ref, w_ref, r_ref, g_ref, o_ref, xg_ref, ssq_ref, cat_ref):
    j = pl.program_id(1)

    @pl.when(j == 0)
    def _():
        cat_ref[:, 0:A_WIDTH] = oa_ref[...]
        cat_ref[:, A_WIDTH:A_WIDTH + B_WIDTH] = ob_ref[...]
        cat_ref[:, A_WIDTH + B_WIDTH:] = oc_ref[...]

    x = r_ref[...] + _dot(cat_ref[...], w_ref[...])
    o_ref[...] = x
    _emit_normed(x, g_ref, xg_ref, ssq_ref, j == 0)


def _matmul_out(oa, ob, oc, w, resid, gain, tm, tn):
    m = oa.shape[0]
    k, n = w.shape
    tile = pl.BlockSpec((tm, tn), lambda i, j: (i, j))
    return pl.pallas_call(
        _mm_out_kernel,
        grid=(m // tm, n // tn),
        in_specs=[pl.BlockSpec((tm, A_WIDTH), lambda i, j: (i, 0)),
                  pl.BlockSpec((tm, B_WIDTH), lambda i, j: (i, 0)),
                  pl.BlockSpec((tm, C_WIDTH), lambda i, j: (i, 0)),
                  pl.BlockSpec((k, tn), lambda i, j: (0, j)),
                  tile,
                  pl.BlockSpec((1, tn), lambda i, j: (0, j))],
        out_specs=[tile, tile, pl.BlockSpec((tm, LANES), lambda i, j: (i, 0))],
        out_shape=[jax.ShapeDtypeStruct((m, n), F32), jax.ShapeDtypeStruct((m, n), BF16),
                   jax.ShapeDtypeStruct((m, LANES), F32)],
        scratch_shapes=[pltpu.VMEM((tm, k), BF16)],
        compiler_params=_cparams(("parallel", "arbitrary")),
        name="mm_out",
    )(oa, ob, oc, w, resid, gain.reshape(1, n))


def _mm_gate_up_kernel(x_ref, s_ref, wg_ref, wu_ref, o_ref):
    x = x_ref[...]
    r = _row_scale(s_ref, x.shape[1])
    gate = _dot(x, wg_ref[...]) * r
    up = _dot(x, wu_ref[...]) * r
    o_ref[...] = (gate * (1.0 / (1.0 + jnp.exp(-gate))) * up).astype(o_ref.dtype)


def _matmul_gate_up(x, ssq, wg, wu, tm, tn):
    m, k = x.shape
    n = wg.shape[1]
    return pl.pallas_call(
        _mm_gate_up_kernel,
        grid=(m // tm, n // tn),
        in_specs=[pl.BlockSpec((tm, k), lambda i, j: (i, 0)),
                  pl.BlockSpec((tm, LANES), lambda i, j: (i, 0)),
                  pl.BlockSpec((k, tn), lambda i, j: (0, j)),
                  pl.BlockSpec((k, tn), lambda i, j: (0, j))],
        out_specs=pl.BlockSpec((tm, tn), lambda i, j: (i, j)),
        out_shape=jax.ShapeDtypeStruct((m, n), BF16),
        compiler_params=_cparams(("parallel", "arbitrary")),
        name="mm_gate_up",
    )(x, ssq, wg, wu)


def _mm_down_kernel(x_ref, w_ref, r_ref, *rest, nk, emit):
    kk = pl.program_id(2)
    o_ref = rest[1] if emit else rest[0]

    @pl.when(kk == 0)
    def _():
        o_ref[...] = r_ref[...] + _dot(x_ref[...], w_ref[...])

    @pl.when(kk != 0)
    def _():
        o_ref[...] += _dot(x_ref[...], w_ref[...])

    if emit:
        g_ref, _, xg_ref, ssq_ref = rest

        @pl.when(kk == nk - 1)
        def _():
            _emit_normed(o_ref[...], g_ref, xg_ref, ssq_ref, pl.program_id(1) == 0)


def _matmul_down(x, w, resid, tm, tn, tk, gain=None):
    m, k = x.shape
    n = w.shape[1]
    emit = gain is not None
    tile = pl.BlockSpec((tm, tn), lambda i, j, kk: (i, j))
    in_specs = [pl.BlockSpec((tm, tk), lambda i, j, kk: (i, kk)),
                pl.BlockSpec((tk, tn), lambda i, j, kk: (kk, j)),
                tile]
    out_specs, out_shape, args = [tile], [jax.ShapeDtypeStruct((m, n), F32)], [x, w, resid]
    if emit:
        in_specs.append(pl.BlockSpec((1, tn), lambda i, j, kk: (0, j)))
        args.append(gain.reshape(1, n))
        out_specs += [tile, pl.BlockSpec((tm, LANES), lambda i, j, kk: (i, 0))]
        out_shape += [jax.ShapeDtypeStruct((m, n), BF16), jax.ShapeDtypeStruct((m, LANES), F32)]
    out = pl.pallas_call(
        functools.partial(_mm_down_kernel, nk=k // tk, emit=emit),
        grid=(m // tm, n // tn, k // tk),
        in_specs=in_specs,
        out_specs=out_specs,
        out_shape=out_shape,
        compiler_params=_cparams(("parallel", "arbitrary", "arbitrary")),
        name="mm_down",
    )(*args)
    return out if emit else out[0]


CAST_TN_GATE = 256
CAST_TK_DOWN = 512


def _masked_bf16(w, first, n_valid, axis):
    idx = first + lax.broadcasted_iota(jnp.int32, w.shape, axis)
    return jnp.where(idx < n_valid, w, 0.0).astype(BF16)


def _mm_in_cast_kernel(x_ref, w_ref, o_ref, wb_ref, *, n_valid, tn):
    idx = pl.program_id(0) * tn + lax.broadcasted_iota(jnp.int32, w_ref.shape, 0)
    wb = jnp.where(idx < n_valid, w_ref[...], 0.0).T.astype(BF16)
    wb_ref[...] = wb
    o_ref[...] = _dot(x_ref[...], wb)


def _mm_in_cast(x, w3, layer, n_pad, tn):
    m, k = x.shape
    n = w3.shape[2]
    return pl.pallas_call(
        functools.partial(_mm_in_cast_kernel, n_valid=n, tn=tn),
        grid=(n_pad // tn,),
        in_specs=[pl.BlockSpec((m, k), lambda j: (0, 0)),
                  pl.BlockSpec((None, tn, k), lambda j: (layer, j, 0))],
        out_specs=[pl.BlockSpec((m, tn), lambda j: (0, j)),
                   pl.BlockSpec((k, tn), lambda j: (0, j))],
        out_shape=[jax.ShapeDtypeStruct((m, n_pad), F32),
                   jax.ShapeDtypeStruct((k, n_pad), BF16)],
        compiler_params=_cparams(("arbitrary",)),
        name="mm_in_cast",
    )(x, jnp.swapaxes(w3, 1, 2))


def _mm_out_cast_kernel(oa_ref, ob_ref, oc_ref, w_ref, r_ref, o_ref, wb_ref, cat_ref):
    @pl.when(pl.program_id(0) == 0)
    def _():
        cat_ref[:, 0:A_WIDTH] = oa_ref[...]
        cat_ref[:, A_WIDTH:A_WIDTH + B_WIDTH] = ob_ref[...]
        cat_ref[:, A_WIDTH + B_WIDTH:] = oc_ref[...]

    wb = w_ref[...].astype(BF16)
    wb_ref[...] = wb
    o_ref[...] = r_ref[...] + _dot(cat_ref[...], wb)


def _mm_out_cast(oa, ob, oc, w3, resid, layer, tn):
    m = oa.shape[0]
    k, n = w3.shape[1:]
    return pl.pallas_call(
        _mm_out_cast_kernel,
        grid=(n // tn,),
        in_specs=[pl.BlockSpec((m, A_WIDTH), lambda j: (0, 0)),
                  pl.BlockSpec((m, B_WIDTH), lambda j: (0, 0)),
                  pl.BlockSpec((m, C_WIDTH), lambda j: (0, 0)),
                  pl.BlockSpec((None, k, tn), lambda j: (layer, 0, j)),
                  pl.BlockSpec((m, tn), lambda j: (0, j))],
        out_specs=[pl.BlockSpec((m, tn), lambda j: (0, j)),
                   pl.BlockSpec((k, tn), lambda j: (0, j))],
        out_shape=[jax.ShapeDtypeStruct((m, n), F32),
                   jax.ShapeDtypeStruct((k, n), BF16)],
        scratch_shapes=[pltpu.VMEM((m, k), BF16)],
        compiler_params=_cparams(("arbitrary",)),
        name="mm_out_cast",
    )(oa, ob, oc, w3, resid)


def _gate_up_cast_kernel(x_ref, wg_ref, wu_ref, o_ref, wgb_ref, wub_ref, *, n_tiles):
    valid = pl.program_id(0) < n_tiles
    wg = jnp.where(valid, wg_ref[...], 0.0).astype(BF16)
    wu = jnp.where(valid, wu_ref[...], 0.0).astype(BF16)
    wgb_ref[...] = wg
    wub_ref[...] = wu
    x = x_ref[...]
    gate = _dot(x, wg)
    o_ref[...] = (gate * (1.0 / (1.0 + jnp.exp(-gate))) * _dot(x, wu)).astype(o_ref.dtype)


def _gate_up_cast(x, wgu3, layer, ff_pad):
    m, k = x.shape
    tn = CAST_TN_GATE
    n_tiles = wgu3.shape[2] // 2 // tn
    assert n_tiles * tn * 2 == wgu3.shape[2]
    return pl.pallas_call(
        functools.partial(_gate_up_cast_kernel, n_tiles=n_tiles),
        grid=(ff_pad // tn,),
        in_specs=[pl.BlockSpec((m, k), lambda j: (0, 0)),
                  pl.BlockSpec((None, k, tn), lambda j: (layer, 0, jnp.minimum(j, n_tiles - 1))),
                  pl.BlockSpec((None, k, tn),
                               lambda j: (layer, 0, n_tiles + jnp.minimum(j, n_tiles - 1)))],
        out_specs=[pl.BlockSpec((m, tn), lambda j: (0, j)),
                   pl.BlockSpec((k, tn), lambda j: (0, j)),
                   pl.BlockSpec((k, tn), lambda j: (0, j))],
        out_shape=[jax.ShapeDtypeStruct((m, ff_pad), BF16),
                   jax.ShapeDtypeStruct((k, ff_pad), BF16),
                   jax.ShapeDtypeStruct((k, ff_pad), BF16)],
        compiler_params=_cparams(("arbitrary",)),
        name="gate_up_cast",
    )(x, wgu3, wgu3)


def _down_cast_kernel(x_ref, w_ref, r_ref, o_ref, wb_ref, *, k_valid, tk):
    kk = pl.program_id(0)
    wb = _masked_bf16(w_ref[...], kk * tk, k_valid, 0)
    wb_ref[...] = wb

    @pl.when(kk == 0)
    def _():
        o_ref[...] = r_ref[...] + _dot(x_ref[...], wb)

    @pl.when(kk != 0)
    def _():
        o_ref[...] += _dot(x_ref[...], wb)


def _down_cast(x, wd3, resid, layer):
    m, ff_pad = x.shape
    ff, d = wd3.shape[1:]
    tk = CAST_TK_DOWN
    last = (ff - 1) // tk
    return pl.pallas_call(
        functools.partial(_down_cast_kernel, k_valid=ff, tk=tk),
        grid=(ff_pad // tk,),
        in_specs=[pl.BlockSpec((m, tk), lambda kk: (0, kk)),
                  pl.BlockSpec((None, tk, d), lambda kk: (layer, jnp.minimum(kk, last), 0)),
                  pl.BlockSpec((m, d), lambda kk: (0, 0))],
        out_specs=[pl.BlockSpec((m, d), lambda kk: (0, 0)),
                   pl.BlockSpec((tk, d), lambda kk: (kk, 0))],
        out_shape=[jax.ShapeDtypeStruct((m, d), F32),
                   jax.ShapeDtypeStruct((ff_pad, d), BF16)],
        compiler_params=_cparams(("arbitrary",)),
        name="down_cast",
    )(x, wd3, resid)


LOG2E = 1.4426950408889634
SUBLANES = 8
HGRN_W = 256
GLA_W = 256
GLR_UNROLL = 8


def _levels(c):
    out, b = [], 1
    while b < c:
        out.append(b)
        b *= 2
    return out


def _prefix_matrix(c):
    t = np.arange(c)[:, None]
    s = np.arange(c)[None, :]
    low = s <= t
    mats = [low, ~low]
    for b in _levels(c):
        same = (t // b) == (s // b)
        odd = ((t // b) % 2) == 1
        mats.append(np.where(odd, low & same, (~low) & same))
    m = np.concatenate(mats, axis=0).astype(np.float32)
    return jnp.asarray(np.concatenate([m, m, m], axis=1), dtype=BF16)


def _diag_matrix(c, nh, w):
    kd = w // nh
    lane = np.arange(w)[:, None]
    col = np.arange(nh * c)[None, :]
    return jnp.asarray(((col // c) == (lane // kd)).astype(np.float32), dtype=BF16)


def _glr_consts(c, nh, w):
    t = lax.broadcasted_iota(jnp.int32, (c, nh * c), 0)
    s = lax.broadcasted_iota(jnp.int32, (c, nh * c), 1) & (c - 1)
    lvl_masks = []
    for b in _levels(c):
        lg = b.bit_length() - 1
        tb = lax.shift_right_logical(t, lg)
        sb = lax.shift_right_logical(s, lg)
        lvl_masks.append((lax.shift_right_logical(tb, 1) == lax.shift_right_logical(sb, 1))
                         & ((tb & 1) == 1) & ((sb & 1) == 0))
    lgk = (w // nh).bit_length() - 1
    lane = lax.shift_right_logical(lax.broadcasted_iota(jnp.int32, (1, w), 1), lgk)
    srow = lax.broadcasted_iota(jnp.int32, (nh * LANES, w), 0) // LANES
    scol = lax.shift_right_logical(lax.broadcasted_iota(jnp.int32, (nh * LANES, w), 1), lgk)
    return dict(lvl_masks=lvl_masks, diag_valid=t == s,
                head_lanes=[lane == h for h in range(nh)], st_mask=srow == scol)


def _head_rows(x, cst, nh):
    if nh * LANES == x.shape[1]:
        z = jnp.zeros((x.shape[0], LANES), x.dtype)
        return jnp.concatenate(
            [jnp.concatenate([x[:, h * LANES:(h + 1) * LANES] if g == h else z
                              for g in range(nh)], axis=1) for h in range(nh)], axis=0)
    zero = jnp.zeros_like(x)
    return jnp.concatenate([jnp.where(cst["head_lanes"][h], x, zero) for h in range(nh)], axis=0)


def _glr_prefix(lf2, pm):
    hi = lf2.astype(BF16)
    r1 = lf2 - hi.astype(F32)
    mid = r1.astype(BF16)
    lo = (r1 - mid.astype(F32)).astype(BF16)
    return _dot(pm, jnp.concatenate([hi, mid, lo], axis=0))


def _glr_intra(qs, ks, lf2s, pm, rd, cst, *, c, nh):
    n = len(qs)
    prefs = [_glr_prefix(lf2, pm) for lf2 in lf2s]
    qhat = [(qs[j] * jnp.exp2(prefs[j][0:c])).astype(BF16) for j in range(n)]
    kdec = [(ks[j] * jnp.exp2(prefs[j][c:2 * c])).astype(BF16) for j in range(n)]
    a = [jnp.where(cst["diag_valid"], _dot((qs[j] * ks[j]).astype(BF16), rd), 0.0)
         for j in range(n)]
    for i in range(len(cst["lvl_masks"])):
        ebs = [jnp.exp2(prefs[j][(2 + i) * c:(3 + i) * c]) for j in range(n)]
        dots = [_dot_nt((qs[j] * ebs[j]).astype(BF16),
                        _head_rows((ks[j] * ebs[j]).astype(BF16), cst, nh)) for j in range(n)]
        a = [jnp.where(cst["lvl_masks"][i], dots[j], a[j]) for j in range(n)]
    return [(a[j].astype(BF16), qhat[j], kdec[j], jnp.exp2(prefs[j][c - 1:c, :]))
            for j in range(n)]


def _glr_state(chunks, st, cst, *, nh):
    o_in, upd = [], []
    for a, _, kdec, _, vs in chunks:
        vb = [v.astype(BF16) for v in vs]
        z = jnp.zeros_like(vb[0])
        vbd = jnp.concatenate(
            [jnp.concatenate([vb[h] if g == h else z for g in range(nh)], axis=1)
             for h in range(nh)], axis=0)
        o_in.append(_dot(a, vbd))
        upd.append(jnp.where(cst["st_mask"], _dot_tn(jnp.concatenate(vb, axis=1), kdec), 0.0))
    outs = []
    for j, (_, qhat, _, dec, _) in enumerate(chunks):
        outs.append(o_in[j] + _dot_nt(qhat, st.astype(BF16)))
        st = st * dec + upd[j]
    return outs, st


def _gated_norm(o, norm, gate):
    var = jnp.mean(o * o, axis=-1, keepdims=True)
    return o * lax.rsqrt(var + EPS) * norm * (gate * (1.0 / (1.0 + jnp.exp(-gate))))


def _glr_scratch(t, c, nh, w):
    return [pltpu.VMEM((t, nh * c), BF16), pltpu.VMEM((t, w), BF16),
            pltpu.VMEM((t, w), BF16), pltpu.VMEM((t // c, SUBLANES, w), F32)]


def _glr_run(load_qkl, load_vg, store_o, st0, pm, rd, cst, scratch, *, c, nchunks, nh, w):
    a_s, qh_s, kd_s, dec_s = scratch
    u = min(GLR_UNROLL, nchunks)
    assert nchunks % u == 0

    def rows_of(n):
        return pl.ds(pl.multiple_of(n * c, c), c)

    def intra(i, carry):
        ns = [i * u + j for j in range(u)]
        ins = [load_qkl(rows_of(n)) for n in ns]
        res = _glr_intra([x[0] for x in ins], [x[1] for x in ins], [x[2] for x in ins],
                         pm, rd, cst, c=c, nh=nh)
        for n, (a, qh, kd, dec) in zip(ns, res):
            rows = rows_of(n)
            a_s[rows, :] = a
            qh_s[rows, :] = qh
            kd_s[rows, :] = kd
            dec_s[n] = jnp.broadcast_to(dec, (SUBLANES, w))
        return carry

    lax.fori_loop(0, nchunks // u, intra, 0)

    def state(i, st):
        ns = [i * u + j for j in range(u)]
        ins = [(a_s[rows_of(n), :], qh_s[rows_of(n), :], kd_s[rows_of(n), :], dec_s[n, 0:1, :])
               + tuple(load_vg(rows_of(n))) for n in ns]
        outs, st = _glr_state([x[:5] for x in ins], st, cst, nh=nh)
        for n, o, x in zip(ns, outs, ins):
            store_o(rows_of(n), o, x[5])
        return st

    return lax.fori_loop(0, nchunks // u, state, st0)


def _hgrn_kernel(*refs, c, nchunks, layer, has_s0):
    if has_s0:
        (pm_ref, rd_ref, lbl_ref, norm_ref, q_ref, z_ref, v_ref, g_ref, s0_ref, o_ref, s_ref,
         a_s, qh_s, kd_s, dec_s) = refs
    else:
        (pm_ref, rd_ref, lbl_ref, norm_ref, q_ref, z_ref, v_ref, g_ref, o_ref, s_ref,
         a_s, qh_s, kd_s, dec_s) = refs
    w = HGRN_W
    nh = w // A_DK
    logits = lbl_ref[...]
    e = jnp.exp(logits - jnp.max(logits, axis=0, keepdims=True))
    probs = e / jnp.sum(e, axis=0, keepdims=True)
    lb = jnp.sum(probs[0:layer + 1], axis=0, keepdims=True) - probs[0:1]
    lb_floor = jnp.maximum(lb, LB_FLOOR)
    oml = 1.0 - lb
    norm = norm_ref[...]
    pm = pm_ref[...]
    rd = rd_ref[...]
    cst = _glr_consts(c, nh, w)
    heads = [slice(h * LANES, (h + 1) * LANES) for h in range(nh)]

    def load_qkl(rows):
        q = q_ref[rows, :]
        z = z_ref[rows, :]
        ez = jnp.exp(-jnp.abs(z))
        r = 1.0 / (1.0 + ez)
        pos = z >= 0.0
        lf2 = jnp.log2(lb_floor + oml * jnp.where(pos, r, ez * r))
        return q, oml * jnp.where(pos, ez * r, r), lf2

    def load_vg(rows):
        v = v_ref[rows, :]
        return [v[:, sl] for sl in heads], g_ref[rows, :]

    def store_o(rows, o, g):
        o_ref[rows, :] = jnp.concatenate(
            [_gated_norm(o[:, sl], norm, g[:, sl]) for sl in heads], axis=1).astype(o_ref.dtype)

    zero = jnp.zeros((A_DV, A_DK), F32)
    if has_s0:
        st0 = jnp.concatenate(
            [jnp.concatenate([s0_ref[0, h].T if g == h else zero for g in range(nh)], axis=1)
             for h in range(nh)], axis=0)
    else:
        st0 = jnp.zeros((nh * A_DV, w), F32)
    st = _glr_run(load_qkl, load_vg, store_o, st0, pm, rd, cst, (a_s, qh_s, kd_s, dec_s),
                  c=c, nchunks=nchunks, nh=nh, w=w)
    for h in range(nh):
        s_ref[0, h] = st[h * A_DV:(h + 1) * A_DV, heads[h]].T


def _hgrn(proj, lbl, norm, s0, layer, nb, t, c):
    w = HGRN_W
    nh = w // A_DK
    pm = _prefix_matrix(c)
    rd = _diag_matrix(c, nh, w)
    nchunks = t // c
    has_s0 = s0 is not None

    def col(off):
        return pl.BlockSpec((t, w), lambda b, h: (b, off // nh + h))

    st_spec = pl.BlockSpec((1, nh, A_DK, A_DV), lambda b, h: (b, h, 0, 0))
    in_specs = [pl.BlockSpec(pm.shape, lambda b, h: (0, 0)),
                pl.BlockSpec(rd.shape, lambda b, h: (0, 0)),
                pl.BlockSpec((DEPTH, w), lambda b, h: (0, h)),
                pl.BlockSpec((1, LANES), lambda b, h: (0, 0)),
                col(COL_A_Q), col(COL_A_F), col(COL_A_I), col(COL_A_G)]
    args = [pm, rd, lbl, norm.reshape(1, LANES), proj, proj, proj, proj]
    if has_s0:
        in_specs.append(st_spec)
        args.append(s0)
    return pl.pallas_call(
        functools.partial(_hgrn_kernel, c=c, nchunks=nchunks, layer=layer, has_s0=has_s0),
        grid=(nb, A_HEADS // nh),
        in_specs=in_specs,
        out_specs=[pl.BlockSpec((t, w), lambda b, h: (b, h)), st_spec],
        out_shape=[jax.ShapeDtypeStruct((nb * t, A_WIDTH), BF16),
                   jax.ShapeDtypeStruct((nb, A_HEADS, A_DK, A_DV), F32)],
        scratch_shapes=_glr_scratch(t, c, nh, w),
        compiler_params=_cparams(("parallel", "parallel")),
        name="hgrn",
    )(*args)


def _gla_kernel(*refs, c, nchunks, has_s0):
    if has_s0:
        (pm_ref, rd_ref, wa_ref, ba_ref, norm_ref, q_ref, k_ref, v_ref, r_ref, ca_ref,
         s0_ref, o_ref, s_ref, a_s, qh_s, kd_s, dec_s) = refs
    else:
        (pm_ref, rd_ref, wa_ref, ba_ref, norm_ref, q_ref, k_ref, v_ref, r_ref, ca_ref,
         o_ref, s_ref, a_s, qh_s, kd_s, dec_s) = refs
    wa = wa_ref[...]
    ba = ba_ref[...]
    norm = norm_ref[...]
    pm = pm_ref[...]
    rd = rd_ref[...]
    w = GLA_W
    nh = w // C_DK
    npl = w // LANES
    cst = _glr_consts(c, nh, w)
    heads = [slice(h * C_DV, (h + 1) * C_DV) for h in range(nh)]
    lane = lax.broadcasted_iota(jnp.int32, (1, LANES), 1)
    half = [lane < C_DK, lane >= C_DK]

    def load_qkl(rows):
        x = _dot(ca_ref[rows, :].astype(BF16), wa) + ba
        lf2 = (jnp.minimum(x, 0.0) - jnp.log1p(jnp.exp(-jnp.abs(x)))) * (LOG2E / GLA_NORMALIZER)
        return q_ref[rows, :] * (C_DK ** -0.5), k_ref[rows, :], lf2

    def load_vg(rows):
        v = v_ref[rows, :]
        return [v[:, sl] for sl in heads], r_ref[rows, :]

    def store_o(rows, o, gate):
        o_ref[rows, :] = jnp.concatenate(
            [_gated_norm(o[:, sl], norm, gate[:, sl]) for sl in heads], axis=1).astype(o_ref.dtype)

    zero = jnp.zeros((C_DV, LANES), F32)
    if has_s0:
        blocks = []
        for h in range(nh):
            p, h2 = divmod(h, 2)
            own = jnp.where(half[h2], s0_ref[0, p].T, 0.0)
            blocks.append(jnp.concatenate([own if g == p else zero for g in range(npl)], axis=1))
        st0 = jnp.concatenate(blocks, axis=0)
    else:
        st0 = jnp.zeros((nh * C_DV, w), F32)
    st = _glr_run(load_qkl, load_vg, store_o, st0, pm, rd, cst, (a_s, qh_s, kd_s, dec_s),
                  c=c, nchunks=nchunks, nh=nh, w=w)
    for p in range(npl):
        cols = slice(p * LANES, (p + 1) * LANES)
        pair = st[2 * p * C_DV:(2 * p + 1) * C_DV, cols] + st[(2 * p + 1) * C_DV:(2 * p + 2) * C_DV, cols]
        s_ref[0, p] = pair.T


def _gla(proj, wa2, ba, norm, s0, nb, t, c):
    w = GLA_W
    nh = w // C_DK
    npl = w // LANES
    pm = _prefix_matrix(c)
    rd = _diag_matrix(c, nh, w)
    nchunks = t // c
    has_s0 = s0 is not None
    npair = C_HEADS // 2

    def col(off, width):
        return pl.BlockSpec((t, width), lambda b, p: (b, off * LANES // width + p))

    st_spec = pl.BlockSpec((1, npl, LANES, C_DV), lambda b, p: (b, p, 0, 0))
    in_specs = [pl.BlockSpec(pm.shape, lambda b, p: (0, 0)),
                pl.BlockSpec(rd.shape, lambda b, p: (0, 0)),
                pl.BlockSpec((LANES, w), lambda b, p: (0, p)),
                pl.BlockSpec((1, w), lambda b, p: (0, p)),
                pl.BlockSpec((1, LANES), lambda b, p: (0, 0)),
                col(COL_C_Q, w), col(COL_C_K, w), col(COL_C_V, 2 * w), col(COL_C_R, 2 * w),
                pl.BlockSpec((t, LANES), lambda b, p: (b, COL_C_A))]
    args = [pm, rd, wa2, ba.reshape(1, -1), norm.reshape(1, LANES), proj, proj, proj, proj, proj]
    if has_s0:
        in_specs.append(st_spec)
        args.append(s0.reshape(nb, npair, 2 * C_DK, C_DV))
    o, s = pl.pallas_call(
        functools.partial(_gla_kernel, c=c, nchunks=nchunks, has_s0=has_s0),
        grid=(nb, npair // npl),
        in_specs=in_specs,
        out_specs=[pl.BlockSpec((t, 2 * w), lambda b, p: (b, p)), st_spec],
        out_shape=[jax.ShapeDtypeStruct((nb * t, C_WIDTH), BF16),
                   jax.ShapeDtypeStruct((nb, npair, 2 * C_DK, C_DV), F32)],
        scratch_shapes=_glr_scratch(t, c, nh, w),
        compiler_params=_cparams(("parallel", "parallel")),
        name="gla",
    )(*args)
    return o, s.reshape(nb, C_HEADS, C_DK, C_DV)


SWA_CB = 8
Q_SCALE = B_SCALE * 1.4426950408889634


def _rope_tables(pos):
    half = B_HD // 2
    inv = ROPE_THETA ** (-jnp.arange(half, dtype=F32) / half)
    ang = pos.astype(F32)[:, None] * inv[None, :]
    cos = jnp.cos(ang)
    sin = jnp.sin(ang)
    cos_t = jnp.tile(cos, (1, 2 * LANES // B_HD))
    sin_t = jnp.tile(jnp.concatenate([-sin, sin], axis=-1), (1, LANES // B_HD))
    return cos_t, sin_t


def _rope(x, cos_t, sin_t):
    reps = x.shape[1] // LANES
    if reps > 1:
        cos_t = jnp.concatenate([cos_t] * reps, axis=1)
        sin_t = jnp.concatenate([sin_t] * reps, axis=1)
    return x * cos_t + _swap_halves(x) * sin_t


def _swap_halves(x):
    n = x.shape[1]
    half = B_HD // 2
    lane = lax.broadcasted_iota(jnp.int32, (1, n), 1)
    first = (lane & (B_HD - 1)) < half
    return jnp.where(first, pltpu.roll(x, n - half, axis=1), pltpu.roll(x, half, axis=1))


def _attend(qs, kb, vb, sink_row, valid):
    return _softmax_pv(_dot_nt(kb, qs), vb, sink_row, valid)


def _softmax_pv(s, vb, sink_row, valid):
    if valid is not None:
        s = jnp.where(valid, s, NEG_BIG)
    sink2 = sink_row * LOG2E
    m = jnp.maximum(jnp.max(s, axis=0, keepdims=True), sink2)
    p = jnp.exp2(s - m)
    den = jnp.sum(p, axis=0, keepdims=True) + jnp.exp2(sink2 - m)
    o_t = _dot_tn(vb, p.astype(BF16)) * (1.0 / den)
    return o_t.T


def _sink_row(sink_ref, base, t):
    lanes = lax.broadcasted_iota(jnp.int32, (1, B_GROUP * t), 1)
    row = jnp.zeros((1, B_GROUP * t), F32)
    for g in range(B_GROUP):
        row = jnp.where((lanes >= g * t) & (lanes < (g + 1) * t), sink_ref[base + g], row)
    return row


def _swa_prompt_kernel(sink_ref, q_ref, k_ref, v_ref, cq_ref, sq_ref, ck_ref, sk_ref,
                       o_ref, kr_ref, vr_ref, kro_ref, vbo_ref, *, t, layer):
    khp = pl.program_id(1)
    step = pl.program_id(2)

    @pl.when(step == 0)
    def _():
        kr = _rope(k_ref[...], ck_ref[...], sk_ref[...])
        kro_ref[...] = kr.astype(BF16)
        vbo_ref[...] = v_ref[...].astype(BF16)
        kr_ref[0] = kr[t - WINDOW:, :]
        vr_ref[0] = v_ref[t - WINDOW:, :]

    band = (WINDOW_CHUNKS + 1) * CHUNK
    sink_rows = [_sink_row(sink_ref, layer * B_HEADS + (khp * 2 + kv) * B_GROUP, CHUNK)
                 for kv in range(2)]
    for ci in range(SWA_CB):
        cidx = step * SWA_CB + ci
        rows = slice(ci * CHUNK, (ci + 1) * CHUNK)
        s0 = pl.multiple_of(jnp.maximum(cidx - WINDOW_CHUNKS, 0) * CHUNK, CHUNK)
        qr = _rope(q_ref[rows, :], cq_ref[rows, :], sq_ref[rows, :]).astype(BF16)
        kband = kro_ref[pl.ds(s0, band), :]
        vband = vbo_ref[pl.ds(s0, band), :]
        key_pos = s0 + lax.broadcasted_iota(jnp.int32, (band, 1), 0)
        valid = key_pos < (cidx + 1) * CHUNK
        for kv in range(2):
            base = kv * B_GROUP * B_HD
            qs = jnp.concatenate(
                [qr[:, base + g * B_HD:base + (g + 1) * B_HD] for g in range(B_GROUP)], axis=0)
            o = _attend(qs, kband[:, kv * B_HD:(kv + 1) * B_HD],
                        vband[:, kv * B_HD:(kv + 1) * B_HD], sink_rows[kv], valid)
            for g in range(0, B_GROUP, 2):
                o_ref[rows, base + g * B_HD:base + (g + 2) * B_HD] = jnp.concatenate(
                    [o[g * CHUNK:(g + 1) * CHUNK], o[(g + 1) * CHUNK:(g + 2) * CHUNK]],
                    axis=1).astype(o_ref.dtype)


def _swa_prompt(proj, sinks, layer, nb, t):
    rb = SWA_CB * CHUNK
    nc = t // rb
    cos_t, sin_t = _rope_tables(jnp.arange(t))
    qw = 2 * B_GROUP * B_HD
    qoff = COL_B_Q * LANES // qw
    kernel = functools.partial(_swa_prompt_kernel, t=t, layer=layer)
    return pl.pallas_call(
        kernel,
        grid=(nb, B_KV_HEADS // 2, nc),
        in_specs=[pl.BlockSpec(memory_space=pltpu.SMEM),
                  pl.BlockSpec((rb, qw), lambda b, p, c: (b * nc + c, qoff + p)),
                  pl.BlockSpec((t, LANES), lambda b, p, c: (b, COL_B_K + p)),
                  pl.BlockSpec((t, LANES), lambda b, p, c: (b, COL_B_V + p)),
                  pl.BlockSpec((rb, LANES), lambda b, p, c: (c, 0)),
                  pl.BlockSpec((rb, LANES), lambda b, p, c: (c, 0)),
                  pl.BlockSpec((t, LANES), lambda b, p, c: (0, 0)),
                  pl.BlockSpec((t, LANES), lambda b, p, c: (0, 0))],
        out_specs=[pl.BlockSpec((rb, qw), lambda b, p, c: (b * nc + c, p)),
                   pl.BlockSpec((1, WINDOW, LANES), lambda b, p, c: (b, 0, p)),
                   pl.BlockSpec((1, WINDOW, LANES), lambda b, p, c: (b, 0, p))],
        out_shape=[jax.ShapeDtypeStruct((nb * t, B_WIDTH), BF16),
                   jax.ShapeDtypeStruct((nb, WINDOW, B_KV_HEADS * B_HD), F32),
                   jax.ShapeDtypeStruct((nb, WINDOW, B_KV_HEADS * B_HD), F32)],
        scratch_shapes=[pltpu.VMEM((t, LANES), BF16), pltpu.VMEM((t, LANES), BF16)],
        compiler_params=_cparams(("parallel", "parallel", "arbitrary")),
        name="swa_prompt",
    )(sinks.reshape(-1), proj, proj, proj, cos_t * Q_SCALE, sin_t * Q_SCALE, cos_t, sin_t)


def _swa_sample_kernel(sink_ref, q_ref, k_ref, v_ref, ckc_ref, cvc_ref, cos_ref, sin_ref,
                       o_ref, kr_ref, vr_ref, *, t, layer):
    cos_t = cos_ref[...]
    sin_t = sin_ref[...]
    kr = _rope(k_ref[...], cos_t, sin_t)
    v = v_ref[...]
    kr_ref[...] = kr
    vr_ref[...] = v
    qr = (_rope(q_ref[...], cos_t, sin_t) * Q_SCALE).astype(BF16)
    ka = jnp.concatenate([ckc_ref[0], kr], axis=0).astype(BF16)
    va = jnp.concatenate([cvc_ref[0], v], axis=0).astype(BF16)
    for kh in range(B_KV_HEADS):
        base = kh * B_GROUP * B_HD
        qs = jnp.concatenate(
            [qr[:, base + g * B_HD:base + (g + 1) * B_HD] for g in range(B_GROUP)], axis=0)
        sink_row = _sink_row(sink_ref, layer * B_HEADS + kh * B_GROUP, t)
        o = _attend(qs, ka[:, kh * B_HD:(kh + 1) * B_HD], va[:, kh * B_HD:(kh + 1) * B_HD],
                    sink_row, None)
        for g in range(0, B_GROUP, 2):
            o_ref[:, base + g * B_HD:base + (g + 2) * B_HD] = jnp.concatenate(
                [o[g * t:(g + 1) * t], o[(g + 1) * t:(g + 2) * t]], axis=1).astype(o_ref.dtype)


def _swa_sample(proj, cache_k, cache_v, sinks, layer, nb, t):
    cos_t, sin_t = _rope_tables(PAST_LEN + jnp.arange(t))
    kvw = B_KV_HEADS * B_HD
    kernel = functools.partial(_swa_sample_kernel, t=t, layer=layer)
    return pl.pallas_call(
        kernel,
        grid=(nb,),
        in_specs=[pl.BlockSpec(memory_space=pltpu.SMEM),
                  pl.BlockSpec((t, B_WIDTH), lambda b: (b, COL_B_Q * LANES // B_WIDTH)),
                  pl.BlockSpec((t, kvw), lambda b: (b, COL_B_K * LANES // kvw)),
                  pl.BlockSpec((t, kvw), lambda b: (b, COL_B_V * LANES // kvw)),
                  pl.BlockSpec((1, WINDOW, kvw), lambda b: (b, 0, 0)),
                  pl.BlockSpec((1, WINDOW, kvw), lambda b: (b, 0, 0)),
                  pl.BlockSpec((t, LANES), lambda b: (0, 0)),
                  pl.BlockSpec((t, LANES), lambda b: (0, 0))],
        out_specs=[pl.BlockSpec((t, B_WIDTH), lambda b: (b, 0)),
                   pl.BlockSpec((t, kvw), lambda b: (b, 0)),
                   pl.BlockSpec((t, kvw), lambda b: (b, 0))],
        out_shape=[jax.ShapeDtypeStruct((nb * t, B_WIDTH), BF16),
                   jax.ShapeDtypeStruct((nb * t, kvw), F32),
                   jax.ShapeDtypeStruct((nb * t, kvw), F32)],
        compiler_params=_cparams(("parallel",)),
        name="swa_sample",
    )(sinks.reshape(-1), proj, proj, proj,
      cache_k.reshape(nb, WINDOW, kvw), cache_v.reshape(nb, WINDOW, kvw), cos_t, sin_t)


def _layer_sample(x, w, layer, nb, t, cache, s_a, s_c, ff_pad):
    m, d = x.shape
    h = _rmsnorm(x, w["norm_mix"][layer], BF16, m)
    proj, wb_in = _mm_in_cast(h, w["w_in"], layer, IN_COLS_PAD, MM_TN_IN)
    o_a, s_a_new = _hgrn(proj, w["lb_logits"], w["hgrn_norm"][layer], s_a, layer, nb, t, t)
    o_b, k_rows, v_rows = _swa_sample(proj, cache[0], cache[1], w["sinks"], layer, nb, t)
    o_c, s_c_new = _gla(proj, w["w_a2"][layer], w["b_a"][layer], w["gla_norm"][layer],
                        s_c, nb, t, t)
    x, wb_out = _mm_out_cast(o_a, o_b, o_c, w["w_out"], x, layer, min(d, 512))
    h = _rmsnorm(x, w["norm_ffn"][layer], BF16, m)
    mid, wb_gate, wb_up = _gate_up_cast(h, w["w_gate_up"], layer, ff_pad)
    x, wb_down = _down_cast(mid, w["w_down"], x, layer)
    outs = (k_rows.reshape(nb, t, B_KV_HEADS, B_HD), v_rows.reshape(nb, t, B_KV_HEADS, B_HD),
            s_a_new, s_c_new)
    return x, outs, dict(w_in=wb_in, w_out=wb_out, w_gate=wb_gate, w_up=wb_up, w_down=wb_down)


def _layer_prompt(x, normed, w, wb, layer, nb, t):
    m, d = x.shape
    tm = min(m, 1024)
    if normed is None:
        h = _rmsnorm(x, w["norm_mix"][layer], BF16, min(m, 256))
        proj = _matmul(h, wb["w_in"], tm, MM_TN_IN, F32)
    else:
        proj = _matmul(normed[0], wb["w_in"], tm, MM_TN_IN, F32, normed[1])
    o_a, s_a_new = _hgrn(proj, w["lb_logits"], w["hgrn_norm"][layer], None, layer, nb, t, CHUNK)
    o_b, k_rows, v_rows = _swa_prompt(proj, w["sinks"], layer, nb, t)
    o_c, s_c_new = _gla(proj, w["w_a2"][layer], w["b_a"][layer], w["gla_norm"][layer],
                        None, nb, t, CHUNK)
    x, xg, ssq = _matmul_out(o_a, o_b, o_c, wb["w_out"], x, w["norm_ffn"][layer], tm, min(d, 512))
    mid = _matmul_gate_up(xg, ssq, wb["w_gate"], wb["w_up"], tm, 512)
    tk = wb["w_down"].shape[0] // 4
    if layer + 1 < DEPTH:
        x, xg, ssq = _matmul_down(mid, wb["w_down"], x, tm, min(d, 1024), tk,
                                  w["norm_mix"][layer + 1])
        normed = (xg, ssq)
    else:
        x = _matmul_down(mid, wb["w_down"], x, tm, min(d, 1024), tk)
        normed = None
    outs = (k_rows.reshape(nb, WINDOW, B_KV_HEADS, B_HD),
            v_rows.reshape(nb, WINDOW, B_KV_HEADS, B_HD), s_a_new, s_c_new)
    return x, normed, outs


def kernel(x_prompt, x_sample, cache_k_swa, cache_v_swa, state_hgrn, state_gla, norm_mix, w_in,
           hgrn_lb_logits, hgrn_norm, swa_sinks, gla_w_alpha2, gla_b_alpha, gla_norm, w_out,
           norm_ffn, w_gate_up, w_down, norm_final):
    n_p, t_p, d = x_prompt.shape
    n_s, t_s, _ = x_sample.shape
    d_ff = w_down.shape[1]
    ff_pad = -(-d_ff // 1024) * 1024
    w = dict(
        norm_mix=norm_mix, norm_ffn=norm_ffn, hgrn_norm=hgrn_norm, gla_norm=gla_norm,
        lb_logits=hgrn_lb_logits, sinks=swa_sinks, b_a=gla_b_alpha,
        w_in=w_in, w_out=w_out, w_gate_up=w_gate_up, w_down=w_down,
        w_a2=jnp.pad(gla_w_alpha2, ((0, 0), (0, LANES - GLA_RANK), (0, 0))).astype(BF16),
    )
    xp = x_prompt.reshape(n_p * t_p, d)
    xs = x_sample.reshape(n_s * t_s, d)
    outs_p, outs_s = [], []
    normed = None
    for layer in range(DEPTH):
        xs, rest, wb = _layer_sample(xs, w, layer, n_s, t_s,
                                     (cache_k_swa[layer], cache_v_swa[layer]),
                                     state_hgrn[layer], state_gla[layer], ff_pad)
        outs_s.append(rest)
        xp, normed, rest = _layer_prompt(xp, normed, w, wb, layer, n_p, t_p)
        outs_p.append(rest)
    y_p = _rmsnorm(xp, norm_final, F32, min(xp.shape[0], 256)).reshape(n_p, t_p, d)
    y_s = _rmsnorm(xs, norm_final, F32, min(xs.shape[0], 256)).reshape(n_s, t_s, d)
    stack = lambda outs, i: jnp.stack([o[i] for o in outs])
    return (y_p, y_s,
            stack(outs_p, 0), stack(outs_p, 1), stack(outs_p, 2), stack(outs_p, 3),
            stack(outs_s, 0), stack(outs_s, 1), stack(outs_s, 2), stack(outs_s, 3))
```

```python
import functools

import jax
import jax.numpy as jnp
import numpy as np
from jax import lax
from jax.experimental import pallas as pl
from jax.experimental.pallas import tpu as pltpu

F32 = jnp.float32
BF16 = jnp.bfloat16

DEPTH = 2
PAST_LEN = 4096
CHUNK = 64
EPS = 1e-6
NEG_BIG = -1e30
LB_FLOOR = 1e-30
A_HEADS = 8
A_DK = 128
A_DV = 128
A_WIDTH = A_HEADS * A_DV
B_HEADS = 32
B_KV_HEADS = 4
B_GROUP = B_HEADS // B_KV_HEADS
B_HD = 64
B_WIDTH = B_HEADS * B_HD
B_SCALE = B_HD ** -0.5
WINDOW = 128
WINDOW_CHUNKS = WINDOW // CHUNK
ROPE_THETA = 10000.0
C_HEADS = 8
C_DK = 64
C_DV = 128
C_WIDTH = C_HEADS * C_DV
GLA_RANK = 16
GLA_NORMALIZER = 16.0

LANES = 128

COL_A_Q = 0
COL_A_F = 8
COL_A_I = 16
COL_A_G = 24
COL_B_Q = 32
COL_B_K = 48
COL_B_V = 50
COL_C_Q = 52
COL_C_K = 56
COL_C_V = 60
COL_C_R = 68
COL_C_A = 76
IN_COLS = 9744
IN_COLS_PAD = 9984
MM_TN_IN = 768
VMEM_LIMIT = 56 * 1024 * 1024


def _cparams(sem):
    return pltpu.CompilerParams(dimension_semantics=sem, vmem_limit_bytes=VMEM_LIMIT)


def _dot(a, b):
    return jnp.dot(a, b, preferred_element_type=F32)


def _dot_nt(a, b):
    return lax.dot_general(a, b, (((1,), (1,)), ((), ())), preferred_element_type=F32)


def _dot_tn(a, b):
    return lax.dot_general(a, b, (((0,), (0,)), ((), ())), preferred_element_type=F32)


def _rmsnorm_kernel(x_ref, g_ref, o_ref):
    x = x_ref[...]
    var = jnp.mean(x * x, axis=-1, keepdims=True)
    o_ref[...] = (x * lax.rsqrt(var + EPS) * g_ref[...]).astype(o_ref.dtype)


def _rmsnorm(x, g, out_dtype, tm):
    m, d = x.shape
    return pl.pallas_call(
        _rmsnorm_kernel,
        grid=(m // tm,),
        in_specs=[pl.BlockSpec((tm, d), lambda i: (i, 0)),
                  pl.BlockSpec((1, d), lambda i: (0, 0))],
        out_specs=pl.BlockSpec((tm, d), lambda i: (i, 0)),
        out_shape=jax.ShapeDtypeStruct((m, d), out_dtype),
        compiler_params=_cparams(("parallel",)),
        name="rmsnorm",
    )(x, g.reshape(1, d))


def _row_scale(ssq_ref, d):
    return lax.rsqrt(jnp.sum(ssq_ref[...], axis=-1, keepdims=True) * (1.0 / d) + EPS)


def _lane_partial_sq(x):
    sq = x * x
    part = sq[:, 0:LANES]
    for c in range(1, x.shape[1] // LANES):
        part = part + sq[:, c * LANES:(c + 1) * LANES]
    return part


def _accumulate(ref, part, first):
    @pl.when(first)
    def _():
        ref[...] = part

    @pl.when(jnp.logical_not(first))
    def _():
        ref[...] += part


def _emit_normed(x, g_ref, xg_ref, ssq_ref, first):
    xg_ref[...] = (x * g_ref[...]).astype(xg_ref.dtype)
    _accumulate(ssq_ref, _lane_partial_sq(x), first)


def _mm_kernel(x_ref, w_ref, o_ref):
    o_ref[...] = _dot(x_ref[...], w_ref[...]).astype(o_ref.dtype)


def _mm_scaled_kernel(x_ref, s_ref, w_ref, o_ref):
    r = _row_scale(s_ref, x_ref.shape[1])
    o_ref[...] = (_dot(x_ref[...], w_ref[...]) * r).astype(o_ref.dtype)


def _matmul(x, w, tm, tn, out_dtype, ssq=None):
    m, k = x.shape
    n = w.shape[1]
    x_spec = pl.BlockSpec((tm, k), lambda i, j: (i, 0))
    w_spec = pl.BlockSpec((k, tn), lambda i, j: (0, j))
    s_spec = pl.BlockSpec((tm, LANES), lambda i, j: (i, 0))
    return pl.pallas_call(
        _mm_kernel if ssq is None else _mm_scaled_kernel,
        grid=(m // tm, n // tn),
        in_specs=[x_spec, w_spec] if ssq is None else [x_spec, s_spec, w_spec],
        out_specs=pl.BlockSpec((tm, tn), lambda i, j: (i, j)),
        out_shape=jax.ShapeDtypeStruct((m, n), out_dtype),
        compiler_params=_cparams(("parallel", "arbitrary")),
        name="mm_in",
    )(*((x, w) if ssq is None else (x, ssq, w)))


def _mm_out_kernel(oa_ref, ob_ref, oc_ref, w_ref, r_ref, g_ref, o_ref, xg_ref, ssq_ref, cat_ref):
    j = pl.program_id(1)

    @pl.when(j == 0)
    def _():
        cat_ref[:, 0:A_WIDTH] = oa_ref[...]
        cat_ref[:, A_WIDTH:A_WIDTH + B_WIDTH] = ob_ref[...]
        cat_ref[:, A_WIDTH + B_WIDTH:] = oc_ref[...]

    x = r_ref[...] + _dot(cat_ref[...], w_ref[...])
    o_ref[...] = x
    _emit_normed(x, g_ref, xg_ref, ssq_ref, j == 0)


def _matmul_out(oa, ob, oc, w, resid, gain, tm, tn):
    m = oa.shape[0]
    k, n = w.shape
    tile = pl.BlockSpec((tm, tn), lambda i, j: (i, j))
    return pl.pallas_call(
        _mm_out_kernel,
        grid=(m // tm, n // tn),
        in_specs=[pl.BlockSpec((tm, A_WIDTH), lambda i, j: (i, 0)),
                  pl.BlockSpec((tm, B_WIDTH), lambda i, j: (i, 0)),
                  pl.BlockSpec((tm, C_WIDTH), lambda i, j: (i, 0)),
                  pl.BlockSpec((k, tn), lambda i, j: (0, j)),
                  tile,
                  pl.BlockSpec((1, tn), lambda i, j: (0, j))],
        out_specs=[tile, tile, pl.BlockSpec((tm, LANES), lambda i, j: (i, 0))],
        out_shape=[jax.ShapeDtypeStruct((m, n), F32), jax.ShapeDtypeStruct((m, n), BF16),
                   jax.ShapeDtypeStruct((m, LANES), F32)],
        scratch_shapes=[pltpu.VMEM((tm, k), BF16)],
        compiler_params=_cparams(("parallel", "arbitrary")),
        name="mm_out",
    )(oa, ob, oc, w, resid, gain.reshape(1, n))


def _mm_gate_up_kernel(x_ref, s_ref, wg_ref, wu_ref, o_ref):
    x = x_ref[...]
    r = _row_scale(s_ref, x.shape[1])
    gate = _dot(x, wg_ref[...]) * r
    up = _dot(x, wu_ref[...]) * r
    o_ref[...] = (gate * (1.0 / (1.0 + jnp.exp(-gate))) * up).astype(o_ref.dtype)


def _matmul_gate_up(x, ssq, wg, wu, tm, tn):
    m, k = x.shape
    n = wg.shape[1]
    return pl.pallas_call(
        _mm_gate_up_kernel,
        grid=(m // tm, n // tn),
        in_specs=[pl.BlockSpec((tm, k), lambda i, j: (i, 0)),
                  pl.BlockSpec((tm, LANES), lambda i, j: (i, 0)),
                  pl.BlockSpec((k, tn), lambda i, j: (0, j)),
                  pl.BlockSpec((k, tn), lambda i, j: (0, j))],
        out_specs=pl.BlockSpec((tm, tn), lambda i, j: (i, j)),
        out_shape=jax.ShapeDtypeStruct((m, n), BF16),
        compiler_params=_cparams(("parallel", "arbitrary")),
        name="mm_gate_up",
    )(x, ssq, wg, wu)


def _mm_down_kernel(x_ref, w_ref, r_ref, *rest, nk, emit):
    kk = pl.program_id(2)
    o_ref = rest[1] if emit else rest[0]

    @pl.when(kk == 0)
    def _():
        o_ref[...] = r_ref[...] + _dot(x_ref[...], w_ref[...])

    @pl.when(kk != 0)
    def _():
        o_ref[...] += _dot(x_ref[...], w_ref[...])

    if emit:
        g_ref, _, xg_ref, ssq_ref = rest

        @pl.when(kk == nk - 1)
        def _():
            _emit_normed(o_ref[...], g_ref, xg_ref, ssq_ref, pl.program_id(1) == 0)


def _matmul_down(x, w, resid, tm, tn, tk, gain=None):
    m, k = x.shape
    n = w.shape[1]
    emit = gain is not None
    tile = pl.BlockSpec((tm, tn), lambda i, j, kk: (i, j))
    in_specs = [pl.BlockSpec((tm, tk), lambda i, j, kk: (i, kk)),
                pl.BlockSpec((tk, tn), lambda i, j, kk: (kk, j)),
                tile]
    out_specs, out_shape, args = [tile], [jax.ShapeDtypeStruct((m, n), F32)], [x, w, resid]
    if emit:
        in_specs.append(pl.BlockSpec((1, tn), lambda i, j, kk: (0, j)))
        args.append(gain.reshape(1, n))
        out_specs += [tile, pl.BlockSpec((tm, LANES), lambda i, j, kk: (i, 0))]
        out_shape += [jax.ShapeDtypeStruct((m, n), BF16), jax.ShapeDtypeStruct((m, LANES), F32)]
    out = pl.pallas_call(
        functools.partial(_mm_down_kernel, nk=k // tk, emit=emit),
        grid=(m // tm, n // tn, k // tk),
        in_specs=in_specs,
        out_specs=out_specs,
        out_shape=out_shape,
        compiler_params=_cparams(("parallel", "arbitrary", "arbitrary")),
        name="mm_down",
    )(*args)
    return out if emit else out[0]


CAST_TN_GATE = 256
CAST_TK_DOWN = 512


def _masked_bf16(w, first, n_valid, axis):
    idx = first + lax.broadcasted_iota(jnp.int32, w.shape, axis)
    return jnp.where(idx < n_valid, w, 0.0).astype(BF16)


def _mm_in_cast_kernel(x_ref, w_ref, o_ref, wb_ref, *, n_valid, tn):
    idx = pl.program_id(0) * tn + lax.broadcasted_iota(jnp.int32, w_ref.shape, 0)
    wb = jnp.where(idx < n_valid, w_ref[...], 0.0).T.astype(BF16)
    wb_ref[...] = wb
    o_ref[...] = _dot(x_ref[...], wb)


def _mm_in_cast(x, w3, layer, n_pad, tn):
    m, k = x.shape
    n = w3.shape[2]
    return pl.pallas_call(
        functools.partial(_mm_in_cast_kernel, n_valid=n, tn=tn),
        grid=(n_pad // tn,),
        in_specs=[pl.BlockSpec((m, k), lambda j: (0, 0)),
                  pl.BlockSpec((None, tn, k), lambda j: (layer, j, 0))],
        out_specs=[pl.BlockSpec((m, tn), lambda j: (0, j)),
                   pl.BlockSpec((k, tn), lambda j: (0, j))],
        out_shape=[jax.ShapeDtypeStruct((m, n_pad), F32),
                   jax.ShapeDtypeStruct((k, n_pad), BF16)],
        compiler_params=_cparams(("arbitrary",)),
        name="mm_in_cast",
    )(x, jnp.swapaxes(w3, 1, 2))


def _mm_out_cast_kernel(oa_ref, ob_ref, oc_ref, w_ref, r_ref, o_ref, wb_ref, cat_ref):
    @pl.when(pl.program_id(0) == 0)
    def _():
        cat_ref[:, 0:A_WIDTH] = oa_ref[...]
        cat_ref[:, A_WIDTH:A_WIDTH + B_WIDTH] = ob_ref[...]
        cat_ref[:, A_WIDTH + B_WIDTH:] = oc_ref[...]

    wb = w_ref[...].astype(BF16)
    wb_ref[...] = wb
    o_ref[...] = r_ref[...] + _dot(cat_ref[...], wb)


def _mm_out_cast(oa, ob, oc, w3, resid, layer, tn):
    m = oa.shape[0]
    k, n = w3.shape[1:]
    return pl.pallas_call(
        _mm_out_cast_kernel,
        grid=(n // tn,),
        in_specs=[pl.BlockSpec((m, A_WIDTH), lambda j: (0, 0)),
                  pl.BlockSpec((m, B_WIDTH), lambda j: (0, 0)),
                  pl.BlockSpec((m, C_WIDTH), lambda j: (0, 0)),
                  pl.BlockSpec((None, k, tn), lambda j: (layer, 0, j)),
                  pl.BlockSpec((m, tn), lambda j: (0, j))],
        out_specs=[pl.BlockSpec((m, tn), lambda j: (0, j)),
                   pl.BlockSpec((k, tn), lambda j: (0, j))],
        out_shape=[jax.ShapeDtypeStruct((m, n), F32),
                   jax.ShapeDtypeStruct((k, n), BF16)],
        scratch_shapes=[pltpu.VMEM((m, k), BF16)],
        compiler_params=_cparams(("arbitrary",)),
        name="mm_out_cast",
    )(oa, ob, oc, w3, resid)


def _gate_up_cast_kernel(x_ref, wg_ref, wu_ref, o_ref, wgb_ref, wub_ref, *, n_tiles):
    valid = pl.program_id(0) < n_tiles
    wg = jnp.where(valid, wg_ref[...], 0.0).astype(BF16)
    wu = jnp.where(valid, wu_ref[...], 0.0).astype(BF16)
    wgb_ref[...] = wg
    wub_ref[...] = wu
    x = x_ref[...]
    gate = _dot(x, wg)
    o_ref[...] = (gate * (1.0 / (1.0 + jnp.exp(-gate))) * _dot(x, wu)).astype(o_ref.dtype)


def _gate_up_cast(x, wgu3, layer, ff_pad):
    m, k = x.shape
    tn = CAST_TN_GATE
    n_tiles = wgu3.shape[2] // 2 // tn
    assert n_tiles * tn * 2 == wgu3.shape[2]
    return pl.pallas_call(
        functools.partial(_gate_up_cast_kernel, n_tiles=n_tiles),
        grid=(ff_pad // tn,),
        in_specs=[pl.BlockSpec((m, k), lambda j: (0, 0)),
                  pl.BlockSpec((None, k, tn), lambda j: (layer, 0, jnp.minimum(j, n_tiles - 1))),
                  pl.BlockSpec((None, k, tn),
                               lambda j: (layer, 0, n_tiles + jnp.minimum(j, n_tiles - 1)))],
        out_specs=[pl.BlockSpec((m, tn), lambda j: (0, j)),
                   pl.BlockSpec((k, tn), lambda j: (0, j)),
                   pl.BlockSpec((k, tn), lambda j: (0, j))],
        out_shape=[jax.ShapeDtypeStruct((m, ff_pad), BF16),
                   jax.ShapeDtypeStruct((k, ff_pad), BF16),
                   jax.ShapeDtypeStruct((k, ff_pad), BF16)],
        compiler_params=_cparams(("arbitrary",)),
        name="gate_up_cast",
    )(x, wgu3, wgu3)


def _down_cast_kernel(x_ref, w_ref, r_ref, o_ref, wb_ref, *, k_valid, tk):
    kk = pl.program_id(0)
    wb = _masked_bf16(w_ref[...], kk * tk, k_valid, 0)
    wb_ref[...] = wb

    @pl.when(kk == 0)
    def _():
        o_ref[...] = r_ref[...] + _dot(x_ref[...], wb)

    @pl.when(kk != 0)
    def _():
        o_ref[...] += _dot(x_ref[...], wb)


def _down_cast(x, wd3, resid, layer):
    m, ff_pad = x.shape
    ff, d = wd3.shape[1:]
    tk = CAST_TK_DOWN
    last = (ff - 1) // tk
    return pl.pallas_call(
        functools.partial(_down_cast_kernel, k_valid=ff, tk=tk),
        grid=(ff_pad // tk,),
        in_specs=[pl.BlockSpec((m, tk), lambda kk: (0, kk)),
                  pl.BlockSpec((None, tk, d), lambda kk: (layer, jnp.minimum(kk, last), 0)),
                  pl.BlockSpec((m, d), lambda kk: (0, 0))],
        out_specs=[pl.BlockSpec((m, d), lambda kk: (0, 0)),
                   pl.BlockSpec((tk, d), lambda kk: (kk, 0))],
        out_shape=[jax.ShapeDtypeStruct((m, d), F32),
                   jax.ShapeDtypeStruct((ff_pad, d), BF16)],
        compiler_params=_cparams(("arbitrary",)),
        name="down_cast",
    )(x, wd3, resid)


LOG2E = 1.4426950408889634
SUBLANES = 8
HGRN_W = 256
GLA_W = 256
GLR_UNROLL = 8
GLR_UNROLL_STATE = 16


def _levels(c):
    out, b = [], 1
    while b < c:
        out.append(b)
        b *= 2
    return out


def _prefix_matrix(c):
    t = np.arange(c)[:, None]
    s = np.arange(c)[None, :]
    low = s <= t
    mats = [low]
    for b in _levels(c):
        if b >= SUBLANES:
            break
        same = (t // b) == (s // b)
        odd = ((t // b) % 2) == 1
        mats.append(np.where(odd, low & same, (~low) & same))
    m = np.concatenate(mats, axis=0).astype(np.float32)
    return jnp.asarray(np.concatenate([m, m, m], axis=1), dtype=BF16)


def _level_sums(g, b):
    parts = []
    for j in range(g.shape[0] // b):
        blk = g[j * b:(j + 1) * b]
        if j % 2 == 1:
            parts.append(blk - g[j * b - 1:j * b])
        else:
            parts.append(g[(j + 1) * b - 1:(j + 1) * b] - blk)
    return jnp.concatenate(parts, axis=0)


def _diag_matrix(c, nh, w):
    kd = w // nh
    lane = np.arange(w)[:, None]
    col = np.arange(nh * c)[None, :]
    return jnp.asarray(((col // c) == (lane // kd)).astype(np.float32), dtype=BF16)


def _glr_consts(c, nh, w):
    t = lax.broadcasted_iota(jnp.int32, (c, nh * c), 0)
    s = lax.broadcasted_iota(jnp.int32, (c, nh * c), 1) & (c - 1)
    lvl_masks = []
    for b in _levels(c):
        lg = b.bit_length() - 1
        tb = lax.shift_right_logical(t, lg)
        sb = lax.shift_right_logical(s, lg)
        lvl_masks.append((lax.shift_right_logical(tb, 1) == lax.shift_right_logical(sb, 1))
                         & ((tb & 1) == 1) & ((sb & 1) == 0))
    lgk = (w // nh).bit_length() - 1
    lane = lax.shift_right_logical(lax.broadcasted_iota(jnp.int32, (1, w), 1), lgk)
    srow = lax.broadcasted_iota(jnp.int32, (nh * LANES, w), 0) // LANES
    scol = lax.shift_right_logical(lax.broadcasted_iota(jnp.int32, (nh * LANES, w), 1), lgk)
    return dict(lvl_masks=lvl_masks, diag_valid=t == s,
                head_lanes=[lane == h for h in range(nh)], st_mask=srow == scol)


def _head_rows(x, cst, nh):
    if nh * LANES == x.shape[1]:
        z = jnp.zeros((x.shape[0], LANES), x.dtype)
        return jnp.concatenate(
            [jnp.concatenate([x[:, h * LANES:(h + 1) * LANES] if g == h else z
                              for g in range(nh)], axis=1) for h in range(nh)], axis=0)
    zero = jnp.zeros_like(x)
    return jnp.concatenate([jnp.where(cst["head_lanes"][h], x, zero) for h in range(nh)], axis=0)


def _glr_prefix(lf2, pm):
    hi = lf2.astype(BF16)
    r1 = lf2 - hi.astype(F32)
    mid = r1.astype(BF16)
    lo = (r1 - mid.astype(F32)).astype(BF16)
    return _dot(pm, jnp.concatenate([hi, mid, lo], axis=0))


def _glr_intra(qs, ks, lf2s, pm, rd, cst, *, c, nh):
    n = len(qs)
    prefs = [_glr_prefix(lf2, pm) for lf2 in lf2s]
    gs = [p[0:c] for p in prefs]
    qhat = [(qs[j] * jnp.exp2(gs[j])).astype(BF16) for j in range(n)]
    kdec = [(ks[j] * jnp.exp2(gs[j][c - 1:c, :] - gs[j])).astype(BF16) for j in range(n)]
    a = [jnp.where(cst["diag_valid"], _dot((qs[j] * ks[j]).astype(BF16), rd), 0.0)
         for j in range(n)]
    for i, b in enumerate(_levels(c)):
        if b < SUBLANES:
            ebs = [jnp.exp2(prefs[j][(1 + i) * c:(2 + i) * c]) for j in range(n)]
        else:
            ebs = [jnp.exp2(_level_sums(gs[j], b)) for j in range(n)]
        dots = [_dot_nt((qs[j] * ebs[j]).astype(BF16),
                        _head_rows((ks[j] * ebs[j]).astype(BF16), cst, nh)) for j in range(n)]
        a = [jnp.where(cst["lvl_masks"][i], dots[j], a[j]) for j in range(n)]
    return [(a[j].astype(BF16), qhat[j], kdec[j], jnp.exp2(gs[j][c - 1:c, :]))
            for j in range(n)]


def _glr_state(chunks, st, cst, *, nh):
    o_in, upd = [], []
    for a, _, kdec, _, vs in chunks:
        vb = [v.astype(BF16) for v in vs]
        z = jnp.zeros_like(vb[0])
        vbd = jnp.concatenate(
            [jnp.concatenate([vb[h] if g == h else z for g in range(nh)], axis=1)
             for h in range(nh)], axis=0)
        o_in.append(_dot(a, vbd))
        upd.append(jnp.where(cst["st_mask"], _dot_tn(jnp.concatenate(vb, axis=1), kdec), 0.0))
    outs = []
    for j, (_, qhat, _, dec, _) in enumerate(chunks):
        outs.append(o_in[j] + _dot_nt(qhat, st.astype(BF16)))
        st = st * dec + upd[j]
    return outs, st


def _gated_norm(o, norm, gate):
    var = jnp.mean(o * o, axis=-1, keepdims=True)
    return o * lax.rsqrt(var + EPS) * norm * (gate * (1.0 / (1.0 + jnp.exp(-gate))))


def _glr_scratch(t, c, nh, w):
    return [pltpu.VMEM((t, nh * c), BF16), pltpu.VMEM((t, w), BF16),
            pltpu.VMEM((t, w), BF16), pltpu.VMEM((t // c, SUBLANES, w), F32)]


def _glr_run(load_qkl, load_vg, store_o, st0, pm, rd, cst, scratch, *, c, nchunks, nh, w):
    a_s, qh_s, kd_s, dec_s = scratch
    u = min(GLR_UNROLL, nchunks)
    assert nchunks % u == 0

    def rows_of(n):
        return pl.ds(pl.multiple_of(n * c, c), c)

    def intra(i, carry):
        ns = [i * u + j for j in range(u)]
        ins = [load_qkl(rows_of(n)) for n in ns]
        res = _glr_intra([x[0] for x in ins], [x[1] for x in ins], [x[2] for x in ins],
                         pm, rd, cst, c=c, nh=nh)
        for n, (a, qh, kd, dec) in zip(ns, res):
            rows = rows_of(n)
            a_s[rows, :] = a
            qh_s[rows, :] = qh
            kd_s[rows, :] = kd
            dec_s[n] = jnp.broadcast_to(dec, (SUBLANES, w))
        return carry

    lax.fori_loop(0, nchunks // u, intra, 0)

    us = min(GLR_UNROLL_STATE, nchunks)
    assert nchunks % us == 0

    def state(i, st):
        ns = [i * us + j for j in range(us)]
        ins = [(a_s[rows_of(n), :], qh_s[rows_of(n), :], kd_s[rows_of(n), :], dec_s[n, 0:1, :])
               + tuple(load_vg(rows_of(n))) for n in ns]
        outs, st = _glr_state([x[:5] for x in ins], st, cst, nh=nh)
        for n, o, x in zip(ns, outs, ins):
            store_o(rows_of(n), o, x[5])
        return st

    return lax.fori_loop(0, nchunks // us, state, st0)


def _hgrn_kernel(*refs, c, nchunks, layer, has_s0):
    if has_s0:
        (pm_ref, rd_ref, lbl_ref, norm_ref, q_ref, z_ref, v_ref, g_ref, s0_ref, o_ref, s_ref,
         a_s, qh_s, kd_s, dec_s) = refs
    else:
        (pm_ref, rd_ref, lbl_ref, norm_ref, q_ref, z_ref, v_ref, g_ref, o_ref, s_ref,
         a_s, qh_s, kd_s, dec_s) = refs
    w = HGRN_W
    nh = w // A_DK
    logits = lbl_ref[...]
    e = jnp.exp(logits - jnp.max(logits, axis=0, keepdims=True))
    probs = e / jnp.sum(e, axis=0, keepdims=True)
    lb = jnp.sum(probs[0:layer + 1], axis=0, keepdims=True) - probs[0:1]
    lb_floor = jnp.maximum(lb, LB_FLOOR)
    oml = 1.0 - lb
    norm = norm_ref[...]
    pm = pm_ref[...]
    rd = rd_ref[...]
    cst = _glr_consts(c, nh, w)
    heads = [slice(h * LANES, (h + 1) * LANES) for h in range(nh)]

    def load_qkl(rows):
        q = q_ref[rows, :]
        z = z_ref[rows, :]
        ez = jnp.exp(-jnp.abs(z))
        r = 1.0 / (1.0 + ez)
        pos = z >= 0.0
        lf2 = jnp.log2(lb_floor + oml * jnp.where(pos, r, ez * r))
        return q, oml * jnp.where(pos, ez * r, r), lf2

    def load_vg(rows):
        v = v_ref[rows, :]
        return [v[:, sl] for sl in heads], g_ref[rows, :]

    def store_o(rows, o, g):
        o_ref[rows, :] = jnp.concatenate(
            [_gated_norm(o[:, sl], norm, g[:, sl]) for sl in heads], axis=1).astype(o_ref.dtype)

    zero = jnp.zeros((A_DV, A_DK), F32)
    if has_s0:
        st0 = jnp.concatenate(
            [jnp.concatenate([s0_ref[0, h].T if g == h else zero for g in range(nh)], axis=1)
             for h in range(nh)], axis=0)
    else:
        st0 = jnp.zeros((nh * A_DV, w), F32)
    st = _glr_run(load_qkl, load_vg, store_o, st0, pm, rd, cst, (a_s, qh_s, kd_s, dec_s),
                  c=c, nchunks=nchunks, nh=nh, w=w)
    for h in range(nh):
        s_ref[0, h] = st[h * A_DV:(h + 1) * A_DV, heads[h]].T


def _hgrn(proj, lbl, norm, s0, layer, nb, t, c):
    w = HGRN_W
    nh = w // A_DK
    pm = _prefix_matrix(c)
    rd = _diag_matrix(c, nh, w)
    nchunks = t // c
    has_s0 = s0 is not None

    def col(off):
        return pl.BlockSpec((t, w), lambda b, h: (b, off // nh + h))

    st_spec = pl.BlockSpec((1, nh, A_DK, A_DV), lambda b, h: (b, h, 0, 0))
    in_specs = [pl.BlockSpec(pm.shape, lambda b, h: (0, 0)),
                pl.BlockSpec(rd.shape, lambda b, h: (0, 0)),
                pl.BlockSpec((DEPTH, w), lambda b, h: (0, h)),
                pl.BlockSpec((1, LANES), lambda b, h: (0, 0)),
                col(COL_A_Q), col(COL_A_F), col(COL_A_I), col(COL_A_G)]
    args = [pm, rd, lbl, norm.reshape(1, LANES), proj, proj, proj, proj]
    if has_s0:
        in_specs.append(st_spec)
        args.append(s0)
    return pl.pallas_call(
        functools.partial(_hgrn_kernel, c=c, nchunks=nchunks, layer=layer, has_s0=has_s0),
        grid=(nb, A_HEADS // nh),
        in_specs=in_specs,
        out_specs=[pl.BlockSpec((t, w), lambda b, h: (b, h)), st_spec],
        out_shape=[jax.ShapeDtypeStruct((nb * t, A_WIDTH), BF16),
                   jax.ShapeDtypeStruct((nb, A_HEADS, A_DK, A_DV), F32)],
        scratch_shapes=_glr_scratch(t, c, nh, w),
        compiler_params=_cparams(("parallel", "parallel")),
        name="hgrn",
    )(*args)


def _gla_kernel(*refs, c, nchunks, has_s0):
    if has_s0:
        (pm_ref, rd_ref, wa_ref, ba_ref, norm_ref, q_ref, k_ref, v_ref, r_ref, ca_ref,
         s0_ref, o_ref, s_ref, a_s, qh_s, kd_s, dec_s) = refs
    else:
        (pm_ref, rd_ref, wa_ref, ba_ref, norm_ref, q_ref, k_ref, v_ref, r_ref, ca_ref,
         o_ref, s_ref, a_s, qh_s, kd_s, dec_s) = refs
    wa = wa_ref[...]
    ba = ba_ref[...]
    norm = norm_ref[...]
    pm = pm_ref[...]
    rd = rd_ref[...]
    w = GLA_W
    nh = w // C_DK
    npl = w // LANES
    cst = _glr_consts(c, nh, w)
    heads = [slice(h * C_DV, (h + 1) * C_DV) for h in range(nh)]
    lane = lax.broadcasted_iota(jnp.int32, (1, LANES), 1)
    half = [lane < C_DK, lane >= C_DK]

    def load_qkl(rows):
        x = _dot(ca_ref[rows, :].astype(BF16), wa) + ba
        lf2 = (jnp.minimum(x, 0.0) - jnp.log1p(jnp.exp(-jnp.abs(x)))) * (LOG2E / GLA_NORMALIZER)
        return q_ref[rows, :] * (C_DK ** -0.5), k_ref[rows, :], lf2

    def load_vg(rows):
        v = v_ref[rows, :]
        return [v[:, sl] for sl in heads], r_ref[rows, :]

    def store_o(rows, o, gate):
        o_ref[rows, :] = jnp.concatenate(
            [_gated_norm(o[:, sl], norm, gate[:, sl]) for sl in heads], axis=1).astype(o_ref.dtype)

    zero = jnp.zeros((C_DV, LANES), F32)
    if has_s0:
        blocks = []
        for h in range(nh):
            p, h2 = divmod(h, 2)
            own = jnp.where(half[h2], s0_ref[0, p].T, 0.0)
            blocks.append(jnp.concatenate([own if g == p else zero for g in range(npl)], axis=1))
        st0 = jnp.concatenate(blocks, axis=0)
    else:
        st0 = jnp.zeros((nh * C_DV, w), F32)
    st = _glr_run(load_qkl, load_vg, store_o, st0, pm, rd, cst, (a_s, qh_s, kd_s, dec_s),
                  c=c, nchunks=nchunks, nh=nh, w=w)
    for p in range(npl):
        cols = slice(p * LANES, (p + 1) * LANES)
        pair = st[2 * p * C_DV:(2 * p + 1) * C_DV, cols] + st[(2 * p + 1) * C_DV:(2 * p + 2) * C_DV, cols]
        s_ref[0, p] = pair.T


def _gla(proj, wa2, ba, norm, s0, nb, t, c):
    w = GLA_W
    nh = w // C_DK
    npl = w // LANES
    pm = _prefix_matrix(c)
    rd = _diag_matrix(c, nh, w)
    nchunks = t // c
    has_s0 = s0 is not None
    npair = C_HEADS // 2

    def col(off, width):
        return pl.BlockSpec((t, width), lambda b, p: (b, off * LANES // width + p))

    st_spec = pl.BlockSpec((1, npl, LANES, C_DV), lambda b, p: (b, p, 0, 0))
    in_specs = [pl.BlockSpec(pm.shape, lambda b, p: (0, 0)),
                pl.BlockSpec(rd.shape, lambda b, p: (0, 0)),
                pl.BlockSpec((LANES, w), lambda b, p: (0, p)),
                pl.BlockSpec((1, w), lambda b, p: (0, p)),
                pl.BlockSpec((1, LANES), lambda b, p: (0, 0)),
                col(COL_C_Q, w), col(COL_C_K, w), col(COL_C_V, 2 * w), col(COL_C_R, 2 * w),
                pl.BlockSpec((t, LANES), lambda b, p: (b, COL_C_A))]
    args = [pm, rd, wa2, ba.reshape(1, -1), norm.reshape(1, LANES), proj, proj, proj, proj, proj]
    if has_s0:
        in_specs.append(st_spec)
        args.append(s0.reshape(nb, npair, 2 * C_DK, C_DV))
    o, s = pl.pallas_call(
        functools.partial(_gla_kernel, c=c, nchunks=nchunks, has_s0=has_s0),
        grid=(nb, npair // npl),
        in_specs=in_specs,
        out_specs=[pl.BlockSpec((t, 2 * w), lambda b, p: (b, p)), st_spec],
        out_shape=[jax.ShapeDtypeStruct((nb * t, C_WIDTH), BF16),
                   jax.ShapeDtypeStruct((nb, npair, 2 * C_DK, C_DV), F32)],
        scratch_shapes=_glr_scratch(t, c, nh, w),
        compiler_params=_cparams(("parallel", "parallel")),
        name="gla",
    )(*args)
    return o, s.reshape(nb, C_HEADS, C_DK, C_DV)


SWA_CB = 8
Q_SCALE = B_SCALE * 1.4426950408889634


def _rope_tables(pos):
    half = B_HD // 2
    inv = ROPE_THETA ** (-jnp.arange(half, dtype=F32) / half)
    ang = pos.astype(F32)[:, None] * inv[None, :]
    cos = jnp.cos(ang)
    sin = jnp.sin(ang)
    cos_t = jnp.tile(cos, (1, 2 * LANES // B_HD))
    sin_t = jnp.tile(jnp.concatenate([-sin, sin], axis=-1), (1, LANES // B_HD))
    return cos_t, sin_t


def _rope(x, cos_t, sin_t):
    reps = x.shape[1] // LANES
    if reps > 1:
        cos_t = jnp.concatenate([cos_t] * reps, axis=1)
        sin_t = jnp.concatenate([sin_t] * reps, axis=1)
    return x * cos_t + _swap_halves(x) * sin_t


def _swap_halves(x):
    n = x.shape[1]
    half = B_HD // 2
    lane = lax.broadcasted_iota(jnp.int32, (1, n), 1)
    first = (lane & (B_HD - 1)) < half
    return jnp.where(first, pltpu.roll(x, n - half, axis=1), pltpu.roll(x, half, axis=1))


def _attend(qs, kb, vb, sink_row, valid):
    return _softmax_pv(_dot_nt(kb, qs), vb, sink_row, valid)


def _softmax_pv(s, vb, sink_row, valid):
    if valid is not None:
        s = jnp.where(valid, s, NEG_BIG)
    sink2 = sink_row * LOG2E
    m = jnp.maximum(jnp.max(s, axis=0, keepdims=True), sink2)
    p = jnp.exp2(s - m)
    den = jnp.sum(p, axis=0, keepdims=True) + jnp.exp2(sink2 - m)
    o_t = _dot_tn(vb, p.astype(BF16)) * (1.0 / den)
    return o_t.T


def _sink_row(sink_ref, base, t):
    lanes = lax.broadcasted_iota(jnp.int32, (1, B_GROUP * t), 1)
    row = jnp.zeros((1, B_GROUP * t), F32)
    for g in range(B_GROUP):
        row = jnp.where((lanes >= g * t) & (lanes < (g + 1) * t), sink_ref[base + g], row)
    return row


def _swa_prompt_kernel(sink_ref, q_ref, k_ref, v_ref, cq_ref, sq_ref, ck_ref, sk_ref,
                       o_ref, kr_ref, vr_ref, kro_ref, vbo_ref, *, t, layer):
    khp = pl.program_id(1)
    step = pl.program_id(2)

    @pl.when(step == 0)
    def _():
        kr = _rope(k_ref[...], ck_ref[...], sk_ref[...])
        kro_ref[...] = kr.astype(BF16)
        vbo_ref[...] = v_ref[...].astype(BF16)
        kr_ref[0] = kr[t - WINDOW:, :]
        vr_ref[0] = v_ref[t - WINDOW:, :]

    band = (WINDOW_CHUNKS + 1) * CHUNK
    sink_rows = [_sink_row(sink_ref, layer * B_HEADS + (khp * 2 + kv) * B_GROUP, CHUNK)
                 for kv in range(2)]
    for ci in range(SWA_CB):
        cidx = step * SWA_CB + ci
        rows = slice(ci * CHUNK, (ci + 1) * CHUNK)
        s0 = pl.multiple_of(jnp.maximum(cidx - WINDOW_CHUNKS, 0) * CHUNK, CHUNK)
        qr = _rope(q_ref[rows, :], cq_ref[rows, :], sq_ref[rows, :]).astype(BF16)
        kband = kro_ref[pl.ds(s0, band), :]
        vband = vbo_ref[pl.ds(s0, band), :]
        key_pos = s0 + lax.broadcasted_iota(jnp.int32, (band, 1), 0)
        valid = key_pos < (cidx + 1) * CHUNK
        for kv in range(2):
            base = kv * B_GROUP * B_HD
            qs = jnp.concatenate(
                [qr[:, base + g * B_HD:base + (g + 1) * B_HD] for g in range(B_GROUP)], axis=0)
            o = _attend(qs, kband[:, kv * B_HD:(kv + 1) * B_HD],
                        vband[:, kv * B_HD:(kv + 1) * B_HD], sink_rows[kv], valid)
            for g in range(0, B_GROUP, 2):
                o_ref[rows, base + g * B_HD:base + (g + 2) * B_HD] = jnp.concatenate(
                    [o[g * CHUNK:(g + 1) * CHUNK], o[(g + 1) * CHUNK:(g + 2) * CHUNK]],
                    axis=1).astype(o_ref.dtype)


def _swa_prompt(proj, sinks, layer, nb, t):
    rb = SWA_CB * CHUNK
    nc = t // rb
    cos_t, sin_t = _rope_tables(jnp.arange(t))
    qw = 2 * B_GROUP * B_HD
    qoff = COL_B_Q * LANES // qw
    kernel = functools.partial(_swa_prompt_kernel, t=t, layer=layer)
    return pl.pallas_call(
        kernel,
        grid=(nb, B_KV_HEADS // 2, nc),
        in_specs=[pl.BlockSpec(memory_space=pltpu.SMEM),
                  pl.BlockSpec((rb, qw), lambda b, p, c: (b * nc + c, qoff + p)),
                  pl.BlockSpec((t, LANES), lambda b, p, c: (b, COL_B_K + p)),
                  pl.BlockSpec((t, LANES), lambda b, p, c: (b, COL_B_V + p)),
                  pl.BlockSpec((rb, LANES), lambda b, p, c: (c, 0)),
                  pl.BlockSpec((rb, LANES), lambda b, p, c: (c, 0)),
                  pl.BlockSpec((t, LANES), lambda b, p, c: (0, 0)),
                  pl.BlockSpec((t, LANES), lambda b, p, c: (0, 0))],
        out_specs=[pl.BlockSpec((rb, qw), lambda b, p, c: (b * nc + c, p)),
                   pl.BlockSpec((1, WINDOW, LANES), lambda b, p, c: (b, 0, p)),
                   pl.BlockSpec((1, WINDOW, LANES), lambda b, p, c: (b, 0, p))],
        out_shape=[jax.ShapeDtypeStruct((nb * t, B_WIDTH), BF16),
                   jax.ShapeDtypeStruct((nb, WINDOW, B_KV_HEADS * B_HD), F32),
                   jax.ShapeDtypeStruct((nb, WINDOW, B_KV_HEADS * B_HD), F32)],
        scratch_shapes=[pltpu.VMEM((t, LANES), BF16), pltpu.VMEM((t, LANES), BF16)],
        compiler_params=_cparams(("parallel", "parallel", "arbitrary")),
        name="swa_prompt",
    )(sinks.reshape(-1), proj, proj, proj, cos_t * Q_SCALE, sin_t * Q_SCALE, cos_t, sin_t)


def _swa_sample_kernel(sink_ref, q_ref, k_ref, v_ref, ckc_ref, cvc_ref, cos_ref, sin_ref,
                       o_ref, kr_ref, vr_ref, *, t, layer):
    cos_t = cos_ref[...]
    sin_t = sin_ref[...]
    kr = _rope(k_ref[...], cos_t, sin_t)
    v = v_ref[...]
    kr_ref[...] = kr
    vr_ref[...] = v
    qr = (_rope(q_ref[...], cos_t, sin_t) * Q_SCALE).astype(BF16)
    ka = jnp.concatenate([ckc_ref[0], kr], axis=0).astype(BF16)
    va = jnp.concatenate([cvc_ref[0], v], axis=0).astype(BF16)
    for kh in range(B_KV_HEADS):
        base = kh * B_GROUP * B_HD
        qs = jnp.concatenate(
            [qr[:, base + g * B_HD:base + (g + 1) * B_HD] for g in range(B_GROUP)], axis=0)
        sink_row = _sink_row(sink_ref, layer * B_HEADS + kh * B_GROUP, t)
        o = _attend(qs, ka[:, kh * B_HD:(kh + 1) * B_HD], va[:, kh * B_HD:(kh + 1) * B_HD],
                    sink_row, None)
        for g in range(0, B_GROUP, 2):
            o_ref[:, base + g * B_HD:base + (g + 2) * B_HD] = jnp.concatenate(
                [o[g * t:(g + 1) * t], o[(g + 1) * t:(g + 2) * t]], axis=1).astype(o_ref.dtype)


def _swa_sample(proj, cache_k, cache_v, sinks, layer, nb, t):
    cos_t, sin_t = _rope_tables(PAST_LEN + jnp.arange(t))
    kvw = B_KV_HEADS * B_HD
    kernel = functools.partial(_swa_sample_kernel, t=t, layer=layer)
    return pl.pallas_call(
        kernel,
        grid=(nb,),
        in_specs=[pl.BlockSpec(memory_space=pltpu.SMEM),
                  pl.BlockSpec((t, B_WIDTH), lambda b: (b, COL_B_Q * LANES // B_WIDTH)),
                  pl.BlockSpec((t, kvw), lambda b: (b, COL_B_K * LANES // kvw)),
                  pl.BlockSpec((t, kvw), lambda b: (b, COL_B_V * LANES // kvw)),
                  pl.BlockSpec((1, WINDOW, kvw), lambda b: (b, 0, 0)),
                  pl.BlockSpec((1, WINDOW, kvw), lambda b: (b, 0, 0)),
                  pl.BlockSpec((t, LANES), lambda b: (0, 0)),
                  pl.BlockSpec((t, LANES), lambda b: (0, 0))],
        out_specs=[pl.BlockSpec((t, B_WIDTH), lambda b: (b, 0)),
                   pl.BlockSpec((t, kvw), lambda b: (b, 0)),
                   pl.BlockSpec((t, kvw), lambda b: (b, 0))],
        out_shape=[jax.ShapeDtypeStruct((nb * t, B_WIDTH), BF16),
                   jax.ShapeDtypeStruct((nb * t, kvw), F32),
                   jax.ShapeDtypeStruct((nb * t, kvw), F32)],
        compiler_params=_cparams(("parallel",)),
        name="swa_sample",
    )(sinks.reshape(-1), proj, proj, proj,
      cache_k.reshape(nb, WINDOW, kvw), cache_v.reshape(nb, WINDOW, kvw), cos_t, sin_t)


def _layer_sample(x, w, layer, nb, t, cache, s_a, s_c, ff_pad):
    m, d = x.shape
    h = _rmsnorm(x, w["norm_mix"][layer], BF16, m)
    proj, wb_in = _mm_in_cast(h, w["w_in"], layer, IN_COLS_PAD, MM_TN_IN)
    o_a, s_a_new = _hgrn(proj, w["lb_logits"], w["hgrn_norm"][layer], s_a, layer, nb, t, t)
    o_b, k_rows, v_rows = _swa_sample(proj, cache[0], cache[1], w["sinks"], layer, nb, t)
    o_c, s_c_new = _gla(proj, w["w_a2"][layer], w["b_a"][layer], w["gla_norm"][layer],
                        s_c, nb, t, t)
    x, wb_out = _mm_out_cast(o_a, o_b, o_c, w["w_out"], x, layer, min(d, 512))
    h = _rmsnorm(x, w["norm_ffn"][layer], BF16, m)
    mid, wb_gate, wb_up = _gate_up_cast(h, w["w_gate_up"], layer, ff_pad)
    x, wb_down = _down_cast(mid, w["w_down"], x, layer)
    outs = (k_rows.reshape(nb, t, B_KV_HEADS, B_HD), v_rows.reshape(nb, t, B_KV_HEADS, B_HD),
            s_a_new, s_c_new)
    return x, outs, dict(w_in=wb_in, w_out=wb_out, w_gate=wb_gate, w_up=wb_up, w_down=wb_down)


def _layer_prompt(x, normed, w, wb, layer, nb, t):
    m, d = x.shape
    tm = min(m, 1024)
    if normed is None:
        h = _rmsnorm(x, w["norm_mix"][layer], BF16, min(m, 256))
        proj = _matmul(h, wb["w_in"], tm, MM_TN_IN, F32)
    else:
        proj = _matmul(normed[0], wb["w_in"], tm, MM_TN_IN, F32, normed[1])
    o_a, s_a_new = _hgrn(proj, w["lb_logits"], w["hgrn_norm"][layer], None, layer, nb, t, CHUNK)
    o_b, k_rows, v_rows = _swa_prompt(proj, w["sinks"], layer, nb, t)
    o_c, s_c_new = _gla(proj, w["w_a2"][layer], w["b_a"][layer], w["gla_norm"][layer],
                        None, nb, t, CHUNK)
    x, xg, ssq = _matmul_out(o_a, o_b, o_c, wb["w_out"], x, w["norm_ffn"][layer], tm, min(d, 512))
    mid = _matmul_gate_up(xg, ssq, wb["w_gate"], wb["w_up"], tm, 512)
    tk = wb["w_down"].shape[0] // 4
    if layer + 1 < DEPTH:
        x, xg, ssq = _matmul_down(mid, wb["w_down"], x, tm, min(d, 1024), tk,
                                  w["norm_mix"][layer + 1])
        normed = (xg, ssq)
    else:
        x = _matmul_down(mid, wb["w_down"], x, tm, min(d, 1024), tk)
        normed = None
    outs = (k_rows.reshape(nb, WINDOW, B_KV_HEADS, B_HD),
            v_rows.reshape(nb, WINDOW, B_KV_HEADS, B_HD), s_a_new, s_c_new)
    return x, normed, outs


def kernel(x_prompt, x_sample, cache_k_swa, cache_v_swa, state_hgrn, state_gla, norm_mix, w_in,
           hgrn_lb_logits, hgrn_norm, swa_sinks, gla_w_alpha2, gla_b_alpha, gla_norm, w_out,
           norm_ffn, w_gate_up, w_down, norm_final):
    n_p, t_p, d = x_prompt.shape
    n_s, t_s, _ = x_sample.shape
    d_ff = w_down.shape[1]
    ff_pad = -(-d_ff // 1024) * 1024
    w = dict(
        norm_mix=norm_mix, norm_ffn=norm_ffn, hgrn_norm=hgrn_norm, gla_norm=gla_norm,
        lb_logits=hgrn_lb_logits, sinks=swa_sinks, b_a=gla_b_alpha,
        w_in=w_in, w_out=w_out, w_gate_up=w_gate_up, w_down=w_down,
        w_a2=jnp.pad(gla_w_alpha2, ((0, 0), (0, LANES - GLA_RANK), (0, 0))).astype(BF16),
    )
    xp = x_prompt.reshape(n_p * t_p, d)
    xs = x_sample.reshape(n_s * t_s, d)
    outs_p, outs_s = [], []
    normed = None
    for layer in range(DEPTH):
        xs, rest, wb = _layer_sample(xs, w, layer, n_s, t_s,
                                     (cache_k_swa[layer], cache_v_swa[layer]),
                                     state_hgrn[layer], state_gla[layer], ff_pad)
        outs_s.append(rest)
        xp, normed, rest = _layer_prompt(xp, normed, w, wb, layer, n_p, t_p)
        outs_p.append(rest)
    y_p = _rmsnorm(xp, norm_final, F32, min(xp.shape[0], 256)).reshape(n_p, t_p, d)
    y_s = _rmsnorm(xs, norm_final, F32, min(xs.shape[0], 256)).reshape(n_s, t_s, d)
    stack = lambda outs, i: jnp.stack([o[i] for o in outs])
    return (y_p, y_s,
            stack(outs_p, 0), stack(outs_p, 1), stack(outs_p, 2), stack(outs_p, 3),
            stack(outs_s, 0), stack(outs_s, 1), stack(outs_s, 2), stack(outs_s, 3))
```

```python
import functools

import jax
import jax.numpy as jnp
import numpy as np
from jax import lax
from jax.experimental import pallas as pl
from jax.experimental.pallas import tpu as pltpu

F32 = jnp.float32
BF16 = jnp.bfloat16

DEPTH = 2
PAST_LEN = 4096
CHUNK = 64
EPS = 1e-6
NEG_BIG = -1e30
LB_FLOOR = 1e-30
A_HEADS = 8
A_DK = 128
A_DV = 128
A_WIDTH = A_HEADS * A_DV
B_HEADS = 32
B_KV_HEADS = 4
B_GROUP = B_HEADS // B_KV_HEADS
B_HD = 64
B_WIDTH = B_HEADS * B_HD
B_SCALE = B_HD ** -0.5
WINDOW = 128
WINDOW_CHUNKS = WINDOW // CHUNK
ROPE_THETA = 10000.0
C_HEADS = 8
C_DK = 64
C_DV = 128
C_WIDTH = C_HEADS * C_DV
GLA_RANK = 16
GLA_NORMALIZER = 16.0

LANES = 128

COL_A_Q = 0
COL_A_F = 8
COL_A_I = 16
COL_A_G = 24
COL_B_Q = 32
COL_B_K = 48
COL_B_V = 50
COL_C_Q = 52
COL_C_K = 56
COL_C_V = 60
COL_C_R = 68
COL_C_A = 76
IN_COLS = 9744
IN_COLS_PAD = 9984
MM_TN_IN = 768
VMEM_LIMIT = 56 * 1024 * 1024


def _cparams(sem):
    return pltpu.CompilerParams(dimension_semantics=sem, vmem_limit_bytes=VMEM_LIMIT)


def _dot(a, b):
    return jnp.dot(a, b, preferred_element_type=F32)


def _dot_nt(a, b):
    return lax.dot_general(a, b, (((1,), (1,)), ((), ())), preferred_element_type=F32)


def _dot_tn(a, b):
    return lax.dot_general(a, b, (((0,), (0,)), ((), ())), preferred_element_type=F32)


def _rmsnorm_kernel(x_ref, g_ref, o_ref):
    x = x_ref[...]
    var = jnp.mean(x * x, axis=-1, keepdims=True)
    o_ref[...] = (x * lax.rsqrt(var + EPS) * g_ref[...]).astype(o_ref.dtype)


def _rmsnorm(x, g, out_dtype, tm):
    m, d = x.shape
    return pl.pallas_call(
        _rmsnorm_kernel,
        grid=(m // tm,),
        in_specs=[pl.BlockSpec((tm, d), lambda i: (i, 0)),
                  pl.BlockSpec((1, d), lambda i: (0, 0))],
        out_specs=pl.BlockSpec((tm, d), lambda i: (i, 0)),
        out_shape=jax.ShapeDtypeStruct((m, d), out_dtype),
        compiler_params=_cparams(("parallel",)),
        name="rmsnorm",
    )(x, g.reshape(1, d))


def _row_scale(ssq_ref, d):
    return lax.rsqrt(jnp.sum(ssq_ref[...], axis=-1, keepdims=True) * (1.0 / d) + EPS)


def _lane_partial_sq(x):
    sq = x * x
    part = sq[:, 0:LANES]
    for c in range(1, x.shape[1] // LANES):
        part = part + sq[:, c * LANES:(c + 1) * LANES]
    return part


def _accumulate(ref, part, first):
    @pl.when(first)
    def _():
        ref[...] = part

    @pl.when(jnp.logical_not(first))
    def _():
        ref[...] += part


def _emit_normed(x, g_ref, xg_ref, ssq_ref, first):
    xg_ref[...] = (x * g_ref[...]).astype(xg_ref.dtype)
    _accumulate(ssq_ref, _lane_partial_sq(x), first)


def _mm_kernel(x_ref, w_ref, o_ref):
    o_ref[...] = _dot(x_ref[...], w_ref[...]).astype(o_ref.dtype)


def _mm_scaled_kernel(x_ref, s_ref, w_ref, o_ref):
    r = _row_scale(s_ref, x_ref.shape[1])
    o_ref[...] = (_dot(x_ref[...], w_ref[...]) * r).astype(o_ref.dtype)


def _matmul(x, w, tm, tn, out_dtype, ssq=None):
    m, k = x.shape
    n = w.shape[1]
    x_spec = pl.BlockSpec((tm, k), lambda i, j: (i, 0))
    w_spec = pl.BlockSpec((k, tn), lambda i, j: (0, j))
    s_spec = pl.BlockSpec((tm, LANES), lambda i, j: (i, 0))
    return pl.pallas_call(
        _mm_kernel if ssq is None else _mm_scaled_kernel,
        grid=(m // tm, n // tn),
        in_specs=[x_spec, w_spec] if ssq is None else [x_spec, s_spec, w_spec],
        out_specs=pl.BlockSpec((tm, tn), lambda i, j: (i, j)),
        out_shape=jax.ShapeDtypeStruct((m, n), out_dtype),
        compiler_params=_cparams(("parallel", "arbitrary")),
        name="mm_in",
    )(*((x, w) if ssq is None else (x, ssq, w)))


def _mm_out_kernel(oa_ref, ob_ref, oc_ref, w_ref, r_ref, g_ref, o_ref, xg_ref, ssq_ref):
    j = pl.program_id(1)
    x = (r_ref[...] + _dot(oa_ref[...], w_ref[0:A_WIDTH, :])
         + _dot(ob_ref[...], w_ref[A_WIDTH:A_WIDTH + B_WIDTH, :])
         + _dot(oc_ref[...], w_ref[A_WIDTH + B_WIDTH:, :]))
    o_ref[...] = x
    _emit_normed(x, g_ref, xg_ref, ssq_ref, j == 0)


def _matmul_out(oa, ob, oc, w, resid, gain, tm, tn):
    m = oa.shape[0]
    k, n = w.shape
    tile = pl.BlockSpec((tm, tn), lambda i, j: (i, j))
    return pl.pallas_call(
        _mm_out_kernel,
        grid=(m // tm, n // tn),
        in_specs=[pl.BlockSpec((tm, A_WIDTH), lambda i, j: (i, 0)),
                  pl.BlockSpec((tm, B_WIDTH), lambda i, j: (i, 0)),
                  pl.BlockSpec((tm, C_WIDTH), lambda i, j: (i, 0)),
                  pl.BlockSpec((k, tn), lambda i, j: (0, j)),
                  tile,
                  pl.BlockSpec((1, tn), lambda i, j: (0, j))],
        out_specs=[tile, tile, pl.BlockSpec((tm, LANES), lambda i, j: (i, 0))],
        out_shape=[jax.ShapeDtypeStruct((m, n), F32), jax.ShapeDtypeStruct((m, n), BF16),
                   jax.ShapeDtypeStruct((m, LANES), F32)],
        compiler_params=_cparams(("parallel", "arbitrary")),
        name="mm_out",
    )(oa, ob, oc, w, resid, gain.reshape(1, n))


def _mm_gate_up_kernel(x_ref, s_ref, wg_ref, wu_ref, o_ref):
    x = x_ref[...]
    r = _row_scale(s_ref, x.shape[1])
    gate = _dot(x, wg_ref[...]) * r
    up = _dot(x, wu_ref[...]) * r
    o_ref[...] = (gate * (1.0 / (1.0 + jnp.exp(-gate))) * up).astype(o_ref.dtype)


def _matmul_gate_up(x, ssq, wg, wu, tm, tn):
    m, k = x.shape
    n = wg.shape[1]
    return pl.pallas_call(
        _mm_gate_up_kernel,
        grid=(m // tm, n // tn),
        in_specs=[pl.BlockSpec((tm, k), lambda i, j: (i, 0)),
                  pl.BlockSpec((tm, LANES), lambda i, j: (i, 0)),
                  pl.BlockSpec((k, tn), lambda i, j: (0, j)),
                  pl.BlockSpec((k, tn), lambda i, j: (0, j))],
        out_specs=pl.BlockSpec((tm, tn), lambda i, j: (i, j)),
        out_shape=jax.ShapeDtypeStruct((m, n), BF16),
        compiler_params=_cparams(("parallel", "arbitrary")),
        name="mm_gate_up",
    )(x, ssq, wg, wu)


def _mm_down_kernel(x_ref, w_ref, r_ref, *rest, nk, emit):
    kk = pl.program_id(2)
    o_ref = rest[1] if emit else rest[0]

    @pl.when(kk == 0)
    def _():
        o_ref[...] = r_ref[...] + _dot(x_ref[...], w_ref[...])

    if not emit:
        @pl.when(kk != 0)
        def _():
            o_ref[...] += _dot(x_ref[...], w_ref[...])
    else:
        assert nk > 1
        g_ref, _, xg_ref, ssq_ref = rest

        @pl.when((kk != 0) & (kk != nk - 1))
        def _():
            o_ref[...] += _dot(x_ref[...], w_ref[...])

        @pl.when(kk == nk - 1)
        def _():
            x = o_ref[...] + _dot(x_ref[...], w_ref[...])
            o_ref[...] = x
            _emit_normed(x, g_ref, xg_ref, ssq_ref, pl.program_id(1) == 0)


def _matmul_down(x, w, resid, tm, tn, tk, gain=None):
    m, k = x.shape
    n = w.shape[1]
    emit = gain is not None
    tile = pl.BlockSpec((tm, tn), lambda i, j, kk: (i, j))
    in_specs = [pl.BlockSpec((tm, tk), lambda i, j, kk: (i, kk)),
                pl.BlockSpec((tk, tn), lambda i, j, kk: (kk, j)),
                tile]
    out_specs, out_shape, args = [tile], [jax.ShapeDtypeStruct((m, n), F32)], [x, w, resid]
    if emit:
        in_specs.append(pl.BlockSpec((1, tn), lambda i, j, kk: (0, j)))
        args.append(gain.reshape(1, n))
        out_specs += [tile, pl.BlockSpec((tm, LANES), lambda i, j, kk: (i, 0))]
        out_shape += [jax.ShapeDtypeStruct((m, n), BF16), jax.ShapeDtypeStruct((m, LANES), F32)]
    out = pl.pallas_call(
        functools.partial(_mm_down_kernel, nk=k // tk, emit=emit),
        grid=(m // tm, n // tn, k // tk),
        in_specs=in_specs,
        out_specs=out_specs,
        out_shape=out_shape,
        compiler_params=_cparams(("parallel", "arbitrary", "arbitrary")),
        name="mm_down",
    )(*args)
    return out if emit else out[0]


CAST_TN_GATE = 256
CAST_TK_DOWN = 512


def _masked_bf16(w, first, n_valid, axis):
    idx = first + lax.broadcasted_iota(jnp.int32, w.shape, axis)
    return jnp.where(idx < n_valid, w, 0.0).astype(BF16)


def _mm_in_cast_kernel(x_ref, w_ref, o_ref, wb_ref, *, n_valid, tn):
    idx = pl.program_id(0) * tn + lax.broadcasted_iota(jnp.int32, w_ref.shape, 0)
    wb = jnp.where(idx < n_valid, w_ref[...], 0.0).T.astype(BF16)
    wb_ref[...] = wb
    o_ref[...] = _dot(x_ref[...], wb)


def _mm_in_cast(x, w3, layer, n_pad, tn):
    m, k = x.shape
    n = w3.shape[2]
    return pl.pallas_call(
        functools.partial(_mm_in_cast_kernel, n_valid=n, tn=tn),
        grid=(n_pad // tn,),
        in_specs=[pl.BlockSpec((m, k), lambda j: (0, 0)),
                  pl.BlockSpec((None, tn, k), lambda j: (layer, j, 0))],
        out_specs=[pl.BlockSpec((m, tn), lambda j: (0, j)),
                   pl.BlockSpec((k, tn), lambda j: (0, j))],
        out_shape=[jax.ShapeDtypeStruct((m, n_pad), F32),
                   jax.ShapeDtypeStruct((k, n_pad), BF16)],
        compiler_params=_cparams(("arbitrary",)),
        name="mm_in_cast",
    )(x, jnp.swapaxes(w3, 1, 2))


def _mm_out_cast_kernel(oa_ref, ob_ref, oc_ref, w_ref, r_ref, o_ref, wb_ref, cat_ref):
    @pl.when(pl.program_id(0) == 0)
    def _():
        cat_ref[:, 0:A_WIDTH] = oa_ref[...]
        cat_ref[:, A_WIDTH:A_WIDTH + B_WIDTH] = ob_ref[...]
        cat_ref[:, A_WIDTH + B_WIDTH:] = oc_ref[...]

    wb = w_ref[...].astype(BF16)
    wb_ref[...] = wb
    o_ref[...] = r_ref[...] + _dot(cat_ref[...], wb)


def _mm_out_cast(oa, ob, oc, w3, resid, layer, tn):
    m = oa.shape[0]
    k, n = w3.shape[1:]
    return pl.pallas_call(
        _mm_out_cast_kernel,
        grid=(n // tn,),
        in_specs=[pl.BlockSpec((m, A_WIDTH), lambda j: (0, 0)),
                  pl.BlockSpec((m, B_WIDTH), lambda j: (0, 0)),
                  pl.BlockSpec((m, C_WIDTH), lambda j: (0, 0)),
                  pl.BlockSpec((None, k, tn), lambda j: (layer, 0, j)),
                  pl.BlockSpec((m, tn), lambda j: (0, j))],
        out_specs=[pl.BlockSpec((m, tn), lambda j: (0, j)),
                   pl.BlockSpec((k, tn), lambda j: (0, j))],
        out_shape=[jax.ShapeDtypeStruct((m, n), F32),
                   jax.ShapeDtypeStruct((k, n), BF16)],
        scratch_shapes=[pltpu.VMEM((m, k), BF16)],
        compiler_params=_cparams(("arbitrary",)),
        name="mm_out_cast",
    )(oa, ob, oc, w3, resid)


def _gate_up_cast_kernel(x_ref, wg_ref, wu_ref, o_ref, wgb_ref, wub_ref, *, n_tiles):
    valid = pl.program_id(0) < n_tiles
    wg = jnp.where(valid, wg_ref[...], 0.0).astype(BF16)
    wu = jnp.where(valid, wu_ref[...], 0.0).astype(BF16)
    wgb_ref[...] = wg
    wub_ref[...] = wu
    x = x_ref[...]
    gate = _dot(x, wg)
    o_ref[...] = (gate * (1.0 / (1.0 + jnp.exp(-gate))) * _dot(x, wu)).astype(o_ref.dtype)


def _gate_up_cast(x, wgu3, layer, ff_pad):
    m, k = x.shape
    tn = CAST_TN_GATE
    n_tiles = wgu3.shape[2] // 2 // tn
    assert n_tiles * tn * 2 == wgu3.shape[2]
    return pl.pallas_call(
        functools.partial(_gate_up_cast_kernel, n_tiles=n_tiles),
        grid=(ff_pad // tn,),
        in_specs=[pl.BlockSpec((m, k), lambda j: (0, 0)),
                  pl.BlockSpec((None, k, tn), lambda j: (layer, 0, jnp.minimum(j, n_tiles - 1))),
                  pl.BlockSpec((None, k, tn),
                               lambda j: (layer, 0, n_tiles + jnp.minimum(j, n_tiles - 1)))],
        out_specs=[pl.BlockSpec((m, tn), lambda j: (0, j)),
                   pl.BlockSpec((k, tn), lambda j: (0, j)),
                   pl.BlockSpec((k, tn), lambda j: (0, j))],
        out_shape=[jax.ShapeDtypeStruct((m, ff_pad), BF16),
                   jax.ShapeDtypeStruct((k, ff_pad), BF16),
                   jax.ShapeDtypeStruct((k, ff_pad), BF16)],
        compiler_params=_cparams(("arbitrary",)),
        name="gate_up_cast",
    )(x, wgu3, wgu3)


def _down_cast_kernel(x_ref, w_ref, r_ref, o_ref, wb_ref, *, k_valid, tk):
    kk = pl.program_id(0)
    wb = _masked_bf16(w_ref[...], kk * tk, k_valid, 0)
    wb_ref[...] = wb

    @pl.when(kk == 0)
    def _():
        o_ref[...] = r_ref[...] + _dot(x_ref[...], wb)

    @pl.when(kk != 0)
    def _():
        o_ref[...] += _dot(x_ref[...], wb)


def _down_cast(x, wd3, resid, layer):
    m, ff_pad = x.shape
    ff, d = wd3.shape[1:]
    tk = CAST_TK_DOWN
    last = (ff - 1) // tk
    return pl.pallas_call(
        functools.partial(_down_cast_kernel, k_valid=ff, tk=tk),
        grid=(ff_pad // tk,),
        in_specs=[pl.BlockSpec((m, tk), lambda kk: (0, kk)),
                  pl.BlockSpec((None, tk, d), lambda kk: (layer, jnp.minimum(kk, last), 0)),
                  pl.BlockSpec((m, d), lambda kk: (0, 0))],
        out_specs=[pl.BlockSpec((m, d), lambda kk: (0, 0)),
                   pl.BlockSpec((tk, d), lambda kk: (kk, 0))],
        out_shape=[jax.ShapeDtypeStruct((m, d), F32),
                   jax.ShapeDtypeStruct((ff_pad, d), BF16)],
        compiler_params=_cparams(("arbitrary",)),
        name="down_cast",
    )(x, wd3, resid)


LOG2E = 1.4426950408889634
SUBLANES = 8
HGRN_W = 256
GLA_W = 256
GLR_UNROLL = 8
GLR_UNROLL_STATE = 16


def _levels(c):
    out, b = [], 1
    while b < c:
        out.append(b)
        b *= 2
    return out


def _prefix_matrix(c):
    t = np.arange(c)[:, None]
    s = np.arange(c)[None, :]
    low = s <= t
    mats = [low]
    for b in _levels(c):
        if b >= SUBLANES:
            break
        same = (t // b) == (s // b)
        odd = ((t // b) % 2) == 1
        mats.append(np.where(odd, low & same, (~low) & same))
    m = np.concatenate(mats, axis=0).astype(np.float32)
    return jnp.asarray(np.concatenate([m, m, m], axis=1), dtype=BF16)


def _level_sums(g, b):
    parts = []
    for j in range(g.shape[0] // b):
        blk = g[j * b:(j + 1) * b]
        if j % 2 == 1:
            parts.append(blk - g[j * b - 1:j * b])
        else:
            parts.append(g[(j + 1) * b - 1:(j + 1) * b] - blk)
    return jnp.concatenate(parts, axis=0)


def _diag_matrix(c, nh, w):
    kd = w // nh
    lane = np.arange(w)[:, None]
    col = np.arange(nh * c)[None, :]
    return jnp.asarray(((col // c) == (lane // kd)).astype(np.float32), dtype=BF16)


def _glr_consts(c, nh, w):
    t = lax.broadcasted_iota(jnp.int32, (c, nh * c), 0)
    s = lax.broadcasted_iota(jnp.int32, (c, nh * c), 1) & (c - 1)
    lvl_masks = []
    for b in _levels(c):
        lg = b.bit_length() - 1
        tb = lax.shift_right_logical(t, lg)
        sb = lax.shift_right_logical(s, lg)
        lvl_masks.append((lax.shift_right_logical(tb, 1) == lax.shift_right_logical(sb, 1))
                         & ((tb & 1) == 1) & ((sb & 1) == 0))
    lgk = (w // nh).bit_length() - 1
    lane = lax.shift_right_logical(lax.broadcasted_iota(jnp.int32, (1, w), 1), lgk)
    srow = lax.broadcasted_iota(jnp.int32, (nh * LANES, w), 0) // LANES
    scol = lax.shift_right_logical(lax.broadcasted_iota(jnp.int32, (nh * LANES, w), 1), lgk)
    return dict(lvl_masks=lvl_masks, diag_valid=t == s,
                head_lanes=[lane == h for h in range(nh)], st_mask=srow == scol)


def _head_rows(x, cst, nh):
    if nh * LANES == x.shape[1]:
        z = jnp.zeros((x.shape[0], LANES), x.dtype)
        return jnp.concatenate(
            [jnp.concatenate([x[:, h * LANES:(h + 1) * LANES] if g == h else z
                              for g in range(nh)], axis=1) for h in range(nh)], axis=0)
    zero = jnp.zeros_like(x)
    return jnp.concatenate([jnp.where(cst["head_lanes"][h], x, zero) for h in range(nh)], axis=0)


def _glr_prefix(lf2, pm):
    hi = lf2.astype(BF16)
    r1 = lf2 - hi.astype(F32)
    mid = r1.astype(BF16)
    lo = (r1 - mid.astype(F32)).astype(BF16)
    return _dot(pm, jnp.concatenate([hi, mid, lo], axis=0))


def _glr_intra(qs, ks, lf2s, pm, rd, cst, *, c, nh):
    n = len(qs)
    prefs = [_glr_prefix(lf2, pm) for lf2 in lf2s]
    gs = [p[0:c] for p in prefs]
    qhat = [(qs[j] * jnp.exp2(gs[j])).astype(BF16) for j in range(n)]
    kdec = [(ks[j] * jnp.exp2(gs[j][c - 1:c, :] - gs[j])).astype(BF16) for j in range(n)]
    a = [jnp.where(cst["diag_valid"], _dot((qs[j] * ks[j]).astype(BF16), rd), 0.0)
         for j in range(n)]
    for i, b in enumerate(_levels(c)):
        if b < SUBLANES:
            ebs = [jnp.exp2(prefs[j][(1 + i) * c:(2 + i) * c]) for j in range(n)]
        else:
            ebs = [jnp.exp2(_level_sums(gs[j], b)) for j in range(n)]
        dots = [_dot_nt((qs[j] * ebs[j]).astype(BF16),
                        _head_rows((ks[j] * ebs[j]).astype(BF16), cst, nh)) for j in range(n)]
        a = [jnp.where(cst["lvl_masks"][i], dots[j], a[j]) for j in range(n)]
    return [(a[j].astype(BF16), qhat[j], kdec[j], jnp.exp2(gs[j][c - 1:c, :]))
            for j in range(n)]


def _glr_state(chunks, st, cst, *, nh):
    o_in, upd = [], []
    for a, _, kdec, _, vs in chunks:
        vb = [v.astype(BF16) for v in vs]
        z = jnp.zeros_like(vb[0])
        vbd = jnp.concatenate(
            [jnp.concatenate([vb[h] if g == h else z for g in range(nh)], axis=1)
             for h in range(nh)], axis=0)
        o_in.append(_dot(a, vbd))
        upd.append(jnp.where(cst["st_mask"], _dot_tn(jnp.concatenate(vb, axis=1), kdec), 0.0))
    outs = []
    for j, (_, qhat, _, dec, _) in enumerate(chunks):
        outs.append(o_in[j] + _dot_nt(qhat, st.astype(BF16)))
        st = st * dec + upd[j]
    return outs, st


def _gated_norm(o, norm, gate):
    var = jnp.mean(o * o, axis=-1, keepdims=True)
    return o * lax.rsqrt(var + EPS) * norm * (gate * (1.0 / (1.0 + jnp.exp(-gate))))


def _glr_scratch(t, c, nh, w):
    return [pltpu.VMEM((t, nh * c), BF16), pltpu.VMEM((t, w), BF16),
            pltpu.VMEM((t, w), BF16), pltpu.VMEM((t // c, SUBLANES, w), F32)]


def _glr_run(load_qkl, load_vg, store_o, st0, pm, rd, cst, scratch, *, c, nchunks, nh, w):
    a_s, qh_s, kd_s, dec_s = scratch
    u = min(GLR_UNROLL, nchunks)
    assert nchunks % u == 0

    def rows_of(n):
        return pl.ds(pl.multiple_of(n * c, c), c)

    def intra(i, carry):
        ns = [i * u + j for j in range(u)]
        ins = [load_qkl(rows_of(n)) for n in ns]
        res = _glr_intra([x[0] for x in ins], [x[1] for x in ins], [x[2] for x in ins],
                         pm, rd, cst, c=c, nh=nh)
        for n, (a, qh, kd, dec) in zip(ns, res):
            rows = rows_of(n)
            a_s[rows, :] = a
            qh_s[rows, :] = qh
            kd_s[rows, :] = kd
            dec_s[n] = jnp.broadcast_to(dec, (SUBLANES, w))
        return carry

    lax.fori_loop(0, nchunks // u, intra, 0)

    us = min(GLR_UNROLL_STATE, nchunks)
    assert nchunks % us == 0

    def state(i, st):
        ns = [i * us + j for j in range(us)]
        ins = [(a_s[rows_of(n), :], qh_s[rows_of(n), :], kd_s[rows_of(n), :], dec_s[n, 0:1, :])
               + tuple(load_vg(rows_of(n))) for n in ns]
        outs, st = _glr_state([x[:5] for x in ins], st, cst, nh=nh)
        for n, o, x in zip(ns, outs, ins):
            store_o(rows_of(n), o, x[5])
        return st

    return lax.fori_loop(0, nchunks // us, state, st0)


def _hgrn_kernel(*refs, c, nchunks, layer, has_s0):
    if has_s0:
        (pm_ref, rd_ref, lbl_ref, norm_ref, q_ref, z_ref, v_ref, g_ref, s0_ref, o_ref, s_ref,
         a_s, qh_s, kd_s, dec_s) = refs
    else:
        (pm_ref, rd_ref, lbl_ref, norm_ref, q_ref, z_ref, v_ref, g_ref, o_ref, s_ref,
         a_s, qh_s, kd_s, dec_s) = refs
    w = HGRN_W
    nh = w // A_DK
    logits = lbl_ref[...]
    e = jnp.exp(logits - jnp.max(logits, axis=0, keepdims=True))
    probs = e / jnp.sum(e, axis=0, keepdims=True)
    lb = jnp.sum(probs[0:layer + 1], axis=0, keepdims=True) - probs[0:1]
    lb_floor = jnp.maximum(lb, LB_FLOOR)
    oml = 1.0 - lb
    norm = norm_ref[...]
    pm = pm_ref[...]
    rd = rd_ref[...]
    cst = _glr_consts(c, nh, w)
    heads = [slice(h * LANES, (h + 1) * LANES) for h in range(nh)]

    def load_qkl(rows):
        q = q_ref[rows, :]
        z = z_ref[rows, :]
        ez = jnp.exp(-jnp.abs(z))
        r = 1.0 / (1.0 + ez)
        pos = z >= 0.0
        lf2 = jnp.log2(lb_floor + oml * jnp.where(pos, r, ez * r))
        return q, oml * jnp.where(pos, ez * r, r), lf2

    def load_vg(rows):
        v = v_ref[rows, :]
        return [v[:, sl] for sl in heads], g_ref[rows, :]

    def store_o(rows, o, g):
        o_ref[rows, :] = jnp.concatenate(
            [_gated_norm(o[:, sl], norm, g[:, sl]) for sl in heads], axis=1).astype(o_ref.dtype)

    zero = jnp.zeros((A_DV, A_DK), F32)
    if has_s0:
        st0 = jnp.concatenate(
            [jnp.concatenate([s0_ref[0, h].T if g == h else zero for g in range(nh)], axis=1)
             for h in range(nh)], axis=0)
    else:
        st0 = jnp.zeros((nh * A_DV, w), F32)
    st = _glr_run(load_qkl, load_vg, store_o, st0, pm, rd, cst, (a_s, qh_s, kd_s, dec_s),
                  c=c, nchunks=nchunks, nh=nh, w=w)
    for h in range(nh):
        s_ref[0, h] = st[h * A_DV:(h + 1) * A_DV, heads[h]].T


def _hgrn(proj, lbl, norm, s0, layer, nb, t, c):
    w = HGRN_W
    nh = w // A_DK
    pm = _prefix_matrix(c)
    rd = _diag_matrix(c, nh, w)
    nchunks = t // c
    has_s0 = s0 is not None

    def col(off):
        return pl.BlockSpec((t, w), lambda b, h: (b, off // nh + h))

    st_spec = pl.BlockSpec((1, nh, A_DK, A_DV), lambda b, h: (b, h, 0, 0))
    in_specs = [pl.BlockSpec(pm.shape, lambda b, h: (0, 0)),
                pl.BlockSpec(rd.shape, lambda b, h: (0, 0)),
                pl.BlockSpec((DEPTH, w), lambda b, h: (0, h)),
                pl.BlockSpec((1, LANES), lambda b, h: (0, 0)),
                col(COL_A_Q), col(COL_A_F), col(COL_A_I), col(COL_A_G)]
    args = [pm, rd, lbl, norm.reshape(1, LANES), proj, proj, proj, proj]
    if has_s0:
        in_specs.append(st_spec)
        args.append(s0)
    return pl.pallas_call(
        functools.partial(_hgrn_kernel, c=c, nchunks=nchunks, layer=layer, has_s0=has_s0),
        grid=(nb, A_HEADS // nh),
        in_specs=in_specs,
        out_specs=[pl.BlockSpec((t, w), lambda b, h: (b, h)), st_spec],
        out_shape=[jax.ShapeDtypeStruct((nb * t, A_WIDTH), BF16),
                   jax.ShapeDtypeStruct((nb, A_HEADS, A_DK, A_DV), F32)],
        scratch_shapes=_glr_scratch(t, c, nh, w),
        compiler_params=_cparams(("parallel", "parallel")),
        name="hgrn",
    )(*args)


def _gla_kernel(*refs, c, nchunks, has_s0):
    if has_s0:
        (pm_ref, rd_ref, wa_ref, ba_ref, norm_ref, q_ref, k_ref, v_ref, r_ref, ca_ref,
         s0_ref, o_ref, s_ref, a_s, qh_s, kd_s, dec_s) = refs
    else:
        (pm_ref, rd_ref, wa_ref, ba_ref, norm_ref, q_ref, k_ref, v_ref, r_ref, ca_ref,
         o_ref, s_ref, a_s, qh_s, kd_s, dec_s) = refs
    wa = wa_ref[...]
    ba = ba_ref[...]
    norm = norm_ref[...]
    pm = pm_ref[...]
    rd = rd_ref[...]
    w = GLA_W
    nh = w // C_DK
    npl = w // LANES
    cst = _glr_consts(c, nh, w)
    heads = [slice(h * C_DV, (h + 1) * C_DV) for h in range(nh)]
    lane = lax.broadcasted_iota(jnp.int32, (1, LANES), 1)
    half = [lane < C_DK, lane >= C_DK]

    def load_qkl(rows):
        x = _dot(ca_ref[rows, :].astype(BF16), wa) + ba
        lf2 = (jnp.minimum(x, 0.0) - jnp.log1p(jnp.exp(-jnp.abs(x)))) * (LOG2E / GLA_NORMALIZER)
        return q_ref[rows, :] * (C_DK ** -0.5), k_ref[rows, :], lf2

    def load_vg(rows):
        v = v_ref[rows, :]
        return [v[:, sl] for sl in heads], r_ref[rows, :]

    def store_o(rows, o, gate):
        o_ref[rows, :] = jnp.concatenate(
            [_gated_norm(o[:, sl], norm, gate[:, sl]) for sl in heads], axis=1).astype(o_ref.dtype)

    zero = jnp.zeros((C_DV, LANES), F32)
    if has_s0:
        blocks = []
        for h in range(nh):
            p, h2 = divmod(h, 2)
            own = jnp.where(half[h2], s0_ref[0, p].T, 0.0)
            blocks.append(jnp.concatenate([own if g == p else zero for g in range(npl)], axis=1))
        st0 = jnp.concatenate(blocks, axis=0)
    else:
        st0 = jnp.zeros((nh * C_DV, w), F32)
    st = _glr_run(load_qkl, load_vg, store_o, st0, pm, rd, cst, (a_s, qh_s, kd_s, dec_s),
                  c=c, nchunks=nchunks, nh=nh, w=w)
    for p in range(npl):
        cols = slice(p * LANES, (p + 1) * LANES)
        pair = st[2 * p * C_DV:(2 * p + 1) * C_DV, cols] + st[(2 * p + 1) * C_DV:(2 * p + 2) * C_DV, cols]
        s_ref[0, p] = pair.T


def _gla(proj, wa2, ba, norm, s0, nb, t, c):
    w = GLA_W
    nh = w // C_DK
    npl = w // LANES
    pm = _prefix_matrix(c)
    rd = _diag_matrix(c, nh, w)
    nchunks = t // c
    has_s0 = s0 is not None
    npair = C_HEADS // 2

    def col(off, width):
        return pl.BlockSpec((t, width), lambda b, p: (b, off * LANES // width + p))

    st_spec = pl.BlockSpec((1, npl, LANES, C_DV), lambda b, p: (b, p, 0, 0))
    in_specs = [pl.BlockSpec(pm.shape, lambda b, p: (0, 0)),
                pl.BlockSpec(rd.shape, lambda b, p: (0, 0)),
                pl.BlockSpec((LANES, w), lambda b, p: (0, p)),
                pl.BlockSpec((1, w), lambda b, p: (0, p)),
                pl.BlockSpec((1, LANES), lambda b, p: (0, 0)),
                col(COL_C_Q, w), col(COL_C_K, w), col(COL_C_V, 2 * w), col(COL_C_R, 2 * w),
                pl.BlockSpec((t, LANES), lambda b, p: (b, COL_C_A))]
    args = [pm, rd, wa2, ba.reshape(1, -1), norm.reshape(1, LANES), proj, proj, proj, proj, proj]
    if has_s0:
        in_specs.append(st_spec)
        args.append(s0.reshape(nb, npair, 2 * C_DK, C_DV))
    o, s = pl.pallas_call(
        functools.partial(_gla_kernel, c=c, nchunks=nchunks, has_s0=has_s0),
        grid=(nb, npair // npl),
        in_specs=in_specs,
        out_specs=[pl.BlockSpec((t, 2 * w), lambda b, p: (b, p)), st_spec],
        out_shape=[jax.ShapeDtypeStruct((nb * t, C_WIDTH), BF16),
                   jax.ShapeDtypeStruct((nb, npair, 2 * C_DK, C_DV), F32)],
        scratch_shapes=_glr_scratch(t, c, nh, w),
        compiler_params=_cparams(("parallel", "parallel")),
        name="gla",
    )(*args)
    return o, s.reshape(nb, C_HEADS, C_DK, C_DV)


SWA_CB = 8
Q_SCALE = B_SCALE * 1.4426950408889634


def _rope_tables(pos):
    half = B_HD // 2
    inv = ROPE_THETA ** (-jnp.arange(half, dtype=F32) / half)
    ang = pos.astype(F32)[:, None] * inv[None, :]
    cos = jnp.cos(ang)
    sin = jnp.sin(ang)
    cos_t = jnp.tile(cos, (1, 2 * LANES // B_HD))
    sin_t = jnp.tile(jnp.concatenate([-sin, sin], axis=-1), (1, LANES // B_HD))
    return cos_t, sin_t


def _rope(x, cos_t, sin_t):
    reps = x.shape[1] // LANES
    if reps > 1:
        cos_t = jnp.concatenate([cos_t] * reps, axis=1)
        sin_t = jnp.concatenate([sin_t] * reps, axis=1)
    return x * cos_t + _swap_halves(x) * sin_t


def _swap_halves(x):
    n = x.shape[1]
    half = B_HD // 2
    lane = lax.broadcasted_iota(jnp.int32, (1, n), 1)
    first = (lane & (B_HD - 1)) < half
    return jnp.where(first, pltpu.roll(x, n - half, axis=1), pltpu.roll(x, half, axis=1))


def _attend(qs, kb, vb, sink_row, valid):
    return _softmax_pv(_dot_nt(kb, qs), vb, sink_row, valid)


def _softmax_pv(s, vb, sink_row, valid):
    if valid is not None:
        s = jnp.where(valid, s, NEG_BIG)
    sink2 = sink_row * LOG2E
    m = jnp.maximum(jnp.max(s, axis=0, keepdims=True), sink2)
    p = jnp.exp2(s - m)
    den = jnp.sum(p, axis=0, keepdims=True) + jnp.exp2(sink2 - m)
    o_t = _dot_tn(vb, p.astype(BF16)) * (1.0 / den)
    return o_t.T


def _sink_row(sink_ref, base, t):
    lanes = lax.broadcasted_iota(jnp.int32, (1, B_GROUP * t), 1)
    row = jnp.zeros((1, B_GROUP * t), F32)
    for g in range(B_GROUP):
        row = jnp.where((lanes >= g * t) & (lanes < (g + 1) * t), sink_ref[base + g], row)
    return row


def _swa_prompt_kernel(sink_ref, q_ref, k_ref, v_ref, cq_ref, sq_ref, ck_ref, sk_ref,
                       o_ref, kr_ref, vr_ref, kro_ref, vbo_ref, *, t, layer):
    khp = pl.program_id(1)
    step = pl.program_id(2)

    @pl.when(step == 0)
    def _():
        kr = _rope(k_ref[...], ck_ref[...], sk_ref[...])
        kro_ref[...] = kr.astype(BF16)
        vbo_ref[...] = v_ref[...].astype(BF16)
        kr_ref[0] = kr[t - WINDOW:, :]
        vr_ref[0] = v_ref[t - WINDOW:, :]

    band = (WINDOW_CHUNKS + 1) * CHUNK
    sink_rows = [_sink_row(sink_ref, layer * B_HEADS + (khp * 2 + kv) * B_GROUP, CHUNK)
                 for kv in range(2)]
    for ci in range(SWA_CB):
        cidx = step * SWA_CB + ci
        rows = slice(ci * CHUNK, (ci + 1) * CHUNK)
        s0 = pl.multiple_of(jnp.maximum(cidx - WINDOW_CHUNKS, 0) * CHUNK, CHUNK)
        qr = _rope(q_ref[rows, :], cq_ref[rows, :], sq_ref[rows, :]).astype(BF16)
        kband = kro_ref[pl.ds(s0, band), :]
        vband = vbo_ref[pl.ds(s0, band), :]
        key_pos = s0 + lax.broadcasted_iota(jnp.int32, (band, 1), 0)
        valid = key_pos < (cidx + 1) * CHUNK
        for kv in range(2):
            base = kv * B_GROUP * B_HD
            qs = jnp.concatenate(
                [qr[:, base + g * B_HD:base + (g + 1) * B_HD] for g in range(B_GROUP)], axis=0)
            o = _attend(qs, kband[:, kv * B_HD:(kv + 1) * B_HD],
                        vband[:, kv * B_HD:(kv + 1) * B_HD], sink_rows[kv], valid)
            for g in range(0, B_GROUP, 2):
                o_ref[rows, base + g * B_HD:base + (g + 2) * B_HD] = jnp.concatenate(
                    [o[g * CHUNK:(g + 1) * CHUNK], o[(g + 1) * CHUNK:(g + 2) * CHUNK]],
                    axis=1).astype(o_ref.dtype)


def _swa_prompt(proj, sinks, layer, nb, t):
    rb = SWA_CB * CHUNK
    nc = t // rb
    cos_t, sin_t = _rope_tables(jnp.arange(t))
    qw = 2 * B_GROUP * B_HD
    qoff = COL_B_Q * LANES // qw
    kernel = functools.partial(_swa_prompt_kernel, t=t, layer=layer)
    return pl.pallas_call(
        kernel,
        grid=(nb, B_KV_HEADS // 2, nc),
        in_specs=[pl.BlockSpec(memory_space=pltpu.SMEM),
                  pl.BlockSpec((rb, qw), lambda b, p, c: (b * nc + c, qoff + p)),
                  pl.BlockSpec((t, LANES), lambda b, p, c: (b, COL_B_K + p)),
                  pl.BlockSpec((t, LANES), lambda b, p, c: (b, COL_B_V + p)),
                  pl.BlockSpec((rb, LANES), lambda b, p, c: (c, 0)),
                  pl.BlockSpec((rb, LANES), lambda b, p, c: (c, 0)),
                  pl.BlockSpec((t, LANES), lambda b, p, c: (0, 0)),
                  pl.BlockSpec((t, LANES), lambda b, p, c: (0, 0))],
        out_specs=[pl.BlockSpec((rb, qw), lambda b, p, c: (b * nc + c, p)),
                   pl.BlockSpec((1, WINDOW, LANES), lambda b, p, c: (b, 0, p)),
                   pl.BlockSpec((1, WINDOW, LANES), lambda b, p, c: (b, 0, p))],
        out_shape=[jax.ShapeDtypeStruct((nb * t, B_WIDTH), BF16),
                   jax.ShapeDtypeStruct((nb, WINDOW, B_KV_HEADS * B_HD), F32),
                   jax.ShapeDtypeStruct((nb, WINDOW, B_KV_HEADS * B_HD), F32)],
        scratch_shapes=[pltpu.VMEM((t, LANES), BF16), pltpu.VMEM((t, LANES), BF16)],
        compiler_params=_cparams(("parallel", "parallel", "arbitrary")),
        name="swa_prompt",
    )(sinks.reshape(-1), proj, proj, proj, cos_t * Q_SCALE, sin_t * Q_SCALE, cos_t, sin_t)


def _swa_sample_kernel(sink_ref, q_ref, k_ref, v_ref, ckc_ref, cvc_ref, cos_ref, sin_ref,
                       o_ref, kr_ref, vr_ref, *, t, layer):
    cos_t = cos_ref[...]
    sin_t = sin_ref[...]
    kr = _rope(k_ref[...], cos_t, sin_t)
    v = v_ref[...]
    kr_ref[...] = kr
    vr_ref[...] = v
    qr = (_rope(q_ref[...], cos_t, sin_t) * Q_SCALE).astype(BF16)
    ka = jnp.concatenate([ckc_ref[0], kr], axis=0).astype(BF16)
    va = jnp.concatenate([cvc_ref[0], v], axis=0).astype(BF16)
    for kh in range(B_KV_HEADS):
        base = kh * B_GROUP * B_HD
        qs = jnp.concatenate(
            [qr[:, base + g * B_HD:base + (g + 1) * B_HD] for g in range(B_GROUP)], axis=0)
        sink_row = _sink_row(sink_ref, layer * B_HEADS + kh * B_GROUP, t)
        o = _attend(qs, ka[:, kh * B_HD:(kh + 1) * B_HD], va[:, kh * B_HD:(kh + 1) * B_HD],
                    sink_row, None)
        for g in range(0, B_GROUP, 2):
            o_ref[:, base + g * B_HD:base + (g + 2) * B_HD] = jnp.concatenate(
                [o[g * t:(g + 1) * t], o[(g + 1) * t:(g + 2) * t]], axis=1).astype(o_ref.dtype)


def _swa_sample(proj, cache_k, cache_v, sinks, layer, nb, t):
    cos_t, sin_t = _rope_tables(PAST_LEN + jnp.arange(t))
    kvw = B_KV_HEADS * B_HD
    kernel = functools.partial(_swa_sample_kernel, t=t, layer=layer)
    return pl.pallas_call(
        kernel,
        grid=(nb,),
        in_specs=[pl.BlockSpec(memory_space=pltpu.SMEM),
                  pl.BlockSpec((t, B_WIDTH), lambda b: (b, COL_B_Q * LANES // B_WIDTH)),
                  pl.BlockSpec((t, kvw), lambda b: (b, COL_B_K * LANES // kvw)),
                  pl.BlockSpec((t, kvw), lambda b: (b, COL_B_V * LANES // kvw)),
                  pl.BlockSpec((1, WINDOW, kvw), lambda b: (b, 0, 0)),
                  pl.BlockSpec((1, WINDOW, kvw), lambda b: (b, 0, 0)),
                  pl.BlockSpec((t, LANES), lambda b: (0, 0)),
                  pl.BlockSpec((t, LANES), lambda b: (0, 0))],
        out_specs=[pl.BlockSpec((t, B_WIDTH), lambda b: (b, 0)),
                   pl.BlockSpec((t, kvw), lambda b: (b, 0)),
                   pl.BlockSpec((t, kvw), lambda b: (b, 0))],
        out_shape=[jax.ShapeDtypeStruct((nb * t, B_WIDTH), BF16),
                   jax.ShapeDtypeStruct((nb * t, kvw), F32),
                   jax.ShapeDtypeStruct((nb * t, kvw), F32)],
        compiler_params=_cparams(("parallel",)),
        name="swa_sample",
    )(sinks.reshape(-1), proj, proj, proj,
      cache_k.reshape(nb, WINDOW, kvw), cache_v.reshape(nb, WINDOW, kvw), cos_t, sin_t)


def _layer_sample(x, w, layer, nb, t, cache, s_a, s_c, ff_pad):
    m, d = x.shape
    h = _rmsnorm(x, w["norm_mix"][layer], BF16, m)
    proj, wb_in = _mm_in_cast(h, w["w_in"], layer, IN_COLS_PAD, MM_TN_IN)
    o_a, s_a_new = _hgrn(proj, w["lb_logits"], w["hgrn_norm"][layer], s_a, layer, nb, t, t)
    o_b, k_rows, v_rows = _swa_sample(proj, cache[0], cache[1], w["sinks"], layer, nb, t)
    o_c, s_c_new = _gla(proj, w["w_a2"][layer], w["b_a"][layer], w["gla_norm"][layer],
                        s_c, nb, t, t)
    x, wb_out = _mm_out_cast(o_a, o_b, o_c, w["w_out"], x, layer, min(d, 512))
    h = _rmsnorm(x, w["norm_ffn"][layer], BF16, m)
    mid, wb_gate, wb_up = _gate_up_cast(h, w["w_gate_up"], layer, ff_pad)
    x, wb_down = _down_cast(mid, w["w_down"], x, layer)
    outs = (k_rows.reshape(nb, t, B_KV_HEADS, B_HD), v_rows.reshape(nb, t, B_KV_HEADS, B_HD),
            s_a_new, s_c_new)
    return x, outs, dict(w_in=wb_in, w_out=wb_out, w_gate=wb_gate, w_up=wb_up, w_down=wb_down)


def _layer_prompt(x, normed, w, wb, layer, nb, t):
    m, d = x.shape
    tm = min(m, 1024)
    if normed is None:
        h = _rmsnorm(x, w["norm_mix"][layer], BF16, min(m, 256))
        proj = _matmul(h, wb["w_in"], tm, MM_TN_IN, F32)
    else:
        proj = _matmul(normed[0], wb["w_in"], tm, MM_TN_IN, F32, normed[1])
    o_a, s_a_new = _hgrn(proj, w["lb_logits"], w["hgrn_norm"][layer], None, layer, nb, t, CHUNK)
    o_b, k_rows, v_rows = _swa_prompt(proj, w["sinks"], layer, nb, t)
    o_c, s_c_new = _gla(proj, w["w_a2"][layer], w["b_a"][layer], w["gla_norm"][layer],
                        None, nb, t, CHUNK)
    x, xg, ssq = _matmul_out(o_a, o_b, o_c, wb["w_out"], x, w["norm_ffn"][layer], tm, min(d, 512))
    mid = _matmul_gate_up(xg, ssq, wb["w_gate"], wb["w_up"], tm, 512)
    tk = wb["w_down"].shape[0] // 4
    if layer + 1 < DEPTH:
        x, xg, ssq = _matmul_down(mid, wb["w_down"], x, tm, min(d, 1024), tk,
                                  w["norm_mix"][layer + 1])
        normed = (xg, ssq)
    else:
        x = _matmul_down(mid, wb["w_down"], x, tm, min(d, 1024), tk)
        normed = None
    outs = (k_rows.reshape(nb, WINDOW, B_KV_HEADS, B_HD),
            v_rows.reshape(nb, WINDOW, B_KV_HEADS, B_HD), s_a_new, s_c_new)
    return x, normed, outs


def kernel(x_prompt, x_sample, cache_k_swa, cache_v_swa, state_hgrn, state_gla, norm_mix, w_in,
           hgrn_lb_logits, hgrn_norm, swa_sinks, gla_w_alpha2, gla_b_alpha, gla_norm, w_out,
           norm_ffn, w_gate_up, w_down, norm_final):
    n_p, t_p, d = x_prompt.shape
    n_s, t_s, _ = x_sample.shape
    d_ff = w_down.shape[1]
    ff_pad = -(-d_ff // 1024) * 1024
    w = dict(
        norm_mix=norm_mix, norm_ffn=norm_ffn, hgrn_norm=hgrn_norm, gla_norm=gla_norm,
        lb_logits=hgrn_lb_logits, sinks=swa_sinks, b_a=gla_b_alpha,
        w_in=w_in, w_out=w_out, w_gate_up=w_gate_up, w_down=w_down,
        w_a2=jnp.pad(gla_w_alpha2, ((0, 0), (0, LANES - GLA_RANK), (0, 0))).astype(BF16),
    )
    xp = x_prompt.reshape(n_p * t_p, d)
    xs = x_sample.reshape(n_s * t_s, d)
    outs_p, outs_s = [], []
    normed = None
    for layer in range(DEPTH):
        xs, rest, wb = _layer_sample(xs, w, layer, n_s, t_s,
                                     (cache_k_swa[layer], cache_v_swa[layer]),
                                     state_hgrn[layer], state_gla[layer], ff_pad)
        outs_s.append(rest)
        xp, normed, rest = _layer_prompt(xp, normed, w, wb, layer, n_p, t_p)
        outs_p.append(rest)
    y_p = _rmsnorm(xp, norm_final, F32, min(xp.shape[0], 256)).reshape(n_p, t_p, d)
    y_s = _rmsnorm(xs, norm_final, F32, min(xs.shape[0], 256)).reshape(n_s, t_s, d)
    stack = lambda outs, i: jnp.stack([o[i] for o in outs])
    return (y_p, y_s,
            stack(outs_p, 0), stack(outs_p, 1), stack(outs_p, 2), stack(outs_p, 3),
            stack(outs_s, 0), stack(outs_s, 1), stack(outs_s, 2), stack(outs_s, 3))
```

```python
import functools

import jax
import jax.numpy as jnp
import numpy as np
from jax import lax
from jax.experimental import pallas as pl
from jax.experimental.pallas import tpu as pltpu

F32 = jnp.float32
BF16 = jnp.bfloat16

DEPTH = 2
PAST_LEN = 4096
CHUNK = 64
EPS = 1e-6
NEG_BIG = -1e30
LB_FLOOR = 1e-30
A_HEADS = 8
A_DK = 128
A_DV = 128
A_WIDTH = A_HEADS * A_DV
B_HEADS = 32
B_KV_HEADS = 4
B_GROUP = B_HEADS // B_KV_HEADS
B_HD = 64
B_WIDTH = B_HEADS * B_HD
B_SCALE = B_HD ** -0.5
WINDOW = 128
WINDOW_CHUNKS = WINDOW // CHUNK
ROPE_THETA = 10000.0
C_HEADS = 8
C_DK = 64
C_DV = 128
C_WIDTH = C_HEADS * C_DV
GLA_RANK = 16
GLA_NORMALIZER = 16.0

LANES = 128

COL_A_Q = 0
COL_A_F = 8
COL_A_I = 16
COL_A_G = 24
COL_B_Q = 32
COL_B_K = 48
COL_B_V = 50
COL_C_Q = 52
COL_C_K = 56
COL_C_V = 60
COL_C_R = 68
COL_C_A = 76
IN_COLS = 9744
IN_COLS_PAD = 9984
MM_TN_IN = 768
VMEM_LIMIT = 56 * 1024 * 1024


def _cparams(sem):
    return pltpu.CompilerParams(dimension_semantics=sem, vmem_limit_bytes=VMEM_LIMIT)


def _dot(a, b):
    return jnp.dot(a, b, preferred_element_type=F32)


def _dot_nt(a, b):
    return lax.dot_general(a, b, (((1,), (1,)), ((), ())), preferred_element_type=F32)


def _dot_tn(a, b):
    return lax.dot_general(a, b, (((0,), (0,)), ((), ())), preferred_element_type=F32)


def _rmsnorm_kernel(x_ref, g_ref, o_ref):
    x = x_ref[...]
    var = jnp.mean(x * x, axis=-1, keepdims=True)
    o_ref[...] = (x * lax.rsqrt(var + EPS) * g_ref[...]).astype(o_ref.dtype)


def _rmsnorm(x, g, out_dtype, tm):
    m, d = x.shape
    return pl.pallas_call(
        _rmsnorm_kernel,
        grid=(m // tm,),
        in_specs=[pl.BlockSpec((tm, d), lambda i: (i, 0)),
                  pl.BlockSpec((1, d), lambda i: (0, 0))],
        out_specs=pl.BlockSpec((tm, d), lambda i: (i, 0)),
        out_shape=jax.ShapeDtypeStruct((m, d), out_dtype),
        compiler_params=_cparams(("parallel",)),
        name="rmsnorm",
    )(x, g.reshape(1, d))


def _row_scale(ssq_ref, d):
    return lax.rsqrt(jnp.sum(ssq_ref[...], axis=-1, keepdims=True) * (1.0 / d) + EPS)


def _lane_partial_sq(x):
    sq = x * x
    part = sq[:, 0:LANES]
    for c in range(1, x.shape[1] // LANES):
        part = part + sq[:, c * LANES:(c + 1) * LANES]
    return part


def _accumulate(ref, part, first):
    @pl.when(first)
    def _():
        ref[...] = part

    @pl.when(jnp.logical_not(first))
    def _():
        ref[...] += part


def _emit_normed(x, g_ref, xg_ref, ssq_ref, first):
    xg_ref[...] = (x * g_ref[...]).astype(xg_ref.dtype)
    _accumulate(ssq_ref, _lane_partial_sq(x), first)


def _mm_kernel(x_ref, w_ref, o_ref):
    o_ref[...] = _dot(x_ref[...], w_ref[...]).astype(o_ref.dtype)


def _mm_scaled_kernel(x_ref, s_ref, w_ref, o_ref):
    r = _row_scale(s_ref, x_ref.shape[1])
    o_ref[...] = (_dot(x_ref[...], w_ref[...]) * r).astype(o_ref.dtype)


def _matmul(x, w, tm, tn, out_dtype, ssq=None):
    m, k = x.shape
    n = w.shape[1]
    x_spec = pl.BlockSpec((tm, k), lambda i, j: (i, 0))
    w_spec = pl.BlockSpec((k, tn), lambda i, j: (0, j))
    s_spec = pl.BlockSpec((tm, LANES), lambda i, j: (i, 0))
    return pl.pallas_call(
        _mm_kernel if ssq is None else _mm_scaled_kernel,
        grid=(m // tm, n // tn),
        in_specs=[x_spec, w_spec] if ssq is None else [x_spec, s_spec, w_spec],
        out_specs=pl.BlockSpec((tm, tn), lambda i, j: (i, j)),
        out_shape=jax.ShapeDtypeStruct((m, n), out_dtype),
        compiler_params=_cparams(("parallel", "arbitrary")),
        name="mm_in",
    )(*((x, w) if ssq is None else (x, ssq, w)))


def _mm_out_kernel(oa_ref, ob_ref, oc_ref, w_ref, r_ref, g_ref, o_ref, xg_ref, ssq_ref):
    j = pl.program_id(1)
    x = (r_ref[...] + _dot(oa_ref[...], w_ref[0:A_WIDTH, :])
         + _dot(ob_ref[...], w_ref[A_WIDTH:A_WIDTH + B_WIDTH, :])
         + _dot(oc_ref[...], w_ref[A_WIDTH + B_WIDTH:, :]))
    o_ref[...] = x
    _emit_normed(x, g_ref, xg_ref, ssq_ref, j == 0)


def _matmul_out(oa, ob, oc, w, resid, gain, tm, tn):
    m = oa.shape[0]
    k, n = w.shape
    tile = pl.BlockSpec((tm, tn), lambda i, j: (i, j))
    return pl.pallas_call(
        _mm_out_kernel,
        grid=(m // tm, n // tn),
        in_specs=[pl.BlockSpec((tm, A_WIDTH), lambda i, j: (i, 0)),
                  pl.BlockSpec((tm, B_WIDTH), lambda i, j: (i, 0)),
                  pl.BlockSpec((tm, C_WIDTH), lambda i, j: (i, 0)),
                  pl.BlockSpec((k, tn), lambda i, j: (0, j)),
                  tile,
                  pl.BlockSpec((1, tn), lambda i, j: (0, j))],
        out_specs=[tile, tile, pl.BlockSpec((tm, LANES), lambda i, j: (i, 0))],
        out_shape=[jax.ShapeDtypeStruct((m, n), F32), jax.ShapeDtypeStruct((m, n), BF16),
                   jax.ShapeDtypeStruct((m, LANES), F32)],
        compiler_params=_cparams(("parallel", "arbitrary")),
        name="mm_out",
    )(oa, ob, oc, w, resid, gain.reshape(1, n))


def _mm_gate_up_kernel(x_ref, s_ref, wg_ref, wu_ref, o_ref):
    x = x_ref[...]
    r = _row_scale(s_ref, x.shape[1])
    gate = _dot(x, wg_ref[...]) * r
    up = _dot(x, wu_ref[...]) * r
    o_ref[...] = (gate * (1.0 / (1.0 + jnp.exp(-gate))) * up).astype(o_ref.dtype)


def _matmul_gate_up(x, ssq, wg, wu, tm, tn):
    m, k = x.shape
    n = wg.shape[1]
    return pl.pallas_call(
        _mm_gate_up_kernel,
        grid=(m // tm, n // tn),
        in_specs=[pl.BlockSpec((tm, k), lambda i, j: (i, 0)),
                  pl.BlockSpec((tm, LANES), lambda i, j: (i, 0)),
                  pl.BlockSpec((k, tn), lambda i, j: (0, j)),
                  pl.BlockSpec((k, tn), lambda i, j: (0, j))],
        out_specs=pl.BlockSpec((tm, tn), lambda i, j: (i, j)),
        out_shape=jax.ShapeDtypeStruct((m, n), BF16),
        compiler_params=_cparams(("parallel", "arbitrary")),
        name="mm_gate_up",
    )(x, ssq, wg, wu)


def _mm_down_kernel(x_ref, w_ref, r_ref, *rest, nk, emit):
    kk = pl.program_id(2)
    o_ref = rest[1] if emit else rest[0]

    @pl.when(kk == 0)
    def _():
        o_ref[...] = r_ref[...] + _dot(x_ref[...], w_ref[...])

    if not emit:
        @pl.when(kk != 0)
        def _():
            o_ref[...] += _dot(x_ref[...], w_ref[...])
    else:
        assert nk > 1
        g_ref, _, xg_ref, ssq_ref = rest

        @pl.when((kk != 0) & (kk != nk - 1))
        def _():
            o_ref[...] += _dot(x_ref[...], w_ref[...])

        @pl.when(kk == nk - 1)
        def _():
            x = o_ref[...] + _dot(x_ref[...], w_ref[...])
            o_ref[...] = x
            _emit_normed(x, g_ref, xg_ref, ssq_ref, pl.program_id(1) == 0)


def _matmul_down(x, w, resid, tm, tn, tk, gain=None):
    m, k = x.shape
    n = w.shape[1]
    emit = gain is not None
    tile = pl.BlockSpec((tm, tn), lambda i, j, kk: (i, j))
    in_specs = [pl.BlockSpec((tm, tk), lambda i, j, kk: (i, kk)),
                pl.BlockSpec((tk, tn), lambda i, j, kk: (kk, j)),
                tile]
    out_specs, out_shape, args = [tile], [jax.ShapeDtypeStruct((m, n), F32)], [x, w, resid]
    if emit:
        in_specs.append(pl.BlockSpec((1, tn), lambda i, j, kk: (0, j)))
        args.append(gain.reshape(1, n))
        out_specs += [tile, pl.BlockSpec((tm, LANES), lambda i, j, kk: (i, 0))]
        out_shape += [jax.ShapeDtypeStruct((m, n), BF16), jax.ShapeDtypeStruct((m, LANES), F32)]
    out = pl.pallas_call(
        functools.partial(_mm_down_kernel, nk=k // tk, emit=emit),
        grid=(m // tm, n // tn, k // tk),
        in_specs=in_specs,
        out_specs=out_specs,
        out_shape=out_shape,
        compiler_params=_cparams(("parallel", "arbitrary", "arbitrary")),
        name="mm_down",
    )(*args)
    return out if emit else out[0]


CAST_TN_GATE = 256
CAST_TK_DOWN = 512


def _masked_bf16(w, first, n_valid, axis):
    idx = first + lax.broadcasted_iota(jnp.int32, w.shape, axis)
    return jnp.where(idx < n_valid, w, 0.0).astype(BF16)


def _mm_in_cast_kernel(x_ref, w_ref, o_ref, wb_ref, *, n_valid, tn):
    idx = pl.program_id(0) * tn + lax.broadcasted_iota(jnp.int32, w_ref.shape, 0)
    wb = jnp.where(idx < n_valid, w_ref[...], 0.0).T.astype(BF16)
    wb_ref[...] = wb
    o_ref[...] = _dot(x_ref[...], wb)


def _mm_in_cast(x, w3, layer, n_pad, tn):
    m, k = x.shape
    n = w3.shape[2]
    return pl.pallas_call(
        functools.partial(_mm_in_cast_kernel, n_valid=n, tn=tn),
        grid=(n_pad // tn,),
        in_specs=[pl.BlockSpec((m, k), lambda j: (0, 0)),
                  pl.BlockSpec((None, tn, k), lambda j: (layer, j, 0))],
        out_specs=[pl.BlockSpec((m, tn), lambda j: (0, j)),
                   pl.BlockSpec((k, tn), lambda j: (0, j))],
        out_shape=[jax.ShapeDtypeStruct((m, n_pad), F32),
                   jax.ShapeDtypeStruct((k, n_pad), BF16)],
        compiler_params=_cparams(("arbitrary",)),
        name="mm_in_cast",
    )(x, jnp.swapaxes(w3, 1, 2))


def _mm_out_cast_kernel(oa_ref, ob_ref, oc_ref, w_ref, r_ref, o_ref, wb_ref, cat_ref):
    @pl.when(pl.program_id(0) == 0)
    def _():
        cat_ref[:, 0:A_WIDTH] = oa_ref[...]
        cat_ref[:, A_WIDTH:A_WIDTH + B_WIDTH] = ob_ref[...]
        cat_ref[:, A_WIDTH + B_WIDTH:] = oc_ref[...]

    wb = w_ref[...].astype(BF16)
    wb_ref[...] = wb
    o_ref[...] = r_ref[...] + _dot(cat_ref[...], wb)


def _mm_out_cast(oa, ob, oc, w3, resid, layer, tn):
    m = oa.shape[0]
    k, n = w3.shape[1:]
    return pl.pallas_call(
        _mm_out_cast_kernel,
        grid=(n // tn,),
        in_specs=[pl.BlockSpec((m, A_WIDTH), lambda j: (0, 0)),
                  pl.BlockSpec((m, B_WIDTH), lambda j: (0, 0)),
                  pl.BlockSpec((m, C_WIDTH), lambda j: (0, 0)),
                  pl.BlockSpec((None, k, tn), lambda j: (layer, 0, j)),
                  pl.BlockSpec((m, tn), lambda j: (0, j))],
        out_specs=[pl.BlockSpec((m, tn), lambda j: (0, j)),
                   pl.BlockSpec((k, tn), lambda j: (0, j))],
        out_shape=[jax.ShapeDtypeStruct((m, n), F32),
                   jax.ShapeDtypeStruct((k, n), BF16)],
        scratch_shapes=[pltpu.VMEM((m, k), BF16)],
        compiler_params=_cparams(("arbitrary",)),
        name="mm_out_cast",
    )(oa, ob, oc, w3, resid)


def _gate_up_cast_kernel(x_ref, wg_ref, wu_ref, o_ref, wgb_ref, wub_ref, *, n_tiles):
    valid = pl.program_id(0) < n_tiles
    wg = jnp.where(valid, wg_ref[...], 0.0).astype(BF16)
    wu = jnp.where(valid, wu_ref[...], 0.0).astype(BF16)
    wgb_ref[...] = wg
    wub_ref[...] = wu
    x = x_ref[...]
    gate = _dot(x, wg)
    o_ref[...] = (gate * (1.0 / (1.0 + jnp.exp(-gate))) * _dot(x, wu)).astype(o_ref.dtype)


def _gate_up_cast(x, wgu3, layer, ff_pad):
    m, k = x.shape
    tn = CAST_TN_GATE
    n_tiles = wgu3.shape[2] // 2 // tn
    assert n_tiles * tn * 2 == wgu3.shape[2]
    return pl.pallas_call(
        functools.partial(_gate_up_cast_kernel, n_tiles=n_tiles),
        grid=(ff_pad // tn,),
        in_specs=[pl.BlockSpec((m, k), lambda j: (0, 0)),
                  pl.BlockSpec((None, k, tn), lambda j: (layer, 0, jnp.minimum(j, n_tiles - 1))),
                  pl.BlockSpec((None, k, tn),
                               lambda j: (layer, 0, n_tiles + jnp.minimum(j, n_tiles - 1)))],
        out_specs=[pl.BlockSpec((m, tn), lambda j: (0, j)),
                   pl.BlockSpec((k, tn), lambda j: (0, j)),
                   pl.BlockSpec((k, tn), lambda j: (0, j))],
        out_shape=[jax.ShapeDtypeStruct((m, ff_pad), BF16),
                   jax.ShapeDtypeStruct((k, ff_pad), BF16),
                   jax.ShapeDtypeStruct((k, ff_pad), BF16)],
        compiler_params=_cparams(("arbitrary",)),
        name="gate_up_cast",
    )(x, wgu3, wgu3)


def _down_cast_kernel(x_ref, w_ref, r_ref, o_ref, wb_ref, *, k_valid, tk):
    kk = pl.program_id(0)
    wb = _masked_bf16(w_ref[...], kk * tk, k_valid, 0)
    wb_ref[...] = wb

    @pl.when(kk == 0)
    def _():
        o_ref[...] = r_ref[...] + _dot(x_ref[...], wb)

    @pl.when(kk != 0)
    def _():
        o_ref[...] += _dot(x_ref[...], wb)


def _down_cast(x, wd3, resid, layer):
    m, ff_pad = x.shape
    ff, d = wd3.shape[1:]
    tk = CAST_TK_DOWN
    last = (ff - 1) // tk
    return pl.pallas_call(
        functools.partial(_down_cast_kernel, k_valid=ff, tk=tk),
        grid=(ff_pad // tk,),
        in_specs=[pl.BlockSpec((m, tk), lambda kk: (0, kk)),
                  pl.BlockSpec((None, tk, d), lambda kk: (layer, jnp.minimum(kk, last), 0)),
                  pl.BlockSpec((m, d), lambda kk: (0, 0))],
        out_specs=[pl.BlockSpec((m, d), lambda kk: (0, 0)),
                   pl.BlockSpec((tk, d), lambda kk: (kk, 0))],
        out_shape=[jax.ShapeDtypeStruct((m, d), F32),
                   jax.ShapeDtypeStruct((ff_pad, d), BF16)],
        compiler_params=_cparams(("arbitrary",)),
        name="down_cast",
    )(x, wd3, resid)


LOG2E = 1.4426950408889634
SUBLANES = 8
HGRN_W = 256
GLA_W = 256
GLR_UNROLL = 8
GLR_UNROLL_STATE = 16


def _levels(c):
    out, b = [], 1
    while b < c:
        out.append(b)
        b *= 2
    return out


def _prefix_matrix(c):
    t = np.arange(c)[:, None]
    s = np.arange(c)[None, :]
    low = s <= t
    mats = [low]
    for b in _levels(c):
        if b >= SUBLANES:
            break
        same = (t // b) == (s // b)
        odd = ((t // b) % 2) == 1
        mats.append(np.where(odd, low & same, (~low) & same))
    m = np.concatenate(mats, axis=0).astype(np.float32)
    return jnp.asarray(np.concatenate([m, m, m], axis=1), dtype=BF16)


def _level_sums(g, b):
    parts = []
    for j in range(g.shape[0] // b):
        blk = g[j * b:(j + 1) * b]
        if j % 2 == 1:
            parts.append(blk - g[j * b - 1:j * b])
        else:
            parts.append(g[(j + 1) * b - 1:(j + 1) * b] - blk)
    return jnp.concatenate(parts, axis=0)


def _diag_matrix(c, nh, w):
    kd = w // nh
    lane = np.arange(w)[:, None]
    col = np.arange(nh * c)[None, :]
    return jnp.asarray(((col // c) == (lane // kd)).astype(np.float32), dtype=BF16)


def _glr_consts(c, nh, w):
    t = lax.broadcasted_iota(jnp.int32, (c, nh * c), 0)
    s = lax.broadcasted_iota(jnp.int32, (c, nh * c), 1) & (c - 1)
    lvl_masks = []
    for b in _levels(c):
        lg = b.bit_length() - 1
        tb = lax.shift_right_logical(t, lg)
        sb = lax.shift_right_logical(s, lg)
        lvl_masks.append((lax.shift_right_logical(tb, 1) == lax.shift_right_logical(sb, 1))
                         & ((tb & 1) == 1) & ((sb & 1) == 0))
    lgk = (w // nh).bit_length() - 1
    lane = lax.shift_right_logical(lax.broadcasted_iota(jnp.int32, (1, w), 1), lgk)
    srow = lax.broadcasted_iota(jnp.int32, (nh * LANES, w), 0) // LANES
    scol = lax.shift_right_logical(lax.broadcasted_iota(jnp.int32, (nh * LANES, w), 1), lgk)
    return dict(lvl_masks=lvl_masks, diag_valid=t == s,
                head_lanes=[lane == h for h in range(nh)], st_mask=srow == scol)


def _head_rows(x, cst, nh):
    if nh * LANES == x.shape[1]:
        z = jnp.zeros((x.shape[0], LANES), x.dtype)
        return jnp.concatenate(
            [jnp.concatenate([x[:, h * LANES:(h + 1) * LANES] if g == h else z
                              for g in range(nh)], axis=1) for h in range(nh)], axis=0)
    zero = jnp.zeros_like(x)
    return jnp.concatenate([jnp.where(cst["head_lanes"][h], x, zero) for h in range(nh)], axis=0)


def _glr_prefix(lf2, pm):
    hi = lf2.astype(BF16)
    r1 = lf2 - hi.astype(F32)
    mid = r1.astype(BF16)
    lo = (r1 - mid.astype(F32)).astype(BF16)
    return _dot(pm, jnp.concatenate([hi, mid, lo], axis=0))


def _glr_intra(qs, ks, lf2s, pm, rd, cst, *, c, nh):
    n = len(qs)
    prefs = [_glr_prefix(lf2, pm) for lf2 in lf2s]
    gs = [p[0:c] for p in prefs]
    qhat = [(qs[j] * jnp.exp2(gs[j])).astype(BF16) for j in range(n)]
    kdec = [(ks[j] * jnp.exp2(gs[j][c - 1:c, :] - gs[j])).astype(BF16) for j in range(n)]
    a = [jnp.where(cst["diag_valid"], _dot((qs[j] * ks[j]).astype(BF16), rd), 0.0)
         for j in range(n)]
    for i, b in enumerate(_levels(c)):
        if b < SUBLANES:
            ebs = [jnp.exp2(prefs[j][(1 + i) * c:(2 + i) * c]) for j in range(n)]
        else:
            ebs = [jnp.exp2(_level_sums(gs[j], b)) for j in range(n)]
        dots = [_dot_nt((qs[j] * ebs[j]).astype(BF16),
                        _head_rows((ks[j] * ebs[j]).astype(BF16), cst, nh)) for j in range(n)]
        a = [jnp.where(cst["lvl_masks"][i], dots[j], a[j]) for j in range(n)]
    return [(a[j].astype(BF16), qhat[j], kdec[j], jnp.exp2(gs[j][c - 1:c, :]))
            for j in range(n)]


def _glr_state(chunks, st, cst, *, nh, streams=False):
    o_in, upd = [], []
    for a, _, kdec, _, vs in chunks:
        vb = [v.astype(BF16) for v in vs]
        z = jnp.zeros_like(vb[0])
        vbd = jnp.concatenate(
            [jnp.concatenate([vb[h] if g == h else z for g in range(nh)], axis=1)
             for h in range(nh)], axis=0)
        o_in.append(_dot(a, vbd))
        upd.append(jnp.where(cst["st_mask"], _dot_tn(jnp.concatenate(vb, axis=1), kdec), 0.0))
    if streams:
        outs = [o_in[j] + _dot_nt(ch[1], st[j].astype(BF16)) for j, ch in enumerate(chunks)]
        return outs, [st[j] * ch[3] + upd[j] for j, ch in enumerate(chunks)]
    outs = []
    for j, (_, qhat, _, dec, _) in enumerate(chunks):
        outs.append(o_in[j] + _dot_nt(qhat, st.astype(BF16)))
        st = st * dec + upd[j]
    return outs, st


def _gated_norm(o, norm, gate):
    var = jnp.mean(o * o, axis=-1, keepdims=True)
    return o * lax.rsqrt(var + EPS) * norm * (gate * (1.0 / (1.0 + jnp.exp(-gate))))


def _glr_scratch(t, c, nh, w):
    return [pltpu.VMEM((t, nh * c), BF16), pltpu.VMEM((t, w), BF16),
            pltpu.VMEM((t, w), BF16), pltpu.VMEM((t // c, SUBLANES, w), F32)]


def _glr_run_streams(load_qkl, load_vg, store_o, st0s, pm, rd, cst, *, c, nh):
    rows = [slice(b * c, (b + 1) * c) for b in range(len(st0s))]
    ins = [load_qkl(r) for r in rows]
    res = _glr_intra([x[0] for x in ins], [x[1] for x in ins], [x[2] for x in ins],
                     pm, rd, cst, c=c, nh=nh)
    vg = [load_vg(r) for r in rows]
    outs, sts = _glr_state([r4 + (x[0],) for r4, x in zip(res, vg)], st0s, cst, nh=nh,
                           streams=True)
    for r, o, x in zip(rows, outs, vg):
        store_o(r, o, x[1])
    return sts


def _glr_run(load_qkl, load_vg, store_o, st0, pm, rd, cst, scratch, *, c, nchunks, nh, w):
    a_s, qh_s, kd_s, dec_s = scratch
    u = min(GLR_UNROLL, nchunks)
    assert nchunks % u == 0

    def rows_of(n):
        return pl.ds(pl.multiple_of(n * c, c), c)

    def intra(i, carry):
        ns = [i * u + j for j in range(u)]
        ins = [load_qkl(rows_of(n)) for n in ns]
        res = _glr_intra([x[0] for x in ins], [x[1] for x in ins], [x[2] for x in ins],
                         pm, rd, cst, c=c, nh=nh)
        for n, (a, qh, kd, dec) in zip(ns, res):
            rows = rows_of(n)
            a_s[rows, :] = a
            qh_s[rows, :] = qh
            kd_s[rows, :] = kd
            dec_s[n] = jnp.broadcast_to(dec, (SUBLANES, w))
        return carry

    lax.fori_loop(0, nchunks // u, intra, 0)

    us = min(GLR_UNROLL_STATE, nchunks)
    assert nchunks % us == 0

    def state(i, st):
        ns = [i * us + j for j in range(us)]
        ins = [(a_s[rows_of(n), :], qh_s[rows_of(n), :], kd_s[rows_of(n), :], dec_s[n, 0:1, :])
               + tuple(load_vg(rows_of(n))) for n in ns]
        outs, st = _glr_state([x[:5] for x in ins], st, cst, nh=nh)
        for n, o, x in zip(ns, outs, ins):
            store_o(rows_of(n), o, x[5])
        return st

    return lax.fori_loop(0, nchunks // us, state, st0)


def _hgrn_kernel(*refs, c, nchunks, layer, has_s0, sb):
    pm_ref, rd_ref, lbl_ref, norm_ref, q_ref, z_ref, v_ref, g_ref = refs[:8]
    s0_ref = refs[8] if has_s0 else None
    o_ref, s_ref = refs[8 + has_s0:10 + has_s0]
    scratch = refs[10 + has_s0:]
    w = HGRN_W
    nh = w // A_DK
    logits = lbl_ref[...]
    e = jnp.exp(logits - jnp.max(logits, axis=0, keepdims=True))
    probs = e / jnp.sum(e, axis=0, keepdims=True)
    lb = jnp.sum(probs[0:layer + 1], axis=0, keepdims=True) - probs[0:1]
    lb_floor = jnp.maximum(lb, LB_FLOOR)
    oml = 1.0 - lb
    norm = norm_ref[...]
    pm = pm_ref[...]
    rd = rd_ref[...]
    cst = _glr_consts(c, nh, w)
    heads = [slice(h * LANES, (h + 1) * LANES) for h in range(nh)]

    def load_qkl(rows):
        q = q_ref[rows, :]
        z = z_ref[rows, :]
        ez = jnp.exp(-jnp.abs(z))
        r = 1.0 / (1.0 + ez)
        pos = z >= 0.0
        lf2 = jnp.log2(lb_floor + oml * jnp.where(pos, r, ez * r))
        return q, oml * jnp.where(pos, ez * r, r), lf2

    def load_vg(rows):
        v = v_ref[rows, :]
        return [v[:, sl] for sl in heads], g_ref[rows, :]

    def store_o(rows, o, g):
        o_ref[rows, :] = jnp.concatenate(
            [_gated_norm(o[:, sl], norm, g[:, sl]) for sl in heads], axis=1).astype(o_ref.dtype)

    zero = jnp.zeros((A_DV, A_DK), F32)

    def init_state(b):
        if not has_s0:
            return jnp.zeros((nh * A_DV, w), F32)
        return jnp.concatenate(
            [jnp.concatenate([s0_ref[b, h].T if g == h else zero for g in range(nh)], axis=1)
             for h in range(nh)], axis=0)

    if nchunks == 1:
        sts = _glr_run_streams(load_qkl, load_vg, store_o, [init_state(b) for b in range(sb)],
                               pm, rd, cst, c=c, nh=nh)
    else:
        sts = [_glr_run(load_qkl, load_vg, store_o, init_state(0), pm, rd, cst, scratch,
                        c=c, nchunks=nchunks, nh=nh, w=w)]
    for b, st in enumerate(sts):
        for h in range(nh):
            s_ref[b, h] = st[h * A_DV:(h + 1) * A_DV, heads[h]].T


def _glr_streams_per_step(nb, nchunks):
    return nb if nchunks == 1 else 1


def _hgrn(proj, lbl, norm, s0, layer, nb, t, c):
    w = HGRN_W
    nh = w // A_DK
    pm = _prefix_matrix(c)
    rd = _diag_matrix(c, nh, w)
    nchunks = t // c
    has_s0 = s0 is not None
    sb = _glr_streams_per_step(nb, nchunks)

    def col(off):
        return pl.BlockSpec((sb * t, w), lambda b, h: (b, off // nh + h))

    st_spec = pl.BlockSpec((sb, nh, A_DK, A_DV), lambda b, h: (b, h, 0, 0))
    in_specs = [pl.BlockSpec(pm.shape, lambda b, h: (0, 0)),
                pl.BlockSpec(rd.shape, lambda b, h: (0, 0)),
                pl.BlockSpec((DEPTH, w), lambda b, h: (0, h)),
                pl.BlockSpec((1, LANES), lambda b, h: (0, 0)),
                col(COL_A_Q), col(COL_A_F), col(COL_A_I), col(COL_A_G)]
    args = [pm, rd, lbl, norm.reshape(1, LANES), proj, proj, proj, proj]
    if has_s0:
        in_specs.append(st_spec)
        args.append(s0)
    return pl.pallas_call(
        functools.partial(_hgrn_kernel, c=c, nchunks=nchunks, layer=layer, has_s0=has_s0, sb=sb),
        grid=(nb // sb, A_HEADS // nh),
        in_specs=in_specs,
        out_specs=[pl.BlockSpec((sb * t, w), lambda b, h: (b, h)), st_spec],
        out_shape=[jax.ShapeDtypeStruct((nb * t, A_WIDTH), BF16),
                   jax.ShapeDtypeStruct((nb, A_HEADS, A_DK, A_DV), F32)],
        scratch_shapes=_glr_scratch(t, c, nh, w) if nchunks > 1 else [],
        compiler_params=_cparams(("parallel", "parallel")),
        name="hgrn",
    )(*args)


def _gla_kernel(*refs, c, nchunks, has_s0, sb):
    pm_ref, rd_ref, wa_ref, ba_ref, norm_ref, q_ref, k_ref, v_ref, r_ref, ca_ref = refs[:10]
    s0_ref = refs[10] if has_s0 else None
    o_ref, s_ref = refs[10 + has_s0:12 + has_s0]
    scratch = refs[12 + has_s0:]
    wa = wa_ref[...]
    ba = ba_ref[...]
    norm = norm_ref[...]
    pm = pm_ref[...]
    rd = rd_ref[...]
    w = GLA_W
    nh = w // C_DK
    npl = w // LANES
    cst = _glr_consts(c, nh, w)
    heads = [slice(h * C_DV, (h + 1) * C_DV) for h in range(nh)]
    lane = lax.broadcasted_iota(jnp.int32, (1, LANES), 1)
    half = [lane < C_DK, lane >= C_DK]

    def load_qkl(rows):
        x = _dot(ca_ref[rows, :].astype(BF16), wa) + ba
        lf2 = (jnp.minimum(x, 0.0) - jnp.log1p(jnp.exp(-jnp.abs(x)))) * (LOG2E / GLA_NORMALIZER)
        return q_ref[rows, :] * (C_DK ** -0.5), k_ref[rows, :], lf2

    def load_vg(rows):
        v = v_ref[rows, :]
        return [v[:, sl] for sl in heads], r_ref[rows, :]

    def store_o(rows, o, gate):
        o_ref[rows, :] = jnp.concatenate(
            [_gated_norm(o[:, sl], norm, gate[:, sl]) for sl in heads], axis=1).astype(o_ref.dtype)

    zero = jnp.zeros((C_DV, LANES), F32)

    def init_state(b):
        if not has_s0:
            return jnp.zeros((nh * C_DV, w), F32)
        blocks = []
        for h in range(nh):
            p, h2 = divmod(h, 2)
            own = jnp.where(half[h2], s0_ref[b, p].T, 0.0)
            blocks.append(jnp.concatenate([own if g == p else zero for g in range(npl)], axis=1))
        return jnp.concatenate(blocks, axis=0)

    if nchunks == 1:
        sts = _glr_run_streams(load_qkl, load_vg, store_o, [init_state(b) for b in range(sb)],
                               pm, rd, cst, c=c, nh=nh)
    else:
        sts = [_glr_run(load_qkl, load_vg, store_o, init_state(0), pm, rd, cst, scratch,
                        c=c, nchunks=nchunks, nh=nh, w=w)]
    for b, st in enumerate(sts):
        for p in range(npl):
            cols = slice(p * LANES, (p + 1) * LANES)
            pair = (st[2 * p * C_DV:(2 * p + 1) * C_DV, cols]
                    + st[(2 * p + 1) * C_DV:(2 * p + 2) * C_DV, cols])
            s_ref[b, p] = pair.T


def _gla(proj, wa2, ba, norm, s0, nb, t, c):
    w = GLA_W
    nh = w // C_DK
    npl = w // LANES
    pm = _prefix_matrix(c)
    rd = _diag_matrix(c, nh, w)
    nchunks = t // c
    has_s0 = s0 is not None
    npair = C_HEADS // 2
    sb = _glr_streams_per_step(nb, nchunks)

    def col(off, width):
        return pl.BlockSpec((sb * t, width), lambda b, p: (b, off * LANES // width + p))

    st_spec = pl.BlockSpec((sb, npl, LANES, C_DV), lambda b, p: (b, p, 0, 0))
    in_specs = [pl.BlockSpec(pm.shape, lambda b, p: (0, 0)),
                pl.BlockSpec(rd.shape, lambda b, p: (0, 0)),
                pl.BlockSpec((LANES, w), lambda b, p: (0, p)),
                pl.BlockSpec((1, w), lambda b, p: (0, p)),
                pl.BlockSpec((1, LANES), lambda b, p: (0, 0)),
                col(COL_C_Q, w), col(COL_C_K, w), col(COL_C_V, 2 * w), col(COL_C_R, 2 * w),
                pl.BlockSpec((sb * t, LANES), lambda b, p: (b, COL_C_A))]
    args = [pm, rd, wa2, ba.reshape(1, -1), norm.reshape(1, LANES), proj, proj, proj, proj, proj]
    if has_s0:
        in_specs.append(st_spec)
        args.append(s0.reshape(nb, npair, 2 * C_DK, C_DV))
    o, s = pl.pallas_call(
        functools.partial(_gla_kernel, c=c, nchunks=nchunks, has_s0=has_s0, sb=sb),
        grid=(nb // sb, npair // npl),
        in_specs=in_specs,
        out_specs=[pl.BlockSpec((sb * t, 2 * w), lambda b, p: (b, p)), st_spec],
        out_shape=[jax.ShapeDtypeStruct((nb * t, C_WIDTH), BF16),
                   jax.ShapeDtypeStruct((nb, npair, 2 * C_DK, C_DV), F32)],
        scratch_shapes=_glr_scratch(t, c, nh, w) if nchunks > 1 else [],
        compiler_params=_cparams(("parallel", "parallel")),
        name="gla",
    )(*args)
    return o, s.reshape(nb, C_HEADS, C_DK, C_DV)


SWA_CB = 8
Q_SCALE = B_SCALE * 1.4426950408889634


def _rope_tables(pos):
    half = B_HD // 2
    inv = ROPE_THETA ** (-jnp.arange(half, dtype=F32) / half)
    ang = pos.astype(F32)[:, None] * inv[None, :]
    cos = jnp.cos(ang)
    sin = jnp.sin(ang)
    cos_t = jnp.tile(cos, (1, 2 * LANES // B_HD))
    sin_t = jnp.tile(jnp.concatenate([-sin, sin], axis=-1), (1, LANES // B_HD))
    return cos_t, sin_t


def _rope(x, cos_t, sin_t):
    reps = x.shape[1] // LANES
    if reps > 1:
        cos_t = jnp.concatenate([cos_t] * reps, axis=1)
        sin_t = jnp.concatenate([sin_t] * reps, axis=1)
    return x * cos_t + _swap_halves(x) * sin_t


def _swap_halves(x):
    n = x.shape[1]
    half = B_HD // 2
    lane = lax.broadcasted_iota(jnp.int32, (1, n), 1)
    first = (lane & (B_HD - 1)) < half
    return jnp.where(first, pltpu.roll(x, n - half, axis=1), pltpu.roll(x, half, axis=1))


def _attend(qs, kb, vb, sink_row, valid):
    return _softmax_pv(_dot_nt(kb, qs), vb, sink_row, valid)


def _softmax_pv(s, vb, sink_row, valid):
    if valid is not None:
        s = jnp.where(valid, s, NEG_BIG)
    sink2 = sink_row * LOG2E
    m = jnp.maximum(jnp.max(s, axis=0, keepdims=True), sink2)
    p = jnp.exp2(s - m)
    den = jnp.sum(p, axis=0, keepdims=True) + jnp.exp2(sink2 - m)
    o_t = _dot_tn(vb, p.astype(BF16)) * (1.0 / den)
    return o_t.T


def _sink_row(sink_ref, base, t):
    lanes = lax.broadcasted_iota(jnp.int32, (1, B_GROUP * t), 1)
    row = jnp.zeros((1, B_GROUP * t), F32)
    for g in range(B_GROUP):
        row = jnp.where((lanes >= g * t) & (lanes < (g + 1) * t), sink_ref[base + g], row)
    return row


def _swa_prompt_kernel(sink_ref, q_ref, k_ref, v_ref, cq_ref, sq_ref, ck_ref, sk_ref,
                       o_ref, kr_ref, vr_ref, kro_ref, vbo_ref, *, t, layer):
    khp = pl.program_id(1)
    step = pl.program_id(2)

    @pl.when(step == 0)
    def _():
        kr = _rope(k_ref[...], ck_ref[...], sk_ref[...])
        kro_ref[...] = kr.astype(BF16)
        vbo_ref[...] = v_ref[...].astype(BF16)
        kr_ref[0] = kr[t - WINDOW:, :]
        vr_ref[0] = v_ref[t - WINDOW:, :]

    band = (WINDOW_CHUNKS + 1) * CHUNK
    sink_rows = [_sink_row(sink_ref, layer * B_HEADS + (khp * 2 + kv) * B_GROUP, CHUNK)
                 for kv in range(2)]
    for ci in range(SWA_CB):
        cidx = step * SWA_CB + ci
        rows = slice(ci * CHUNK, (ci + 1) * CHUNK)
        s0 = pl.multiple_of(jnp.maximum(cidx - WINDOW_CHUNKS, 0) * CHUNK, CHUNK)
        qr = _rope(q_ref[rows, :], cq_ref[rows, :], sq_ref[rows, :]).astype(BF16)
        kband = kro_ref[pl.ds(s0, band), :]
        vband = vbo_ref[pl.ds(s0, band), :]
        key_pos = s0 + lax.broadcasted_iota(jnp.int32, (band, 1), 0)
        valid = key_pos < (cidx + 1) * CHUNK
        for kv in range(2):
            base = kv * B_GROUP * B_HD
            qs = jnp.concatenate(
                [qr[:, base + g * B_HD:base + (g + 1) * B_HD] for g in range(B_GROUP)], axis=0)
            o = _attend(qs, kband[:, kv * B_HD:(kv + 1) * B_HD],
                        vband[:, kv * B_HD:(kv + 1) * B_HD], sink_rows[kv], valid)
            for g in range(0, B_GROUP, 2):
                o_ref[rows, base + g * B_HD:base + (g + 2) * B_HD] = jnp.concatenate(
                    [o[g * CHUNK:(g + 1) * CHUNK], o[(g + 1) * CHUNK:(g + 2) * CHUNK]],
                    axis=1).astype(o_ref.dtype)


def _swa_prompt(proj, sinks, layer, nb, t):
    rb = SWA_CB * CHUNK
    nc = t // rb
    cos_t, sin_t = _rope_tables(jnp.arange(t))
    qw = 2 * B_GROUP * B_HD
    qoff = COL_B_Q * LANES // qw
    kernel = functools.partial(_swa_prompt_kernel, t=t, layer=layer)
    return pl.pallas_call(
        kernel,
        grid=(nb, B_KV_HEADS // 2, nc),
        in_specs=[pl.BlockSpec(memory_space=pltpu.SMEM),
                  pl.BlockSpec((rb, qw), lambda b, p, c: (b * nc + c, qoff + p)),
                  pl.BlockSpec((t, LANES), lambda b, p, c: (b, COL_B_K + p)),
                  pl.BlockSpec((t, LANES), lambda b, p, c: (b, COL_B_V + p)),
                  pl.BlockSpec((rb, LANES), lambda b, p, c: (c, 0)),
                  pl.BlockSpec((rb, LANES), lambda b, p, c: (c, 0)),
                  pl.BlockSpec((t, LANES), lambda b, p, c: (0, 0)),
                  pl.BlockSpec((t, LANES), lambda b, p, c: (0, 0))],
        out_specs=[pl.BlockSpec((rb, qw), lambda b, p, c: (b * nc + c, p)),
                   pl.BlockSpec((1, WINDOW, LANES), lambda b, p, c: (b, 0, p)),
                   pl.BlockSpec((1, WINDOW, LANES), lambda b, p, c: (b, 0, p))],
        out_shape=[jax.ShapeDtypeStruct((nb * t, B_WIDTH), BF16),
                   jax.ShapeDtypeStruct((nb, WINDOW, B_KV_HEADS * B_HD), F32),
                   jax.ShapeDtypeStruct((nb, WINDOW, B_KV_HEADS * B_HD), F32)],
        scratch_shapes=[pltpu.VMEM((t, LANES), BF16), pltpu.VMEM((t, LANES), BF16)],
        compiler_params=_cparams(("parallel", "parallel", "arbitrary")),
        name="swa_prompt",
    )(sinks.reshape(-1), proj, proj, proj, cos_t * Q_SCALE, sin_t * Q_SCALE, cos_t, sin_t)


def _swa_sample_kernel(sink_ref, q_ref, k_ref, v_ref, ckc_ref, cvc_ref, cos_ref, sin_ref,
                       o_ref, kr_ref, vr_ref, *, t, layer):
    cos_t = cos_ref[...]
    sin_t = sin_ref[...]
    kr = _rope(k_ref[...], cos_t, sin_t)
    v = v_ref[...]
    kr_ref[...] = kr
    vr_ref[...] = v
    qr = (_rope(q_ref[...], cos_t, sin_t) * Q_SCALE).astype(BF16)
    ka = jnp.concatenate([ckc_ref[0], kr], axis=0).astype(BF16)
    va = jnp.concatenate([cvc_ref[0], v], axis=0).astype(BF16)
    for kh in range(B_KV_HEADS):
        base = kh * B_GROUP * B_HD
        qs = jnp.concatenate(
            [qr[:, base + g * B_HD:base + (g + 1) * B_HD] for g in range(B_GROUP)], axis=0)
        sink_row = _sink_row(sink_ref, layer * B_HEADS + kh * B_GROUP, t)
        o = _attend(qs, ka[:, kh * B_HD:(kh + 1) * B_HD], va[:, kh * B_HD:(kh + 1) * B_HD],
                    sink_row, None)
        for g in range(0, B_GROUP, 2):
            o_ref[:, base + g * B_HD:base + (g + 2) * B_HD] = jnp.concatenate(
                [o[g * t:(g + 1) * t], o[(g + 1) * t:(g + 2) * t]], axis=1).astype(o_ref.dtype)


def _swa_sample(proj, cache_k, cache_v, sinks, layer, nb, t):
    cos_t, sin_t = _rope_tables(PAST_LEN + jnp.arange(t))
    kvw = B_KV_HEADS * B_HD
    kernel = functools.partial(_swa_sample_kernel, t=t, layer=layer)
    return pl.pallas_call(
        kernel,
        grid=(nb,),
        in_specs=[pl.BlockSpec(memory_space=pltpu.SMEM),
                  pl.BlockSpec((t, B_WIDTH), lambda b: (b, COL_B_Q * LANES // B_WIDTH)),
                  pl.BlockSpec((t, kvw), lambda b: (b, COL_B_K * LANES // kvw)),
                  pl.BlockSpec((t, kvw), lambda b: (b, COL_B_V * LANES // kvw)),
                  pl.BlockSpec((1, WINDOW, kvw), lambda b: (b, 0, 0)),
                  pl.BlockSpec((1, WINDOW, kvw), lambda b: (b, 0, 0)),
                  pl.BlockSpec((t, LANES), lambda b: (0, 0)),
                  pl.BlockSpec((t, LANES), lambda b: (0, 0))],
        out_specs=[pl.BlockSpec((t, B_WIDTH), lambda b: (b, 0)),
                   pl.BlockSpec((t, kvw), lambda b: (b, 0)),
                   pl.BlockSpec((t, kvw), lambda b: (b, 0))],
        out_shape=[jax.ShapeDtypeStruct((nb * t, B_WIDTH), BF16),
                   jax.ShapeDtypeStruct((nb * t, kvw), F32),
                   jax.ShapeDtypeStruct((nb * t, kvw), F32)],
        compiler_params=_cparams(("parallel",)),
        name="swa_sample",
    )(sinks.reshape(-1), proj, proj, proj,
      cache_k.reshape(nb, WINDOW, kvw), cache_v.reshape(nb, WINDOW, kvw), cos_t, sin_t)


def _layer_sample(x, w, layer, nb, t, cache, s_a, s_c, ff_pad):
    m, d = x.shape
    h = _rmsnorm(x, w["norm_mix"][layer], BF16, m)
    proj, wb_in = _mm_in_cast(h, w["w_in"], layer, IN_COLS_PAD, MM_TN_IN)
    o_a, s_a_new = _hgrn(proj, w["lb_logits"], w["hgrn_norm"][layer], s_a, layer, nb, t, t)
    o_b, k_rows, v_rows = _swa_sample(proj, cache[0], cache[1], w["sinks"], layer, nb, t)
    o_c, s_c_new = _gla(proj, w["w_a2"][layer], w["b_a"][layer], w["gla_norm"][layer],
                        s_c, nb, t, t)
    x, wb_out = _mm_out_cast(o_a, o_b, o_c, w["w_out"], x, layer, min(d, 512))
    h = _rmsnorm(x, w["norm_ffn"][layer], BF16, m)
    mid, wb_gate, wb_up = _gate_up_cast(h, w["w_gate_up"], layer, ff_pad)
    x, wb_down = _down_cast(mid, w["w_down"], x, layer)
    outs = (k_rows.reshape(nb, t, B_KV_HEADS, B_HD), v_rows.reshape(nb, t, B_KV_HEADS, B_HD),
            s_a_new, s_c_new)
    return x, outs, dict(w_in=wb_in, w_out=wb_out, w_gate=wb_gate, w_up=wb_up, w_down=wb_down)


def _layer_prompt(x, normed, w, wb, layer, nb, t):
    m, d = x.shape
    tm = min(m, 1024)
    if normed is None:
        h = _rmsnorm(x, w["norm_mix"][layer], BF16, min(m, 256))
        proj = _matmul(h, wb["w_in"], tm, MM_TN_IN, F32)
    else:
        proj = _matmul(normed[0], wb["w_in"], tm, MM_TN_IN, F32, normed[1])
    o_a, s_a_new = _hgrn(proj, w["lb_logits"], w["hgrn_norm"][layer], None, layer, nb, t, CHUNK)
    o_b, k_rows, v_rows = _swa_prompt(proj, w["sinks"], layer, nb, t)
    o_c, s_c_new = _gla(proj, w["w_a2"][layer], w["b_a"][layer], w["gla_norm"][layer],
                        None, nb, t, CHUNK)
    x, xg, ssq = _matmul_out(o_a, o_b, o_c, wb["w_out"], x, w["norm_ffn"][layer], tm, min(d, 512))
    mid = _matmul_gate_up(xg, ssq, wb["w_gate"], wb["w_up"], tm, 512)
    tk = wb["w_down"].shape[0] // 4
    if layer + 1 < DEPTH:
        x, xg, ssq = _matmul_down(mid, wb["w_down"], x, tm, min(d, 1024), tk,
                                  w["norm_mix"][layer + 1])
        normed = (xg, ssq)
    else:
        x = _matmul_down(mid, wb["w_down"], x, tm, min(d, 1024), tk)
        normed = None
    outs = (k_rows.reshape(nb, WINDOW, B_KV_HEADS, B_HD),
            v_rows.reshape(nb, WINDOW, B_KV_HEADS, B_HD), s_a_new, s_c_new)
    return x, normed, outs


def kernel(x_prompt, x_sample, cache_k_swa, cache_v_swa, state_hgrn, state_gla, norm_mix, w_in,
           hgrn_lb_logits, hgrn_norm, swa_sinks, gla_w_alpha2, gla_b_alpha, gla_norm, w_out,
           norm_ffn, w_gate_up, w_down, norm_final):
    n_p, t_p, d = x_prompt.shape
    n_s, t_s, _ = x_sample.shape
    d_ff = w_down.shape[1]
    ff_pad = -(-d_ff // 1024) * 1024
    w = dict(
        norm_mix=norm_mix, norm_ffn=norm_ffn, hgrn_norm=hgrn_norm, gla_norm=gla_norm,
        lb_logits=hgrn_lb_logits, sinks=swa_sinks, b_a=gla_b_alpha,
        w_in=w_in, w_out=w_out, w_gate_up=w_gate_up, w_down=w_down,
        w_a2=jnp.pad(gla_w_alpha2, ((0, 0), (0, LANES - GLA_RANK), (0, 0))).astype(BF16),
    )
    xp = x_prompt.reshape(n_p * t_p, d)
    xs = x_sample.reshape(n_s * t_s, d)
    outs_p, outs_s = [], []
    normed = None
    for layer in range(DEPTH):
        xs, rest, wb = _layer_sample(xs, w, layer, n_s, t_s,
                                     (cache_k_swa[layer], cache_v_swa[layer]),
                                     state_hgrn[layer], state_gla[layer], ff_pad)
        outs_s.append(rest)
        xp, normed, rest = _layer_prompt(xp, normed, w, wb, layer, n_p, t_p)
        outs_p.append(rest)
    y_p = _rmsnorm(xp, norm_final, F32, min(xp.shape[0], 256)).reshape(n_p, t_p, d)
    y_s = _rmsnorm(xs, norm_final, F32, min(xs.shape[0], 256)).reshape(n_s, t_s, d)
    stack = lambda outs, i: jnp.stack([o[i] for o in outs])
    return (y_p, y_s,
            stack(outs_p, 0), stack(outs_p, 1), stack(outs_p, 2), stack(outs_p, 3),
            stack(outs_s, 0), stack(outs_s, 1), stack(outs_s, 2), stack(outs_s, 3))
```

```python
import functools

import jax
import jax.numpy as jnp
import numpy as np
from jax import lax
from jax.experimental import pallas as pl
from jax.experimental.pallas import tpu as pltpu

F32 = jnp.float32
BF16 = jnp.bfloat16

DEPTH = 2
PAST_LEN = 4096
CHUNK = 64
EPS = 1e-6
NEG_BIG = -1e30
LB_FLOOR = 1e-30
A_HEADS = 8
A_DK = 128
A_DV = 128
A_WIDTH = A_HEADS * A_DV
B_HEADS = 32
B_KV_HEADS = 4
B_GROUP = B_HEADS // B_KV_HEADS
B_HD = 64
B_WIDTH = B_HEADS * B_HD
B_SCALE = B_HD ** -0.5
WINDOW = 128
WINDOW_CHUNKS = WINDOW // CHUNK
ROPE_THETA = 10000.0
C_HEADS = 8
C_DK = 64
C_DV = 128
C_WIDTH = C_HEADS * C_DV
GLA_RANK = 16
GLA_NORMALIZER = 16.0

LANES = 128

COL_A_Q = 0
COL_A_F = 8
COL_A_I = 16
COL_A_G = 24
COL_B_Q = 32
COL_B_K = 48
COL_B_V = 50
COL_C_Q = 52
COL_C_K = 56
COL_C_V = 60
COL_C_R = 68
COL_C_A = 76
IN_COLS = 9744
IN_COLS_PAD = 9984
MM_TN_IN = 768
VMEM_LIMIT = 56 * 1024 * 1024


def _cparams(sem):
    return pltpu.CompilerParams(dimension_semantics=sem, vmem_limit_bytes=VMEM_LIMIT)


def _dot(a, b):
    return jnp.dot(a, b, preferred_element_type=F32)


def _dot_nt(a, b):
    return lax.dot_general(a, b, (((1,), (1,)), ((), ())), preferred_element_type=F32)


def _dot_tn(a, b):
    return lax.dot_general(a, b, (((0,), (0,)), ((), ())), preferred_element_type=F32)


def _rmsnorm_kernel(x_ref, g_ref, o_ref):
    x = x_ref[...]
    var = jnp.mean(x * x, axis=-1, keepdims=True)
    o_ref[...] = (x * lax.rsqrt(var + EPS) * g_ref[...]).astype(o_ref.dtype)


def _rmsnorm(x, g, out_dtype, tm):
    m, d = x.shape
    return pl.pallas_call(
        _rmsnorm_kernel,
        grid=(m // tm,),
        in_specs=[pl.BlockSpec((tm, d), lambda i: (i, 0)),
                  pl.BlockSpec((1, d), lambda i: (0, 0))],
        out_specs=pl.BlockSpec((tm, d), lambda i: (i, 0)),
        out_shape=jax.ShapeDtypeStruct((m, d), out_dtype),
        compiler_params=_cparams(("parallel",)),
        name="rmsnorm",
    )(x, g.reshape(1, d))


def _row_scale(ssq_ref, d):
    return lax.rsqrt(jnp.sum(ssq_ref[...], axis=-1, keepdims=True) * (1.0 / d) + EPS)


def _lane_partial_sq(x):
    sq = x * x
    part = sq[:, 0:LANES]
    for c in range(1, x.shape[1] // LANES):
        part = part + sq[:, c * LANES:(c + 1) * LANES]
    return part


def _accumulate(ref, part, first):
    @pl.when(first)
    def _():
        ref[...] = part

    @pl.when(jnp.logical_not(first))
    def _():
        ref[...] += part


def _emit_normed(x, g_ref, xg_ref, ssq_ref, first):
    xg_ref[...] = (x * g_ref[...]).astype(xg_ref.dtype)
    _accumulate(ssq_ref, _lane_partial_sq(x), first)


def _mm_kernel(x_ref, w_ref, o_ref):
    o_ref[...] = _dot(x_ref[...], w_ref[...]).astype(o_ref.dtype)


def _mm_scaled_kernel(x_ref, s_ref, w_ref, o_ref):
    r = _row_scale(s_ref, x_ref.shape[1])
    o_ref[...] = (_dot(x_ref[...], w_ref[...]) * r).astype(o_ref.dtype)


def _matmul(x, w, tm, tn, out_dtype, ssq=None):
    m, k = x.shape
    n = w.shape[1]
    x_spec = pl.BlockSpec((tm, k), lambda i, j: (i, 0))
    w_spec = pl.BlockSpec((k, tn), lambda i, j: (0, j))
    s_spec = pl.BlockSpec((tm, LANES), lambda i, j: (i, 0))
    return pl.pallas_call(
        _mm_kernel if ssq is None else _mm_scaled_kernel,
        grid=(m // tm, n // tn),
        in_specs=[x_spec, w_spec] if ssq is None else [x_spec, s_spec, w_spec],
        out_specs=pl.BlockSpec((tm, tn), lambda i, j: (i, j)),
        out_shape=jax.ShapeDtypeStruct((m, n), out_dtype),
        compiler_params=_cparams(("parallel", "arbitrary")),
        name="mm_in",
    )(*((x, w) if ssq is None else (x, ssq, w)))


def _mm_out_kernel(oa_ref, ob_ref, oc_ref, w_ref, r_ref, g_ref, o_ref, xg_ref, ssq_ref):
    j = pl.program_id(1)
    x = (r_ref[...] + _dot(oa_ref[...], w_ref[0:A_WIDTH, :])
         + _dot(ob_ref[...], w_ref[A_WIDTH:A_WIDTH + B_WIDTH, :])
         + _dot(oc_ref[...], w_ref[A_WIDTH + B_WIDTH:, :]))
    o_ref[...] = x
    _emit_normed(x, g_ref, xg_ref, ssq_ref, j == 0)


def _matmul_out(oa, ob, oc, w, resid, gain, tm, tn):
    m = oa.shape[0]
    k, n = w.shape
    tile = pl.BlockSpec((tm, tn), lambda i, j: (i, j))
    return pl.pallas_call(
        _mm_out_kernel,
        grid=(m // tm, n // tn),
        in_specs=[pl.BlockSpec((tm, A_WIDTH), lambda i, j: (i, 0)),
                  pl.BlockSpec((tm, B_WIDTH), lambda i, j: (i, 0)),
                  pl.BlockSpec((tm, C_WIDTH), lambda i, j: (i, 0)),
                  pl.BlockSpec((k, tn), lambda i, j: (0, j)),
                  tile,
                  pl.BlockSpec((1, tn), lambda i, j: (0, j))],
        out_specs=[tile, tile, pl.BlockSpec((tm, LANES), lambda i, j: (i, 0))],
        out_shape=[jax.ShapeDtypeStruct((m, n), F32), jax.ShapeDtypeStruct((m, n), BF16),
                   jax.ShapeDtypeStruct((m, LANES), F32)],
        compiler_params=_cparams(("parallel", "arbitrary")),
        name="mm_out",
    )(oa, ob, oc, w, resid, gain.reshape(1, n))


def _mm_gate_up_kernel(x_ref, s_ref, wg_ref, wu_ref, o_ref):
    x = x_ref[...]
    r = _row_scale(s_ref, x.shape[1])
    gate = _dot(x, wg_ref[...]) * r
    up = _dot(x, wu_ref[...]) * r
    o_ref[...] = (gate * (1.0 / (1.0 + jnp.exp(-gate))) * up).astype(o_ref.dtype)


def _matmul_gate_up(x, ssq, wg, wu, tm, tn):
    m, k = x.shape
    n = wg.shape[1]
    return pl.pallas_call(
        _mm_gate_up_kernel,
        grid=(m // tm, n // tn),
        in_specs=[pl.BlockSpec((tm, k), lambda i, j: (i, 0)),
                  pl.BlockSpec((tm, LANES), lambda i, j: (i, 0)),
                  pl.BlockSpec((k, tn), lambda i, j: (0, j)),
                  pl.BlockSpec((k, tn), lambda i, j: (0, j))],
        out_specs=pl.BlockSpec((tm, tn), lambda i, j: (i, j)),
        out_shape=jax.ShapeDtypeStruct((m, n), BF16),
        compiler_params=_cparams(("parallel", "arbitrary")),
        name="mm_gate_up",
    )(x, ssq, wg, wu)


def _mm_down_kernel(x_ref, w_ref, r_ref, *rest, nk, emit):
    kk = pl.program_id(2)
    o_ref = rest[1] if emit else rest[0]

    @pl.when(kk == 0)
    def _():
        o_ref[...] = r_ref[...] + _dot(x_ref[...], w_ref[...])

    if not emit:
        @pl.when(kk != 0)
        def _():
            o_ref[...] += _dot(x_ref[...], w_ref[...])
    else:
        assert nk > 1
        g_ref, _, xg_ref, ssq_ref = rest

        @pl.when((kk != 0) & (kk != nk - 1))
        def _():
            o_ref[...] += _dot(x_ref[...], w_ref[...])

        @pl.when(kk == nk - 1)
        def _():
            x = o_ref[...] + _dot(x_ref[...], w_ref[...])
            o_ref[...] = x
            _emit_normed(x, g_ref, xg_ref, ssq_ref, pl.program_id(1) == 0)


def _matmul_down(x, w, resid, tm, tn, tk, gain=None):
    m, k = x.shape
    n = w.shape[1]
    emit = gain is not None
    tile = pl.BlockSpec((tm, tn), lambda i, j, kk: (i, j))
    in_specs = [pl.BlockSpec((tm, tk), lambda i, j, kk: (i, kk)),
                pl.BlockSpec((tk, tn), lambda i, j, kk: (kk, j)),
                tile]
    out_specs, out_shape, args = [tile], [jax.ShapeDtypeStruct((m, n), F32)], [x, w, resid]
    if emit:
        in_specs.append(pl.BlockSpec((1, tn), lambda i, j, kk: (0, j)))
        args.append(gain.reshape(1, n))
        out_specs += [tile, pl.BlockSpec((tm, LANES), lambda i, j, kk: (i, 0))]
        out_shape += [jax.ShapeDtypeStruct((m, n), BF16), jax.ShapeDtypeStruct((m, LANES), F32)]
    out = pl.pallas_call(
        functools.partial(_mm_down_kernel, nk=k // tk, emit=emit),
        grid=(m // tm, n // tn, k // tk),
        in_specs=in_specs,
        out_specs=out_specs,
        out_shape=out_shape,
        compiler_params=_cparams(("parallel", "arbitrary", "arbitrary")),
        name="mm_down",
    )(*args)
    return out if emit else out[0]


CAST_TN_GATE = 256
CAST_TK_DOWN = 512


def _masked_bf16(w, first, n_valid, axis):
    idx = first + lax.broadcasted_iota(jnp.int32, w.shape, axis)
    return jnp.where(idx < n_valid, w, 0.0).astype(BF16)


def _mm_in_cast_kernel(x_ref, w_ref, o_ref, wb_ref, *, n_valid, tn):
    idx = pl.program_id(0) * tn + lax.broadcasted_iota(jnp.int32, w_ref.shape, 0)
    wb = jnp.where(idx < n_valid, w_ref[...], 0.0).T.astype(BF16)
    wb_ref[...] = wb
    o_ref[...] = _dot(x_ref[...], wb)


def _mm_in_cast(x, w3, layer, n_pad, tn):
    m, k = x.shape
    n = w3.shape[2]
    return pl.pallas_call(
        functools.partial(_mm_in_cast_kernel, n_valid=n, tn=tn),
        grid=(n_pad // tn,),
        in_specs=[pl.BlockSpec((m, k), lambda j: (0, 0)),
                  pl.BlockSpec((None, tn, k), lambda j: (layer, j, 0))],
        out_specs=[pl.BlockSpec((m, tn), lambda j: (0, j)),
                   pl.BlockSpec((k, tn), lambda j: (0, j))],
        out_shape=[jax.ShapeDtypeStruct((m, n_pad), F32),
                   jax.ShapeDtypeStruct((k, n_pad), BF16)],
        compiler_params=_cparams(("arbitrary",)),
        name="mm_in_cast",
    )(x, jnp.swapaxes(w3, 1, 2))


def _mm_out_cast_kernel(oa_ref, ob_ref, oc_ref, w_ref, r_ref, o_ref, wb_ref, cat_ref):
    @pl.when(pl.program_id(0) == 0)
    def _():
        cat_ref[:, 0:A_WIDTH] = oa_ref[...]
        cat_ref[:, A_WIDTH:A_WIDTH + B_WIDTH] = ob_ref[...]
        cat_ref[:, A_WIDTH + B_WIDTH:] = oc_ref[...]

    wb = w_ref[...].astype(BF16)
    wb_ref[...] = wb
    o_ref[...] = r_ref[...] + _dot(cat_ref[...], wb)


def _mm_out_cast(oa, ob, oc, w3, resid, layer, tn):
    m = oa.shape[0]
    k, n = w3.shape[1:]
    return pl.pallas_call(
        _mm_out_cast_kernel,
        grid=(n // tn,),
        in_specs=[pl.BlockSpec((m, A_WIDTH), lambda j: (0, 0)),
                  pl.BlockSpec((m, B_WIDTH), lambda j: (0, 0)),
                  pl.BlockSpec((m, C_WIDTH), lambda j: (0, 0)),
                  pl.BlockSpec((None, k, tn), lambda j: (layer, 0, j)),
                  pl.BlockSpec((m, tn), lambda j: (0, j))],
        out_specs=[pl.BlockSpec((m, tn), lambda j: (0, j)),
                   pl.BlockSpec((k, tn), lambda j: (0, j))],
        out_shape=[jax.ShapeDtypeStruct((m, n), F32),
                   jax.ShapeDtypeStruct((k, n), BF16)],
        scratch_shapes=[pltpu.VMEM((m, k), BF16)],
        compiler_params=_cparams(("arbitrary",)),
        name="mm_out_cast",
    )(oa, ob, oc, w3, resid)


def _gate_up_cast_kernel(x_ref, wg_ref, wu_ref, o_ref, wgb_ref, wub_ref, *, n_tiles):
    valid = pl.program_id(0) < n_tiles
    wg = jnp.where(valid, wg_ref[...], 0.0).astype(BF16)
    wu = jnp.where(valid, wu_ref[...], 0.0).astype(BF16)
    wgb_ref[...] = wg
    wub_ref[...] = wu
    x = x_ref[...]
    gate = _dot(x, wg)
    o_ref[...] = (gate * (1.0 / (1.0 + jnp.exp(-gate))) * _dot(x, wu)).astype(o_ref.dtype)


def _gate_up_cast(x, wgu3, layer, ff_pad):
    m, k = x.shape
    tn = CAST_TN_GATE
    n_tiles = wgu3.shape[2] // 2 // tn
    assert n_tiles * tn * 2 == wgu3.shape[2]
    return pl.pallas_call(
        functools.partial(_gate_up_cast_kernel, n_tiles=n_tiles),
        grid=(ff_pad // tn,),
        in_specs=[pl.BlockSpec((m, k), lambda j: (0, 0)),
                  pl.BlockSpec((None, k, tn), lambda j: (layer, 0, jnp.minimum(j, n_tiles - 1))),
                  pl.BlockSpec((None, k, tn),
                               lambda j: (layer, 0, n_tiles + jnp.minimum(j, n_tiles - 1)))],
        out_specs=[pl.BlockSpec((m, tn), lambda j: (0, j)),
                   pl.BlockSpec((k, tn), lambda j: (0, j)),
                   pl.BlockSpec((k, tn), lambda j: (0, j))],
        out_shape=[jax.ShapeDtypeStruct((m, ff_pad), BF16),
                   jax.ShapeDtypeStruct((k, ff_pad), BF16),
                   jax.ShapeDtypeStruct((k, ff_pad), BF16)],
        compiler_params=_cparams(("arbitrary",)),
        name="gate_up_cast",
    )(x, wgu3, wgu3)


def _down_cast_kernel(x_ref, w_ref, r_ref, o_ref, wb_ref, *, k_valid, tk):
    kk = pl.program_id(0)
    wb = _masked_bf16(w_ref[...], kk * tk, k_valid, 0)
    wb_ref[...] = wb

    @pl.when(kk == 0)
    def _():
        o_ref[...] = r_ref[...] + _dot(x_ref[...], wb)

    @pl.when(kk != 0)
    def _():
        o_ref[...] += _dot(x_ref[...], wb)


def _down_cast(x, wd3, resid, layer):
    m, ff_pad = x.shape
    ff, d = wd3.shape[1:]
    tk = CAST_TK_DOWN
    last = (ff - 1) // tk
    return pl.pallas_call(
        functools.partial(_down_cast_kernel, k_valid=ff, tk=tk),
        grid=(ff_pad // tk,),
        in_specs=[pl.BlockSpec((m, tk), lambda kk: (0, kk)),
                  pl.BlockSpec((None, tk, d), lambda kk: (layer, jnp.minimum(kk, last), 0)),
                  pl.BlockSpec((m, d), lambda kk: (0, 0))],
        out_specs=[pl.BlockSpec((m, d), lambda kk: (0, 0)),
                   pl.BlockSpec((tk, d), lambda kk: (kk, 0))],
        out_shape=[jax.ShapeDtypeStruct((m, d), F32),
                   jax.ShapeDtypeStruct((ff_pad, d), BF16)],
        compiler_params=_cparams(("arbitrary",)),
        name="down_cast",
    )(x, wd3, resid)


LOG2E = 1.4426950408889634
SUBLANES = 8
HGRN_W = 256
GLA_W = 256
GLR_UNROLL = 8
GLR_UNROLL_STATE = 32


def _levels(c):
    out, b = [], 1
    while b < c:
        out.append(b)
        b *= 2
    return out


def _prefix_matrix(c):
    t = np.arange(c)[:, None]
    s = np.arange(c)[None, :]
    low = s <= t
    mats = [low]
    for b in _levels(c):
        if b >= SUBLANES:
            break
        same = (t // b) == (s // b)
        odd = ((t // b) % 2) == 1
        mats.append(np.where(odd, low & same, (~low) & same))
    m = np.concatenate(mats, axis=0).astype(np.float32)
    return jnp.asarray(np.concatenate([m, m, m], axis=1), dtype=BF16)


def _level_sums(g, b):
    parts = []
    for j in range(g.shape[0] // b):
        blk = g[j * b:(j + 1) * b]
        if j % 2 == 1:
            parts.append(blk - g[j * b - 1:j * b])
        else:
            parts.append(g[(j + 1) * b - 1:(j + 1) * b] - blk)
    return jnp.concatenate(parts, axis=0)


def _diag_matrix(c, nh, w):
    kd = w // nh
    lane = np.arange(w)[:, None]
    col = np.arange(nh * c)[None, :]
    return jnp.asarray(((col // c) == (lane // kd)).astype(np.float32), dtype=BF16)


def _glr_consts(c, nh, w):
    t = lax.broadcasted_iota(jnp.int32, (c, nh * c), 0)
    s = lax.broadcasted_iota(jnp.int32, (c, nh * c), 1) & (c - 1)
    lvl_masks = []
    for b in _levels(c):
        lg = b.bit_length() - 1
        tb = lax.shift_right_logical(t, lg)
        sb = lax.shift_right_logical(s, lg)
        lvl_masks.append((lax.shift_right_logical(tb, 1) == lax.shift_right_logical(sb, 1))
                         & ((tb & 1) == 1) & ((sb & 1) == 0))
    lgk = (w // nh).bit_length() - 1
    lane = lax.shift_right_logical(lax.broadcasted_iota(jnp.int32, (1, w), 1), lgk)
    srow = lax.broadcasted_iota(jnp.int32, (nh * LANES, w), 0) // LANES
    scol = lax.shift_right_logical(lax.broadcasted_iota(jnp.int32, (nh * LANES, w), 1), lgk)
    return dict(lvl_masks=lvl_masks, diag_valid=t == s,
                head_lanes=[lane == h for h in range(nh)], st_mask=srow == scol)


def _head_rows(x, cst, nh):
    if nh * LANES == x.shape[1]:
        z = jnp.zeros((x.shape[0], LANES), x.dtype)
        return jnp.concatenate(
            [jnp.concatenate([x[:, h * LANES:(h + 1) * LANES] if g == h else z
                              for g in range(nh)], axis=1) for h in range(nh)], axis=0)
    zero = jnp.zeros_like(x)
    return jnp.concatenate([jnp.where(cst["head_lanes"][h], x, zero) for h in range(nh)], axis=0)


def _glr_prefix(lf2, pm):
    hi = lf2.astype(BF16)
    r1 = lf2 - hi.astype(F32)
    mid = r1.astype(BF16)
    lo = (r1 - mid.astype(F32)).astype(BF16)
    return _dot(pm, jnp.concatenate([hi, mid, lo], axis=0))


def _glr_intra(qs, ks, lf2s, pm, rd, cst, *, c, nh):
    n = len(qs)
    prefs = [_glr_prefix(lf2, pm) for lf2 in lf2s]
    gs = [p[0:c] for p in prefs]
    qhat = [(qs[j] * jnp.exp2(gs[j])).astype(BF16) for j in range(n)]
    kdec = [(ks[j] * jnp.exp2(gs[j][c - 1:c, :] - gs[j])).astype(BF16) for j in range(n)]
    a = [jnp.where(cst["diag_valid"], _dot((qs[j] * ks[j]).astype(BF16), rd), 0.0)
         for j in range(n)]
    for i, b in enumerate(_levels(c)):
        if b < SUBLANES:
            ebs = [jnp.exp2(prefs[j][(1 + i) * c:(2 + i) * c]) for j in range(n)]
        else:
            ebs = [jnp.exp2(_level_sums(gs[j], b)) for j in range(n)]
        dots = [_dot_nt((qs[j] * ebs[j]).astype(BF16),
                        _head_rows((ks[j] * ebs[j]).astype(BF16), cst, nh)) for j in range(n)]
        a = [jnp.where(cst["lvl_masks"][i], dots[j], a[j]) for j in range(n)]
    return [(a[j].astype(BF16), qhat[j], kdec[j], jnp.exp2(gs[j][c - 1:c, :]))
            for j in range(n)]


def _glr_in_chunk(a, vs, nh):
    vb = [v.astype(BF16) for v in vs]
    z = jnp.zeros_like(vb[0])
    vbd = jnp.concatenate(
        [jnp.concatenate([vb[h] if g == h else z for g in range(nh)], axis=1) for h in range(nh)],
        axis=0)
    return _dot(a, vbd)


def _glr_state(chunks, st, cst, *, nh, streams=False):
    o_in = [ch[0] for ch in chunks]
    upd = [jnp.where(cst["st_mask"],
                     _dot_tn(jnp.concatenate([v.astype(BF16) for v in ch[4]], axis=1), ch[2]), 0.0)
           for ch in chunks]
    if streams:
        outs = [o_in[j] + _dot_nt(ch[1], st[j].astype(BF16)) for j, ch in enumerate(chunks)]
        return outs, [st[j] * ch[3] + upd[j] for j, ch in enumerate(chunks)]
    outs = []
    for j, (_, qhat, _, dec, _) in enumerate(chunks):
        outs.append(o_in[j] + _dot_nt(qhat, st.astype(BF16)))
        st = st * dec + upd[j]
    return outs, st


def _gated_norm(o, norm, gate):
    var = jnp.mean(o * o, axis=-1, keepdims=True)
    return o * lax.rsqrt(var + EPS) * norm * (gate * (1.0 / (1.0 + jnp.exp(-gate))))


def _glr_scratch(t, c, nh, w):
    return [pltpu.VMEM((t, nh * LANES), F32), pltpu.VMEM((t, w), BF16),
            pltpu.VMEM((t, w), BF16), pltpu.VMEM((t // c, SUBLANES, w), F32)]


def _glr_run_streams(load_qkl, load_vg, store_o, st0s, pm, rd, cst, *, c, nh):
    rows = [slice(b * c, (b + 1) * c) for b in range(len(st0s))]
    ins = [load_qkl(r) for r in rows]
    res = _glr_intra([x[0] for x in ins], [x[1] for x in ins], [x[2] for x in ins],
                     pm, rd, cst, c=c, nh=nh)
    vg = [load_vg(r) for r in rows]
    chunks = [(_glr_in_chunk(a, x[0], nh), qh, kd, dec, x[0])
              for (a, qh, kd, dec), x in zip(res, vg)]
    outs, sts = _glr_state(chunks, st0s, cst, nh=nh, streams=True)
    for r, o, x in zip(rows, outs, vg):
        store_o(r, o, x[1])
    return sts


def _glr_run(load_qkl, load_vg, store_o, st0, pm, rd, cst, scratch, *, c, nchunks, nh, w):
    a_s, qh_s, kd_s, dec_s = scratch
    u = min(GLR_UNROLL, nchunks)
    assert nchunks % u == 0

    def rows_of(n):
        return pl.ds(pl.multiple_of(n * c, c), c)

    def intra(i, carry):
        ns = [i * u + j for j in range(u)]
        ins = [load_qkl(rows_of(n)) for n in ns]
        res = _glr_intra([x[0] for x in ins], [x[1] for x in ins], [x[2] for x in ins],
                         pm, rd, cst, c=c, nh=nh)
        oin = [_glr_in_chunk(r4[0], load_vg(rows_of(n))[0], nh) for n, r4 in zip(ns, res)]
        for n, (_, qh, kd, dec), o in zip(ns, res, oin):
            rows = rows_of(n)
            a_s[rows, :] = o
            qh_s[rows, :] = qh
            kd_s[rows, :] = kd
            dec_s[n] = jnp.broadcast_to(dec, (SUBLANES, w))
        return carry

    lax.fori_loop(0, nchunks // u, intra, 0)

    us = min(GLR_UNROLL_STATE, nchunks)
    assert nchunks % us == 0

    def state(i, st):
        ns = [i * us + j for j in range(us)]
        ins = [(a_s[rows_of(n), :], qh_s[rows_of(n), :], kd_s[rows_of(n), :], dec_s[n, 0:1, :])
               + tuple(load_vg(rows_of(n))) for n in ns]
        outs, st = _glr_state([x[:5] for x in ins], st, cst, nh=nh)
        for n, o, x in zip(ns, outs, ins):
            store_o(rows_of(n), o, x[5])
        return st

    return lax.fori_loop(0, nchunks // us, state, st0)


def _hgrn_kernel(*refs, c, nchunks, layer, has_s0, sb):
    pm_ref, rd_ref, lbl_ref, norm_ref, q_ref, z_ref, v_ref, g_ref = refs[:8]
    s0_ref = refs[8] if has_s0 else None
    o_ref, s_ref = refs[8 + has_s0:10 + has_s0]
    scratch = refs[10 + has_s0:]
    w = HGRN_W
    nh = w // A_DK
    logits = lbl_ref[...]
    e = jnp.exp(logits - jnp.max(logits, axis=0, keepdims=True))
    probs = e / jnp.sum(e, axis=0, keepdims=True)
    lb = jnp.sum(probs[0:layer + 1], axis=0, keepdims=True) - probs[0:1]
    lb_floor = jnp.maximum(lb, LB_FLOOR)
    oml = 1.0 - lb
    norm = norm_ref[...]
    pm = pm_ref[...]
    rd = rd_ref[...]
    cst = _glr_consts(c, nh, w)
    heads = [slice(h * LANES, (h + 1) * LANES) for h in range(nh)]

    def load_qkl(rows):
        q = q_ref[rows, :]
        z = z_ref[rows, :]
        ez = jnp.exp(-jnp.abs(z))
        r = 1.0 / (1.0 + ez)
        pos = z >= 0.0
        lf2 = jnp.log2(lb_floor + oml * jnp.where(pos, r, ez * r))
        return q, oml * jnp.where(pos, ez * r, r), lf2

    def load_vg(rows):
        v = v_ref[rows, :]
        return [v[:, sl] for sl in heads], g_ref[rows, :]

    def store_o(rows, o, g):
        o_ref[rows, :] = jnp.concatenate(
            [_gated_norm(o[:, sl], norm, g[:, sl]) for sl in heads], axis=1).astype(o_ref.dtype)

    zero = jnp.zeros((A_DV, A_DK), F32)

    def init_state(b):
        if not has_s0:
            return jnp.zeros((nh * A_DV, w), F32)
        return jnp.concatenate(
            [jnp.concatenate([s0_ref[b, h].T if g == h else zero for g in range(nh)], axis=1)
             for h in range(nh)], axis=0)

    if nchunks == 1:
        sts = _glr_run_streams(load_qkl, load_vg, store_o, [init_state(b) for b in range(sb)],
                               pm, rd, cst, c=c, nh=nh)
    else:
        sts = [_glr_run(load_qkl, load_vg, store_o, init_state(0), pm, rd, cst, scratch,
                        c=c, nchunks=nchunks, nh=nh, w=w)]
    for b, st in enumerate(sts):
        for h in range(nh):
            s_ref[b, h] = st[h * A_DV:(h + 1) * A_DV, heads[h]].T


def _glr_streams_per_step(nb, nchunks):
    return nb if nchunks == 1 else 1


def _hgrn(proj, lbl, norm, s0, layer, nb, t, c):
    w = HGRN_W
    nh = w // A_DK
    pm = _prefix_matrix(c)
    rd = _diag_matrix(c, nh, w)
    nchunks = t // c
    has_s0 = s0 is not None
    sb = _glr_streams_per_step(nb, nchunks)

    def col(off):
        return pl.BlockSpec((sb * t, w), lambda b, h: (b, off // nh + h))

    st_spec = pl.BlockSpec((sb, nh, A_DK, A_DV), lambda b, h: (b, h, 0, 0))
    in_specs = [pl.BlockSpec(pm.shape, lambda b, h: (0, 0)),
                pl.BlockSpec(rd.shape, lambda b, h: (0, 0)),
                pl.BlockSpec((DEPTH, w), lambda b, h: (0, h)),
                pl.BlockSpec((1, LANES), lambda b, h: (0, 0)),
                col(COL_A_Q), col(COL_A_F), col(COL_A_I), col(COL_A_G)]
    args = [pm, rd, lbl, norm.reshape(1, LANES), proj, proj, proj, proj]
    if has_s0:
        in_specs.append(st_spec)
        args.append(s0)
    return pl.pallas_call(
        functools.partial(_hgrn_kernel, c=c, nchunks=nchunks, layer=layer, has_s0=has_s0, sb=sb),
        grid=(nb // sb, A_HEADS // nh),
        in_specs=in_specs,
        out_specs=[pl.BlockSpec((sb * t, w), lambda b, h: (b, h)), st_spec],
        out_shape=[jax.ShapeDtypeStruct((nb * t, A_WIDTH), BF16),
                   jax.ShapeDtypeStruct((nb, A_HEADS, A_DK, A_DV), F32)],
        scratch_shapes=_glr_scratch(t, c, nh, w) if nchunks > 1 else [],
        compiler_params=_cparams(("parallel", "parallel")),
        name="hgrn",
    )(*args)


def _gla_kernel(*refs, c, nchunks, has_s0, sb):
    pm_ref, rd_ref, wa_ref, ba_ref, norm_ref, q_ref, k_ref, v_ref, r_ref, ca_ref = refs[:10]
    s0_ref = refs[10] if has_s0 else None
    o_ref, s_ref = refs[10 + has_s0:12 + has_s0]
    scratch = refs[12 + has_s0:]
    wa = wa_ref[...]
    ba = ba_ref[...]
    norm = norm_ref[...]
    pm = pm_ref[...]
    rd = rd_ref[...]
    w = GLA_W
    nh = w // C_DK
    npl = w // LANES
    cst = _glr_consts(c, nh, w)
    heads = [slice(h * C_DV, (h + 1) * C_DV) for h in range(nh)]
    lane = lax.broadcasted_iota(jnp.int32, (1, LANES), 1)
    half = [lane < C_DK, lane >= C_DK]

    def load_qkl(rows):
        x = _dot(ca_ref[rows, :].astype(BF16), wa) + ba
        lf2 = (jnp.minimum(x, 0.0) - jnp.log1p(jnp.exp(-jnp.abs(x)))) * (LOG2E / GLA_NORMALIZER)
        return q_ref[rows, :] * (C_DK ** -0.5), k_ref[rows, :], lf2

    def load_vg(rows):
        v = v_ref[rows, :]
        return [v[:, sl] for sl in heads], r_ref[rows, :]

    def store_o(rows, o, gate):
        o_ref[rows, :] = jnp.concatenate(
            [_gated_norm(o[:, sl], norm, gate[:, sl]) for sl in heads], axis=1).astype(o_ref.dtype)

    zero = jnp.zeros((C_DV, LANES), F32)

    def init_state(b):
        if not has_s0:
            return jnp.zeros((nh * C_DV, w), F32)
        blocks = []
        for h in range(nh):
            p, h2 = divmod(h, 2)
            own = jnp.where(half[h2], s0_ref[b, p].T, 0.0)
            blocks.append(jnp.concatenate([own if g == p else zero for g in range(npl)], axis=1))
        return jnp.concatenate(blocks, axis=0)

    if nchunks == 1:
        sts = _glr_run_streams(load_qkl, load_vg, store_o, [init_state(b) for b in range(sb)],
                               pm, rd, cst, c=c, nh=nh)
    else:
        sts = [_glr_run(load_qkl, load_vg, store_o, init_state(0), pm, rd, cst, scratch,
                        c=c, nchunks=nchunks, nh=nh, w=w)]
    for b, st in enumerate(sts):
        for p in range(npl):
            cols = slice(p * LANES, (p + 1) * LANES)
            pair = (st[2 * p * C_DV:(2 * p + 1) * C_DV, cols]
                    + st[(2 * p + 1) * C_DV:(2 * p + 2) * C_DV, cols])
            s_ref[b, p] = pair.T


def _gla(proj, wa2, ba, norm, s0, nb, t, c):
    w = GLA_W
    nh = w // C_DK
    npl = w // LANES
    pm = _prefix_matrix(c)
    rd = _diag_matrix(c, nh, w)
    nchunks = t // c
    has_s0 = s0 is not None
    npair = C_HEADS // 2
    sb = _glr_streams_per_step(nb, nchunks)

    def col(off, width):
        return pl.BlockSpec((sb * t, width), lambda b, p: (b, off * LANES // width + p))

    st_spec = pl.BlockSpec((sb, npl, LANES, C_DV), lambda b, p: (b, p, 0, 0))
    in_specs = [pl.BlockSpec(pm.shape, lambda b, p: (0, 0)),
                pl.BlockSpec(rd.shape, lambda b, p: (0, 0)),
                pl.BlockSpec((LANES, w), lambda b, p: (0, p)),
                pl.BlockSpec((1, w), lambda b, p: (0, p)),
                pl.BlockSpec((1, LANES), lambda b, p: (0, 0)),
                col(COL_C_Q, w), col(COL_C_K, w), col(COL_C_V, 2 * w), col(COL_C_R, 2 * w),
                pl.BlockSpec((sb * t, LANES), lambda b, p: (b, COL_C_A))]
    args = [pm, rd, wa2, ba.reshape(1, -1), norm.reshape(1, LANES), proj, proj, proj, proj, proj]
    if has_s0:
        in_specs.append(st_spec)
        args.append(s0.reshape(nb, npair, 2 * C_DK, C_DV))
    o, s = pl.pallas_call(
        functools.partial(_gla_kernel, c=c, nchunks=nchunks, has_s0=has_s0, sb=sb),
        grid=(nb // sb, npair // npl),
        in_specs=in_specs,
        out_specs=[pl.BlockSpec((sb * t, 2 * w), lambda b, p: (b, p)), st_spec],
        out_shape=[jax.ShapeDtypeStruct((nb * t, C_WIDTH), BF16),
                   jax.ShapeDtypeStruct((nb, npair, 2 * C_DK, C_DV), F32)],
        scratch_shapes=_glr_scratch(t, c, nh, w) if nchunks > 1 else [],
        compiler_params=_cparams(("parallel", "parallel")),
        name="gla",
    )(*args)
    return o, s.reshape(nb, C_HEADS, C_DK, C_DV)


SWA_CB = 8
Q_SCALE = B_SCALE * 1.4426950408889634


def _rope_tables(pos):
    half = B_HD // 2
    inv = ROPE_THETA ** (-jnp.arange(half, dtype=F32) / half)
    ang = pos.astype(F32)[:, None] * inv[None, :]
    cos = jnp.cos(ang)
    sin = jnp.sin(ang)
    cos_t = jnp.tile(cos, (1, 2 * LANES // B_HD))
    sin_t = jnp.tile(jnp.concatenate([-sin, sin], axis=-1), (1, LANES // B_HD))
    return cos_t, sin_t


def _rope(x, cos_t, sin_t):
    reps = x.shape[1] // LANES
    if reps > 1:
        cos_t = jnp.concatenate([cos_t] * reps, axis=1)
        sin_t = jnp.concatenate([sin_t] * reps, axis=1)
    return x * cos_t + _swap_halves(x) * sin_t


def _swap_halves(x):
    n = x.shape[1]
    half = B_HD // 2
    lane = lax.broadcasted_iota(jnp.int32, (1, n), 1)
    first = (lane & (B_HD - 1)) < half
    return jnp.where(first, pltpu.roll(x, n - half, axis=1), pltpu.roll(x, half, axis=1))


def _attend(qs, kb, vb, sink_row, valid):
    return _softmax_pv(_dot_nt(kb, qs), vb, sink_row, valid)


def _softmax_pv(s, vb, sink_row, valid):
    if valid is not None:
        s = jnp.where(valid, s, NEG_BIG)
    sink2 = sink_row * LOG2E
    m = jnp.maximum(jnp.max(s, axis=0, keepdims=True), sink2)
    p = jnp.exp2(s - m)
    den = jnp.sum(p, axis=0, keepdims=True) + jnp.exp2(sink2 - m)
    o_t = _dot_tn(vb, p.astype(BF16)) * (1.0 / den)
    return o_t.T


def _sink_row(sink_ref, base, t):
    lanes = lax.broadcasted_iota(jnp.int32, (1, B_GROUP * t), 1)
    row = jnp.zeros((1, B_GROUP * t), F32)
    for g in range(B_GROUP):
        row = jnp.where((lanes >= g * t) & (lanes < (g + 1) * t), sink_ref[base + g], row)
    return row


def _swa_prompt_kernel(sink_ref, q_ref, k_ref, v_ref, cq_ref, sq_ref, ck_ref, sk_ref,
                       o_ref, kr_ref, vr_ref, kro_ref, vbo_ref, *, t, layer):
    khp = pl.program_id(1)
    step = pl.program_id(2)

    @pl.when(step == 0)
    def _():
        kr = _rope(k_ref[...], ck_ref[...], sk_ref[...])
        kro_ref[...] = kr.astype(BF16)
        vbo_ref[...] = v_ref[...].astype(BF16)
        kr_ref[0] = kr[t - WINDOW:, :]
        vr_ref[0] = v_ref[t - WINDOW:, :]

    band = (WINDOW_CHUNKS + 1) * CHUNK
    sink_rows = [_sink_row(sink_ref, layer * B_HEADS + (khp * 2 + kv) * B_GROUP, CHUNK)
                 for kv in range(2)]
    for ci in range(SWA_CB):
        cidx = step * SWA_CB + ci
        rows = slice(ci * CHUNK, (ci + 1) * CHUNK)
        s0 = pl.multiple_of(jnp.maximum(cidx - WINDOW_CHUNKS, 0) * CHUNK, CHUNK)
        qr = _rope(q_ref[rows, :], cq_ref[rows, :], sq_ref[rows, :]).astype(BF16)
        kband = kro_ref[pl.ds(s0, band), :]
        vband = vbo_ref[pl.ds(s0, band), :]
        key_pos = s0 + lax.broadcasted_iota(jnp.int32, (band, 1), 0)
        valid = key_pos < (cidx + 1) * CHUNK
        for kv in range(2):
            base = kv * B_GROUP * B_HD
            qs = jnp.concatenate(
                [qr[:, base + g * B_HD:base + (g + 1) * B_HD] for g in range(B_GROUP)], axis=0)
            o = _attend(qs, kband[:, kv * B_HD:(kv + 1) * B_HD],
                        vband[:, kv * B_HD:(kv + 1) * B_HD], sink_rows[kv], valid)
            for g in range(0, B_GROUP, 2):
                o_ref[rows, base + g * B_HD:base + (g + 2) * B_HD] = jnp.concatenate(
                    [o[g * CHUNK:(g + 1) * CHUNK], o[(g + 1) * CHUNK:(g + 2) * CHUNK]],
                    axis=1).astype(o_ref.dtype)


def _swa_prompt(proj, sinks, layer, nb, t):
    rb = SWA_CB * CHUNK
    nc = t // rb
    cos_t, sin_t = _rope_tables(jnp.arange(t))
    qw = 2 * B_GROUP * B_HD
    qoff = COL_B_Q * LANES // qw
    kernel = functools.partial(_swa_prompt_kernel, t=t, layer=layer)
    return pl.pallas_call(
        kernel,
        grid=(nb, B_KV_HEADS // 2, nc),
        in_specs=[pl.BlockSpec(memory_space=pltpu.SMEM),
                  pl.BlockSpec((rb, qw), lambda b, p, c: (b * nc + c, qoff + p)),
                  pl.BlockSpec((t, LANES), lambda b, p, c: (b, COL_B_K + p)),
                  pl.BlockSpec((t, LANES), lambda b, p, c: (b, COL_B_V + p)),
                  pl.BlockSpec((rb, LANES), lambda b, p, c: (c, 0)),
                  pl.BlockSpec((rb, LANES), lambda b, p, c: (c, 0)),
                  pl.BlockSpec((t, LANES), lambda b, p, c: (0, 0)),
                  pl.BlockSpec((t, LANES), lambda b, p, c: (0, 0))],
        out_specs=[pl.BlockSpec((rb, qw), lambda b, p, c: (b * nc + c, p)),
                   pl.BlockSpec((1, WINDOW, LANES), lambda b, p, c: (b, 0, p)),
                   pl.BlockSpec((1, WINDOW, LANES), lambda b, p, c: (b, 0, p))],
        out_shape=[jax.ShapeDtypeStruct((nb * t, B_WIDTH), BF16),
                   jax.ShapeDtypeStruct((nb, WINDOW, B_KV_HEADS * B_HD), F32),
                   jax.ShapeDtypeStruct((nb, WINDOW, B_KV_HEADS * B_HD), F32)],
        scratch_shapes=[pltpu.VMEM((t, LANES), BF16), pltpu.VMEM((t, LANES), BF16)],
        compiler_params=_cparams(("parallel", "parallel", "arbitrary")),
        name="swa_prompt",
    )(sinks.reshape(-1), proj, proj, proj, cos_t * Q_SCALE, sin_t * Q_SCALE, cos_t, sin_t)


def _swa_sample_kernel(sink_ref, q_ref, k_ref, v_ref, ckc_ref, cvc_ref, cos_ref, sin_ref,
                       o_ref, kr_ref, vr_ref, *, t, layer):
    cos_t = cos_ref[...]
    sin_t = sin_ref[...]
    kr = _rope(k_ref[...], cos_t, sin_t)
    v = v_ref[...]
    kr_ref[...] = kr
    vr_ref[...] = v
    qr = (_rope(q_ref[...], cos_t, sin_t) * Q_SCALE).astype(BF16)
    ka = jnp.concatenate([ckc_ref[0], kr], axis=0).astype(BF16)
    va = jnp.concatenate([cvc_ref[0], v], axis=0).astype(BF16)
    for kh in range(B_KV_HEADS):
        base = kh * B_GROUP * B_HD
        qs = jnp.concatenate(
            [qr[:, base + g * B_HD:base + (g + 1) * B_HD] for g in range(B_GROUP)], axis=0)
        sink_row = _sink_row(sink_ref, layer * B_HEADS + kh * B_GROUP, t)
        o = _attend(qs, ka[:, kh * B_HD:(kh + 1) * B_HD], va[:, kh * B_HD:(kh + 1) * B_HD],
                    sink_row, None)
        for g in range(0, B_GROUP, 2):
            o_ref[:, base + g * B_HD:base + (g + 2) * B_HD] = jnp.concatenate(
                [o[g * t:(g + 1) * t], o[(g + 1) * t:(g + 2) * t]], axis=1).astype(o_ref.dtype)


def _swa_sample(proj, cache_k, cache_v, sinks, layer, nb, t):
    cos_t, sin_t = _rope_tables(PAST_LEN + jnp.arange(t))
    kvw = B_KV_HEADS * B_HD
    kernel = functools.partial(_swa_sample_kernel, t=t, layer=layer)
    return pl.pallas_call(
        kernel,
        grid=(nb,),
        in_specs=[pl.BlockSpec(memory_space=pltpu.SMEM),
                  pl.BlockSpec((t, B_WIDTH), lambda b: (b, COL_B_Q * LANES // B_WIDTH)),
                  pl.BlockSpec((t, kvw), lambda b: (b, COL_B_K * LANES // kvw)),
                  pl.BlockSpec((t, kvw), lambda b: (b, COL_B_V * LANES // kvw)),
                  pl.BlockSpec((1, WINDOW, kvw), lambda b: (b, 0, 0)),
                  pl.BlockSpec((1, WINDOW, kvw), lambda b: (b, 0, 0)),
                  pl.BlockSpec((t, LANES), lambda b: (0, 0)),
                  pl.BlockSpec((t, LANES), lambda b: (0, 0))],
        out_specs=[pl.BlockSpec((t, B_WIDTH), lambda b: (b, 0)),
                   pl.BlockSpec((t, kvw), lambda b: (b, 0)),
                   pl.BlockSpec((t, kvw), lambda b: (b, 0))],
        out_shape=[jax.ShapeDtypeStruct((nb * t, B_WIDTH), BF16),
                   jax.ShapeDtypeStruct((nb * t, kvw), F32),
                   jax.ShapeDtypeStruct((nb * t, kvw), F32)],
        compiler_params=_cparams(("parallel",)),
        name="swa_sample",
    )(sinks.reshape(-1), proj, proj, proj,
      cache_k.reshape(nb, WINDOW, kvw), cache_v.reshape(nb, WINDOW, kvw), cos_t, sin_t)


def _layer_sample(x, w, layer, nb, t, cache, s_a, s_c, ff_pad):
    m, d = x.shape
    h = _rmsnorm(x, w["norm_mix"][layer], BF16, m)
    proj, wb_in = _mm_in_cast(h, w["w_in"], layer, IN_COLS_PAD, MM_TN_IN)
    o_a, s_a_new = _hgrn(proj, w["lb_logits"], w["hgrn_norm"][layer], s_a, layer, nb, t, t)
    o_b, k_rows, v_rows = _swa_sample(proj, cache[0], cache[1], w["sinks"], layer, nb, t)
    o_c, s_c_new = _gla(proj, w["w_a2"][layer], w["b_a"][layer], w["gla_norm"][layer],
                        s_c, nb, t, t)
    x, wb_out = _mm_out_cast(o_a, o_b, o_c, w["w_out"], x, layer, min(d, 512))
    h = _rmsnorm(x, w["norm_ffn"][layer], BF16, m)
    mid, wb_gate, wb_up = _gate_up_cast(h, w["w_gate_up"], layer, ff_pad)
    x, wb_down = _down_cast(mid, w["w_down"], x, layer)
    outs = (k_rows.reshape(nb, t, B_KV_HEADS, B_HD), v_rows.reshape(nb, t, B_KV_HEADS, B_HD),
            s_a_new, s_c_new)
    return x, outs, dict(w_in=wb_in, w_out=wb_out, w_gate=wb_gate, w_up=wb_up, w_down=wb_down)


def _layer_prompt(x, normed, w, wb, layer, nb, t):
    m, d = x.shape
    tm = min(m, 1024)
    if normed is None:
        h = _rmsnorm(x, w["norm_mix"][layer], BF16, min(m, 256))
        proj = _matmul(h, wb["w_in"], tm, MM_TN_IN, F32)
    else:
        proj = _matmul(normed[0], wb["w_in"], tm, MM_TN_IN, F32, normed[1])
    o_a, s_a_new = _hgrn(proj, w["lb_logits"], w["hgrn_norm"][layer], None, layer, nb, t, CHUNK)
    o_b, k_rows, v_rows = _swa_prompt(proj, w["sinks"], layer, nb, t)
    o_c, s_c_new = _gla(proj, w["w_a2"][layer], w["b_a"][layer], w["gla_norm"][layer],
                        None, nb, t, CHUNK)
    x, xg, ssq = _matmul_out(o_a, o_b, o_c, wb["w_out"], x, w["norm_ffn"][layer], tm, min(d, 512))
    mid = _matmul_gate_up(xg, ssq, wb["w_gate"], wb["w_up"], tm, 512)
    tk = wb["w_down"].shape[0] // 4
    if layer + 1 < DEPTH:
        x, xg, ssq = _matmul_down(mid, wb["w_down"], x, tm, min(d, 1024), tk,
                                  w["norm_mix"][layer + 1])
        normed = (xg, ssq)
    else:
        x = _matmul_down(mid, wb["w_down"], x, tm, min(d, 1024), tk)
        normed = None
    outs = (k_rows.reshape(nb, WINDOW, B_KV_HEADS, B_HD),
            v_rows.reshape(nb, WINDOW, B_KV_HEADS, B_HD), s_a_new, s_c_new)
    return x, normed, outs


def kernel(x_prompt, x_sample, cache_k_swa, cache_v_swa, state_hgrn, state_gla, norm_mix, w_in,
           hgrn_lb_logits, hgrn_norm, swa_sinks, gla_w_alpha2, gla_b_alpha, gla_norm, w_out,
           norm_ffn, w_gate_up, w_down, norm_final):
    n_p, t_p, d = x_prompt.shape
    n_s, t_s, _ = x_sample.shape
    d_ff = w_down.shape[1]
    ff_pad = -(-d_ff // 1024) * 1024
    w = dict(
        norm_mix=norm_mix, norm_ffn=norm_ffn, hgrn_norm=hgrn_norm, gla_norm=gla_norm,
        lb_logits=hgrn_lb_logits, sinks=swa_sinks, b_a=gla_b_alpha,
        w_in=w_in, w_out=w_out, w_gate_up=w_gate_up, w_down=w_down,
        w_a2=jnp.pad(gla_w_alpha2, ((0, 0), (0, LANES - GLA_RANK), (0, 0))).astype(BF16),
    )
    xp = x_prompt.reshape(n_p * t_p, d)
    xs = x_sample.reshape(n_s * t_s, d)
    outs_p, outs_s = [], []
    normed = None
    for layer in range(DEPTH):
        xs, rest, wb = _layer_sample(xs, w, layer, n_s, t_s,
                                     (cache_k_swa[layer], cache_v_swa[layer]),
                                     state_hgrn[layer], state_gla[layer], ff_pad)
        outs_s.append(rest)
        xp, normed, rest = _layer_prompt(xp, normed, w, wb, layer, n_p, t_p)
        outs_p.append(rest)
    y_p = _rmsnorm(xp, norm_final, F32, min(xp.shape[0], 256)).reshape(n_p, t_p, d)
    y_s = _rmsnorm(xs, norm_final, F32, min(xs.shape[0], 256)).reshape(n_s, t_s, d)
    stack = lambda outs, i: jnp.stack([o[i] for o in outs])
    return (y_p, y_s,
            stack(outs_p, 0), stack(outs_p, 1), stack(outs_p, 2), stack(outs_p, 3),
            stack(outs_s, 0), stack(outs_s, 1), stack(outs_s, 2), stack(outs_s, 3))
```

```python
import functools

import jax
import jax.numpy as jnp
import numpy as np
from jax import lax
from jax.experimental import pallas as pl
from jax.experimental.pallas import tpu as pltpu

F32 = jnp.float32
BF16 = jnp.bfloat16

DEPTH = 2
PAST_LEN = 4096
CHUNK = 64
EPS = 1e-6
NEG_BIG = -1e30
LB_FLOOR = 1e-30
A_HEADS = 8
A_DK = 128
A_DV = 128
A_WIDTH = A_HEADS * A_DV
B_HEADS = 32
B_KV_HEADS = 4
B_GROUP = B_HEADS // B_KV_HEADS
B_HD = 64
B_WIDTH = B_HEADS * B_HD
B_SCALE = B_HD ** -0.5
WINDOW = 128
WINDOW_CHUNKS = WINDOW // CHUNK
ROPE_THETA = 10000.0
C_HEADS = 8
C_DK = 64
C_DV = 128
C_WIDTH = C_HEADS * C_DV
GLA_RANK = 16
GLA_NORMALIZER = 16.0

LANES = 128

COL_A_Q = 0
COL_A_F = 8
COL_A_I = 16
COL_A_G = 24
COL_B_Q = 32
COL_B_K = 48
COL_B_V = 50
COL_C_Q = 52
COL_C_K = 56
COL_C_V = 60
COL_C_R = 68
COL_C_A = 76
IN_COLS = 9744
IN_COLS_PAD = 9984
MM_TN_IN = 768
VMEM_LIMIT = 56 * 1024 * 1024


def _cparams(sem):
    return pltpu.CompilerParams(dimension_semantics=sem, vmem_limit_bytes=VMEM_LIMIT)


def _dot(a, b):
    return jnp.dot(a, b, preferred_element_type=F32)


def _dot_nt(a, b):
    return lax.dot_general(a, b, (((1,), (1,)), ((), ())), preferred_element_type=F32)


def _dot_tn(a, b):
    return lax.dot_general(a, b, (((0,), (0,)), ((), ())), preferred_element_type=F32)


def _rmsnorm_kernel(x_ref, g_ref, o_ref):
    x = x_ref[...]
    var = jnp.mean(x * x, axis=-1, keepdims=True)
    o_ref[...] = (x * lax.rsqrt(var + EPS) * g_ref[...]).astype(o_ref.dtype)


def _rmsnorm(x, g, out_dtype, tm):
    m, d = x.shape
    return pl.pallas_call(
        _rmsnorm_kernel,
        grid=(m // tm,),
        in_specs=[pl.BlockSpec((tm, d), lambda i: (i, 0)),
                  pl.BlockSpec((1, d), lambda i: (0, 0))],
        out_specs=pl.BlockSpec((tm, d), lambda i: (i, 0)),
        out_shape=jax.ShapeDtypeStruct((m, d), out_dtype),
        compiler_params=_cparams(("parallel",)),
        name="rmsnorm",
    )(x, g.reshape(1, d))


def _row_scale(ssq_ref, d):
    return lax.rsqrt(jnp.sum(ssq_ref[...], axis=-1, keepdims=True) * (1.0 / d) + EPS)


def _lane_partial_sq(x):
    sq = x * x
    part = sq[:, 0:LANES]
    for c in range(1, x.shape[1] // LANES):
        part = part + sq[:, c * LANES:(c + 1) * LANES]
    return part


def _accumulate(ref, part, first):
    @pl.when(first)
    def _():
        ref[...] = part

    @pl.when(jnp.logical_not(first))
    def _():
        ref[...] += part


def _emit_normed(x, g_ref, xg_ref, ssq_ref, first):
    xg_ref[...] = (x * g_ref[...]).astype(xg_ref.dtype)
    _accumulate(ssq_ref, _lane_partial_sq(x), first)


def _mm_kernel(x_ref, w_ref, o_ref):
    o_ref[...] = _dot(x_ref[...], w_ref[...]).astype(o_ref.dtype)


def _mm_scaled_kernel(x_ref, s_ref, w_ref, o_ref):
    r = _row_scale(s_ref, x_ref.shape[1])
    o_ref[...] = (_dot(x_ref[...], w_ref[...]) * r).astype(o_ref.dtype)


def _matmul(x, w, tm, tn, out_dtype, ssq=None):
    m, k = x.shape
    n = w.shape[1]
    x_spec = pl.BlockSpec((tm, k), lambda i, j: (i, 0))
    w_spec = pl.BlockSpec((k, tn), lambda i, j: (0, j))
    s_spec = pl.BlockSpec((tm, LANES), lambda i, j: (i, 0))
    return pl.pallas_call(
        _mm_kernel if ssq is None else _mm_scaled_kernel,
        grid=(m // tm, n // tn),
        in_specs=[x_spec, w_spec] if ssq is None else [x_spec, s_spec, w_spec],
        out_specs=pl.BlockSpec((tm, tn), lambda i, j: (i, j)),
        out_shape=jax.ShapeDtypeStruct((m, n), out_dtype),
        compiler_params=_cparams(("parallel", "arbitrary")),
        name="mm_in",
    )(*((x, w) if ssq is None else (x, ssq, w)))


def _mm_out_kernel(oa_ref, ob_ref, oc_ref, w_ref, r_ref, g_ref, o_ref, xg_ref, ssq_ref):
    j = pl.program_id(1)
    x = (r_ref[...] + _dot(oa_ref[...], w_ref[0:A_WIDTH, :])
         + _dot(ob_ref[...], w_ref[A_WIDTH:A_WIDTH + B_WIDTH, :])
         + _dot(oc_ref[...], w_ref[A_WIDTH + B_WIDTH:, :]))
    o_ref[...] = x
    _emit_normed(x, g_ref, xg_ref, ssq_ref, j == 0)


def _matmul_out(oa, ob, oc, w, resid, gain, tm, tn):
    m = oa.shape[0]
    k, n = w.shape
    tile = pl.BlockSpec((tm, tn), lambda i, j: (i, j))
    return pl.pallas_call(
        _mm_out_kernel,
        grid=(m // tm, n // tn),
        in_specs=[pl.BlockSpec((tm, A_WIDTH), lambda i, j: (i, 0), pipeline_mode=pl.Buffered(1)),
                  pl.BlockSpec((tm, B_WIDTH), lambda i, j: (i, 0), pipeline_mode=pl.Buffered(1)),
                  pl.BlockSpec((tm, C_WIDTH), lambda i, j: (i, 0), pipeline_mode=pl.Buffered(1)),
                  pl.BlockSpec((k, tn), lambda i, j: (0, j)),
                  tile,
                  pl.BlockSpec((1, tn), lambda i, j: (0, j))],
        out_specs=[tile, tile, pl.BlockSpec((tm, LANES), lambda i, j: (i, 0))],
        out_shape=[jax.ShapeDtypeStruct((m, n), F32), jax.ShapeDtypeStruct((m, n), BF16),
                   jax.ShapeDtypeStruct((m, LANES), F32)],
        compiler_params=_cparams(("parallel", "arbitrary")),
        name="mm_out",
    )(oa, ob, oc, w, resid, gain.reshape(1, n))


def _mm_gate_up_kernel(x_ref, s_ref, wg_ref, wu_ref, o_ref):
    x = x_ref[...]
    r = _row_scale(s_ref, x.shape[1])
    gate = _dot(x, wg_ref[...]) * r
    up = _dot(x, wu_ref[...]) * r
    o_ref[...] = (gate * (1.0 / (1.0 + jnp.exp(-gate))) * up).astype(o_ref.dtype)


def _matmul_gate_up(x, ssq, wg, wu, tm, tn):
    m, k = x.shape
    n = wg.shape[1]
    return pl.pallas_call(
        _mm_gate_up_kernel,
        grid=(m // tm, n // tn),
        in_specs=[pl.BlockSpec((tm, k), lambda i, j: (i, 0)),
                  pl.BlockSpec((tm, LANES), lambda i, j: (i, 0)),
                  pl.BlockSpec((k, tn), lambda i, j: (0, j)),
                  pl.BlockSpec((k, tn), lambda i, j: (0, j))],
        out_specs=pl.BlockSpec((tm, tn), lambda i, j: (i, j)),
        out_shape=jax.ShapeDtypeStruct((m, n), BF16),
        compiler_params=_cparams(("parallel", "arbitrary")),
        name="mm_gate_up",
    )(x, ssq, wg, wu)


def _mm_down_kernel(x_ref, w_ref, r_ref, *rest, nk, emit):
    kk = pl.program_id(2)
    o_ref = rest[1] if emit else rest[0]

    @pl.when(kk == 0)
    def _():
        o_ref[...] = r_ref[...] + _dot(x_ref[...], w_ref[...])

    if not emit:
        @pl.when(kk != 0)
        def _():
            o_ref[...] += _dot(x_ref[...], w_ref[...])
    else:
        assert nk > 1
        g_ref, _, xg_ref, ssq_ref = rest

        @pl.when((kk != 0) & (kk != nk - 1))
        def _():
            o_ref[...] += _dot(x_ref[...], w_ref[...])

        @pl.when(kk == nk - 1)
        def _():
            x = o_ref[...] + _dot(x_ref[...], w_ref[...])
            o_ref[...] = x
            _emit_normed(x, g_ref, xg_ref, ssq_ref, pl.program_id(1) == 0)


def _matmul_down(x, w, resid, tm, tn, tk, gain=None):
    m, k = x.shape
    n = w.shape[1]
    emit = gain is not None
    tile = pl.BlockSpec((tm, tn), lambda i, j, kk: (i, j))
    in_specs = [pl.BlockSpec((tm, tk), lambda i, j, kk: (i, kk)),
                pl.BlockSpec((tk, tn), lambda i, j, kk: (kk, j)),
                tile]
    out_specs, out_shape, args = [tile], [jax.ShapeDtypeStruct((m, n), F32)], [x, w, resid]
    if emit:
        in_specs.append(pl.BlockSpec((1, tn), lambda i, j, kk: (0, j)))
        args.append(gain.reshape(1, n))
        out_specs += [tile, pl.BlockSpec((tm, LANES), lambda i, j, kk: (i, 0))]
        out_shape += [jax.ShapeDtypeStruct((m, n), BF16), jax.ShapeDtypeStruct((m, LANES), F32)]
    out = pl.pallas_call(
        functools.partial(_mm_down_kernel, nk=k // tk, emit=emit),
        grid=(m // tm, n // tn, k // tk),
        in_specs=in_specs,
        out_specs=out_specs,
        out_shape=out_shape,
        compiler_params=_cparams(("parallel", "arbitrary", "arbitrary")),
        name="mm_down",
    )(*args)
    return out if emit else out[0]


CAST_TN_GATE = 256
CAST_TK_DOWN = 512


def _masked_bf16(w, first, n_valid, axis):
    idx = first + lax.broadcasted_iota(jnp.int32, w.shape, axis)
    return jnp.where(idx < n_valid, w, 0.0).astype(BF16)


def _mm_in_cast_kernel(x_ref, w_ref, o_ref, wb_ref, *, n_valid, tn):
    idx = pl.program_id(0) * tn + lax.broadcasted_iota(jnp.int32, w_ref.shape, 0)
    wb = jnp.where(idx < n_valid, w_ref[...], 0.0).T.astype(BF16)
    wb_ref[...] = wb
    o_ref[...] = _dot(x_ref[...], wb)


def _mm_in_cast(x, w3, layer, n_pad, tn):
    m, k = x.shape
    n = w3.shape[2]
    return pl.pallas_call(
        functools.partial(_mm_in_cast_kernel, n_valid=n, tn=tn),
        grid=(n_pad // tn,),
        in_specs=[pl.BlockSpec((m, k), lambda j: (0, 0)),
                  pl.BlockSpec((None, tn, k), lambda j: (layer, j, 0))],
        out_specs=[pl.BlockSpec((m, tn), lambda j: (0, j)),
                   pl.BlockSpec((k, tn), lambda j: (0, j))],
        out_shape=[jax.ShapeDtypeStruct((m, n_pad), F32),
                   jax.ShapeDtypeStruct((k, n_pad), BF16)],
        compiler_params=_cparams(("arbitrary",)),
        name="mm_in_cast",
    )(x, jnp.swapaxes(w3, 1, 2))


def _mm_out_cast_kernel(oa_ref, ob_ref, oc_ref, w_ref, r_ref, o_ref, wb_ref, cat_ref):
    @pl.when(pl.program_id(0) == 0)
    def _():
        cat_ref[:, 0:A_WIDTH] = oa_ref[...]
        cat_ref[:, A_WIDTH:A_WIDTH + B_WIDTH] = ob_ref[...]
        cat_ref[:, A_WIDTH + B_WIDTH:] = oc_ref[...]

    wb = w_ref[...].astype(BF16)
    wb_ref[...] = wb
    o_ref[...] = r_ref[...] + _dot(cat_ref[...], wb)


def _mm_out_cast(oa, ob, oc, w3, resid, layer, tn):
    m = oa.shape[0]
    k, n = w3.shape[1:]
    return pl.pallas_call(
        _mm_out_cast_kernel,
        grid=(n // tn,),
        in_specs=[pl.BlockSpec((m, A_WIDTH), lambda j: (0, 0)),
                  pl.BlockSpec((m, B_WIDTH), lambda j: (0, 0)),
                  pl.BlockSpec((m, C_WIDTH), lambda j: (0, 0)),
                  pl.BlockSpec((None, k, tn), lambda j: (layer, 0, j)),
                  pl.BlockSpec((m, tn), lambda j: (0, j))],
        out_specs=[pl.BlockSpec((m, tn), lambda j: (0, j)),
                   pl.BlockSpec((k, tn), lambda j: (0, j))],
        out_shape=[jax.ShapeDtypeStruct((m, n), F32),
                   jax.ShapeDtypeStruct((k, n), BF16)],
        scratch_shapes=[pltpu.VMEM((m, k), BF16)],
        compiler_params=_cparams(("arbitrary",)),
        name="mm_out_cast",
    )(oa, ob, oc, w3, resid)


def _gate_up_cast_kernel(x_ref, wg_ref, wu_ref, o_ref, wgb_ref, wub_ref, *, n_tiles):
    valid = pl.program_id(0) < n_tiles
    wg = jnp.where(valid, wg_ref[...], 0.0).astype(BF16)
    wu = jnp.where(valid, wu_ref[...], 0.0).astype(BF16)
    wgb_ref[...] = wg
    wub_ref[...] = wu
    x = x_ref[...]
    gate = _dot(x, wg)
    o_ref[...] = (gate * (1.0 / (1.0 + jnp.exp(-gate))) * _dot(x, wu)).astype(o_ref.dtype)


def _gate_up_cast(x, wgu3, layer, ff_pad):
    m, k = x.shape
    tn = CAST_TN_GATE
    n_tiles = wgu3.shape[2] // 2 // tn
    assert n_tiles * tn * 2 == wgu3.shape[2]
    return pl.pallas_call(
        functools.partial(_gate_up_cast_kernel, n_tiles=n_tiles),
        grid=(ff_pad // tn,),
        in_specs=[pl.BlockSpec((m, k), lambda j: (0, 0)),
                  pl.BlockSpec((None, k, tn), lambda j: (layer, 0, jnp.minimum(j, n_tiles - 1))),
                  pl.BlockSpec((None, k, tn),
                               lambda j: (layer, 0, n_tiles + jnp.minimum(j, n_tiles - 1)))],
        out_specs=[pl.BlockSpec((m, tn), lambda j: (0, j)),
                   pl.BlockSpec((k, tn), lambda j: (0, j)),
                   pl.BlockSpec((k, tn), lambda j: (0, j))],
        out_shape=[jax.ShapeDtypeStruct((m, ff_pad), BF16),
                   jax.ShapeDtypeStruct((k, ff_pad), BF16),
                   jax.ShapeDtypeStruct((k, ff_pad), BF16)],
        compiler_params=_cparams(("arbitrary",)),
        name="gate_up_cast",
    )(x, wgu3, wgu3)


def _down_cast_kernel(x_ref, w_ref, r_ref, o_ref, wb_ref, *, k_valid, tk):
    kk = pl.program_id(0)
    wb = _masked_bf16(w_ref[...], kk * tk, k_valid, 0)
    wb_ref[...] = wb

    @pl.when(kk == 0)
    def _():
        o_ref[...] = r_ref[...] + _dot(x_ref[...], wb)

    @pl.when(kk != 0)
    def _():
        o_ref[...] += _dot(x_ref[...], wb)


def _down_cast(x, wd3, resid, layer):
    m, ff_pad = x.shape
    ff, d = wd3.shape[1:]
    tk = CAST_TK_DOWN
    last = (ff - 1) // tk
    return pl.pallas_call(
        functools.partial(_down_cast_kernel, k_valid=ff, tk=tk),
        grid=(ff_pad // tk,),
        in_specs=[pl.BlockSpec((m, tk), lambda kk: (0, kk)),
                  pl.BlockSpec((None, tk, d), lambda kk: (layer, jnp.minimum(kk, last), 0)),
                  pl.BlockSpec((m, d), lambda kk: (0, 0))],
        out_specs=[pl.BlockSpec((m, d), lambda kk: (0, 0)),
                   pl.BlockSpec((tk, d), lambda kk: (kk, 0))],
        out_shape=[jax.ShapeDtypeStruct((m, d), F32),
                   jax.ShapeDtypeStruct((ff_pad, d), BF16)],
        compiler_params=_cparams(("arbitrary",)),
        name="down_cast",
    )(x, wd3, resid)


LOG2E = 1.4426950408889634
SUBLANES = 8
HGRN_W = 256
GLA_W = 256
GLR_UNROLL = 8
GLR_UNROLL_STATE = 32


def _levels(c):
    out, b = [], 1
    while b < c:
        out.append(b)
        b *= 2
    return out


def _prefix_matrix(c):
    t = np.arange(c)[:, None]
    s = np.arange(c)[None, :]
    low = s <= t
    mats = [low]
    for b in _levels(c):
        if b >= SUBLANES:
            break
        same = (t // b) == (s // b)
        odd = ((t // b) % 2) == 1
        mats.append(np.where(odd, low & same, (~low) & same))
    m = np.concatenate(mats, axis=0).astype(np.float32)
    return jnp.asarray(np.concatenate([m, m, m], axis=1), dtype=BF16)


def _level_sums(g, b):
    parts = []
    for j in range(g.shape[0] // b):
        blk = g[j * b:(j + 1) * b]
        if j % 2 == 1:
            parts.append(blk - g[j * b - 1:j * b])
        else:
            parts.append(g[(j + 1) * b - 1:(j + 1) * b] - blk)
    return jnp.concatenate(parts, axis=0)


def _diag_matrix(c, nh, w):
    kd = w // nh
    lane = np.arange(w)[:, None]
    col = np.arange(nh * c)[None, :]
    return jnp.asarray(((col // c) == (lane // kd)).astype(np.float32), dtype=BF16)


def _glr_consts(c, nh, w):
    t = lax.broadcasted_iota(jnp.int32, (c, nh * c), 0)
    s = lax.broadcasted_iota(jnp.int32, (c, nh * c), 1) & (c - 1)
    lvl_masks = []
    for b in _levels(c):
        lg = b.bit_length() - 1
        tb = lax.shift_right_logical(t, lg)
        sb = lax.shift_right_logical(s, lg)
        lvl_masks.append((lax.shift_right_logical(tb, 1) == lax.shift_right_logical(sb, 1))
                         & ((tb & 1) == 1) & ((sb & 1) == 0))
    lgk = (w // nh).bit_length() - 1
    lane = lax.shift_right_logical(lax.broadcasted_iota(jnp.int32, (1, w), 1), lgk)
    srow = lax.broadcasted_iota(jnp.int32, (nh * LANES, w), 0) // LANES
    scol = lax.shift_right_logical(lax.broadcasted_iota(jnp.int32, (nh * LANES, w), 1), lgk)
    return dict(lvl_masks=lvl_masks, diag_valid=t == s,
                head_lanes=[lane == h for h in range(nh)], st_mask=srow == scol)


def _head_rows(x, cst, nh):
    if nh * LANES == x.shape[1]:
        z = jnp.zeros((x.shape[0], LANES), x.dtype)
        return jnp.concatenate(
            [jnp.concatenate([x[:, h * LANES:(h + 1) * LANES] if g == h else z
                              for g in range(nh)], axis=1) for h in range(nh)], axis=0)
    zero = jnp.zeros_like(x)
    return jnp.concatenate([jnp.where(cst["head_lanes"][h], x, zero) for h in range(nh)], axis=0)


def _glr_prefix(lf2, pm):
    hi = lf2.astype(BF16)
    r1 = lf2 - hi.astype(F32)
    mid = r1.astype(BF16)
    lo = (r1 - mid.astype(F32)).astype(BF16)
    return _dot(pm, jnp.concatenate([hi, mid, lo], axis=0))


def _glr_intra(qs, ks, lf2s, pm, rd, cst, *, c, nh):
    n = len(qs)
    prefs = [_glr_prefix(lf2, pm) for lf2 in lf2s]
    gs = [p[0:c] for p in prefs]
    qhat = [(qs[j] * jnp.exp2(gs[j])).astype(BF16) for j in range(n)]
    kdec = [(ks[j] * jnp.exp2(gs[j][c - 1:c, :] - gs[j])).astype(BF16) for j in range(n)]
    a = [jnp.where(cst["diag_valid"], _dot((qs[j] * ks[j]).astype(BF16), rd), 0.0)
         for j in range(n)]
    for i, b in enumerate(_levels(c)):
        if b < SUBLANES:
            ebs = [jnp.exp2(prefs[j][(1 + i) * c:(2 + i) * c]) for j in range(n)]
        else:
            ebs = [jnp.exp2(_level_sums(gs[j], b)) for j in range(n)]
        dots = [_dot_nt((qs[j] * ebs[j]).astype(BF16),
                        _head_rows((ks[j] * ebs[j]).astype(BF16), cst, nh)) for j in range(n)]
        a = [jnp.where(cst["lvl_masks"][i], dots[j], a[j]) for j in range(n)]
    return [(a[j].astype(BF16), qhat[j], kdec[j], jnp.exp2(gs[j][c - 1:c, :]))
            for j in range(n)]


def _glr_in_chunk(a, vs, nh):
    vb = [v.astype(BF16) for v in vs]
    z = jnp.zeros_like(vb[0])
    vbd = jnp.concatenate(
        [jnp.concatenate([vb[h] if g == h else z for g in range(nh)], axis=1) for h in range(nh)],
        axis=0)
    return _dot(a, vbd)


def _glr_state(chunks, st, cst, *, nh, streams=False):
    o_in = [ch[0] for ch in chunks]
    upd = [jnp.where(cst["st_mask"],
                     _dot_tn(jnp.concatenate([v.astype(BF16) for v in ch[4]], axis=1), ch[2]), 0.0)
           for ch in chunks]
    if streams:
        outs = [o_in[j] + _dot_nt(ch[1], st[j].astype(BF16)) for j, ch in enumerate(chunks)]
        return outs, [st[j] * ch[3] + upd[j] for j, ch in enumerate(chunks)]
    outs = []
    for j, (_, qhat, _, dec, _) in enumerate(chunks):
        outs.append(o_in[j] + _dot_nt(qhat, st.astype(BF16)))
        st = st * dec + upd[j]
    return outs, st


def _gated_norm(o, norm, gate):
    var = jnp.mean(o * o, axis=-1, keepdims=True)
    return o * lax.rsqrt(var + EPS) * norm * (gate * (1.0 / (1.0 + jnp.exp(-gate))))


def _glr_scratch(t, c, nh, w):
    return [pltpu.VMEM((t, nh * LANES), F32), pltpu.VMEM((t, w), BF16),
            pltpu.VMEM((t, w), BF16), pltpu.VMEM((t // c, SUBLANES, w), F32)]


def _glr_run_streams(load_qkl, load_vg, store_o, st0s, pm, rd, cst, *, c, nh):
    rows = [slice(b * c, (b + 1) * c) for b in range(len(st0s))]
    ins = [load_qkl(r) for r in rows]
    res = _glr_intra([x[0] for x in ins], [x[1] for x in ins], [x[2] for x in ins],
                     pm, rd, cst, c=c, nh=nh)
    vg = [load_vg(r) for r in rows]
    chunks = [(_glr_in_chunk(a, x[0], nh), qh, kd, dec, x[0])
              for (a, qh, kd, dec), x in zip(res, vg)]
    outs, sts = _glr_state(chunks, st0s, cst, nh=nh, streams=True)
    for r, o, x in zip(rows, outs, vg):
        store_o(r, o, x[1])
    return sts


def _glr_run(load_qkl, load_vg, store_o, st0, pm, rd, cst, scratch, *, c, nchunks, nh, w):
    a_s, qh_s, kd_s, dec_s = scratch
    u = min(GLR_UNROLL, nchunks)
    assert nchunks % u == 0

    def rows_of(n):
        return pl.ds(pl.multiple_of(n * c, c), c)

    def intra(i, carry):
        ns = [i * u + j for j in range(u)]
        ins = [load_qkl(rows_of(n)) for n in ns]
        res = _glr_intra([x[0] for x in ins], [x[1] for x in ins], [x[2] for x in ins],
                         pm, rd, cst, c=c, nh=nh)
        oin = [_glr_in_chunk(r4[0], load_vg(rows_of(n))[0], nh) for n, r4 in zip(ns, res)]
        for n, (_, qh, kd, dec), o in zip(ns, res, oin):
            rows = rows_of(n)
            a_s[rows, :] = o
            qh_s[rows, :] = qh
            kd_s[rows, :] = kd
            dec_s[n] = jnp.broadcast_to(dec, (SUBLANES, w))
        return carry

    lax.fori_loop(0, nchunks // u, intra, 0)

    us = min(GLR_UNROLL_STATE, nchunks)
    assert nchunks % us == 0

    def state(i, st):
        ns = [i * us + j for j in range(us)]
        ins = [(a_s[rows_of(n), :], qh_s[rows_of(n), :], kd_s[rows_of(n), :], dec_s[n, 0:1, :])
               + tuple(load_vg(rows_of(n))) for n in ns]
        outs, st = _glr_state([x[:5] for x in ins], st, cst, nh=nh)
        for n, o, x in zip(ns, outs, ins):
            store_o(rows_of(n), o, x[5])
        return st

    return lax.fori_loop(0, nchunks // us, state, st0)


def _hgrn_kernel(*refs, c, nchunks, layer, has_s0, sb):
    pm_ref, rd_ref, lbl_ref, norm_ref, q_ref, z_ref, v_ref, g_ref = refs[:8]
    s0_ref = refs[8] if has_s0 else None
    o_ref, s_ref = refs[8 + has_s0:10 + has_s0]
    scratch = refs[10 + has_s0:]
    w = HGRN_W
    nh = w // A_DK
    logits = lbl_ref[...]
    e = jnp.exp(logits - jnp.max(logits, axis=0, keepdims=True))
    probs = e / jnp.sum(e, axis=0, keepdims=True)
    lb = jnp.sum(probs[0:layer + 1], axis=0, keepdims=True) - probs[0:1]
    lb_floor = jnp.maximum(lb, LB_FLOOR)
    oml = 1.0 - lb
    norm = norm_ref[...]
    pm = pm_ref[...]
    rd = rd_ref[...]
    cst = _glr_consts(c, nh, w)
    heads = [slice(h * LANES, (h + 1) * LANES) for h in range(nh)]

    def load_qkl(rows):
        q = q_ref[rows, :]
        z = z_ref[rows, :]
        ez = jnp.exp(-jnp.abs(z))
        r = 1.0 / (1.0 + ez)
        pos = z >= 0.0
        lf2 = jnp.log2(lb_floor + oml * jnp.where(pos, r, ez * r))
        return q, oml * jnp.where(pos, ez * r, r), lf2

    def load_vg(rows):
        v = v_ref[rows, :]
        return [v[:, sl] for sl in heads], g_ref[rows, :]

    def store_o(rows, o, g):
        o_ref[rows, :] = jnp.concatenate(
            [_gated_norm(o[:, sl], norm, g[:, sl]) for sl in heads], axis=1).astype(o_ref.dtype)

    zero = jnp.zeros((A_DV, A_DK), F32)

    def init_state(b):
        if not has_s0:
            return jnp.zeros((nh * A_DV, w), F32)
        return jnp.concatenate(
            [jnp.concatenate([s0_ref[b, h].T if g == h else zero for g in range(nh)], axis=1)
             for h in range(nh)], axis=0)

    if nchunks == 1:
        sts = _glr_run_streams(load_qkl, load_vg, store_o, [init_state(b) for b in range(sb)],
                               pm, rd, cst, c=c, nh=nh)
    else:
        sts = [_glr_run(load_qkl, load_vg, store_o, init_state(0), pm, rd, cst, scratch,
                        c=c, nchunks=nchunks, nh=nh, w=w)]
    for b, st in enumerate(sts):
        for h in range(nh):
            s_ref[b, h] = st[h * A_DV:(h + 1) * A_DV, heads[h]].T


def _glr_streams_per_step(nb, nchunks):
    return nb if nchunks == 1 else 1


def _hgrn(proj, lbl, norm, s0, layer, nb, t, c):
    w = HGRN_W
    nh = w // A_DK
    pm = _prefix_matrix(c)
    rd = _diag_matrix(c, nh, w)
    nchunks = t // c
    has_s0 = s0 is not None
    sb = _glr_streams_per_step(nb, nchunks)

    def col(off):
        return pl.BlockSpec((sb * t, w), lambda b, h: (b, off // nh + h))

    st_spec = pl.BlockSpec((sb, nh, A_DK, A_DV), lambda b, h: (b, h, 0, 0))
    in_specs = [pl.BlockSpec(pm.shape, lambda b, h: (0, 0)),
                pl.BlockSpec(rd.shape, lambda b, h: (0, 0)),
                pl.BlockSpec((DEPTH, w), lambda b, h: (0, h)),
                pl.BlockSpec((1, LANES), lambda b, h: (0, 0)),
                col(COL_A_Q), col(COL_A_F), col(COL_A_I), col(COL_A_G)]
    args = [pm, rd, lbl, norm.reshape(1, LANES), proj, proj, proj, proj]
    if has_s0:
        in_specs.append(st_spec)
        args.append(s0)
    return pl.pallas_call(
        functools.partial(_hgrn_kernel, c=c, nchunks=nchunks, layer=layer, has_s0=has_s0, sb=sb),
        grid=(nb // sb, A_HEADS // nh),
        in_specs=in_specs,
        out_specs=[pl.BlockSpec((sb * t, w), lambda b, h: (b, h)), st_spec],
        out_shape=[jax.ShapeDtypeStruct((nb * t, A_WIDTH), BF16),
                   jax.ShapeDtypeStruct((nb, A_HEADS, A_DK, A_DV), F32)],
        scratch_shapes=_glr_scratch(t, c, nh, w) if nchunks > 1 else [],
        compiler_params=_cparams(("parallel", "parallel")),
        name="hgrn",
    )(*args)


def _gla_kernel(*refs, c, nchunks, has_s0, sb):
    pm_ref, rd_ref, wa_ref, ba_ref, norm_ref, q_ref, k_ref, v_ref, r_ref, ca_ref = refs[:10]
    s0_ref = refs[10] if has_s0 else None
    o_ref, s_ref = refs[10 + has_s0:12 + has_s0]
    scratch = refs[12 + has_s0:]
    wa = wa_ref[...]
    ba = ba_ref[...]
    norm = norm_ref[...]
    pm = pm_ref[...]
    rd = rd_ref[...]
    w = GLA_W
    nh = w // C_DK
    npl = w // LANES
    cst = _glr_consts(c, nh, w)
    heads = [slice(h * C_DV, (h + 1) * C_DV) for h in range(nh)]
    lane = lax.broadcasted_iota(jnp.int32, (1, LANES), 1)
    half = [lane < C_DK, lane >= C_DK]

    def load_qkl(rows):
        x = _dot(ca_ref[rows, :].astype(BF16), wa) + ba
        lf2 = (jnp.minimum(x, 0.0) - jnp.log1p(jnp.exp(-jnp.abs(x)))) * (LOG2E / GLA_NORMALIZER)
        return q_ref[rows, :] * (C_DK ** -0.5), k_ref[rows, :], lf2

    def load_vg(rows):
        v = v_ref[rows, :]
        return [v[:, sl] for sl in heads], r_ref[rows, :]

    def store_o(rows, o, gate):
        o_ref[rows, :] = jnp.concatenate(
            [_gated_norm(o[:, sl], norm, gate[:, sl]) for sl in heads], axis=1).astype(o_ref.dtype)

    zero = jnp.zeros((C_DV, LANES), F32)

    def init_state(b):
        if not has_s0:
            return jnp.zeros((nh * C_DV, w), F32)
        blocks = []
        for h in range(nh):
            p, h2 = divmod(h, 2)
            own = jnp.where(half[h2], s0_ref[b, p].T, 0.0)
            blocks.append(jnp.concatenate([own if g == p else zero for g in range(npl)], axis=1))
        return jnp.concatenate(blocks, axis=0)

    if nchunks == 1:
        sts = _glr_run_streams(load_qkl, load_vg, store_o, [init_state(b) for b in range(sb)],
                               pm, rd, cst, c=c, nh=nh)
    else:
        sts = [_glr_run(load_qkl, load_vg, store_o, init_state(0), pm, rd, cst, scratch,
                        c=c, nchunks=nchunks, nh=nh, w=w)]
    for b, st in enumerate(sts):
        for p in range(npl):
            cols = slice(p * LANES, (p + 1) * LANES)
            pair = (st[2 * p * C_DV:(2 * p + 1) * C_DV, cols]
                    + st[(2 * p + 1) * C_DV:(2 * p + 2) * C_DV, cols])
            s_ref[b, p] = pair.T


def _gla(proj, wa2, ba, norm, s0, nb, t, c):
    w = GLA_W
    nh = w // C_DK
    npl = w // LANES
    pm = _prefix_matrix(c)
    rd = _diag_matrix(c, nh, w)
    nchunks = t // c
    has_s0 = s0 is not None
    npair = C_HEADS // 2
    sb = _glr_streams_per_step(nb, nchunks)

    def col(off, width):
        return pl.BlockSpec((sb * t, width), lambda b, p: (b, off * LANES // width + p))

    st_spec = pl.BlockSpec((sb, npl, LANES, C_DV), lambda b, p: (b, p, 0, 0))
    in_specs = [pl.BlockSpec(pm.shape, lambda b, p: (0, 0)),
                pl.BlockSpec(rd.shape, lambda b, p: (0, 0)),
                pl.BlockSpec((LANES, w), lambda b, p: (0, p)),
                pl.BlockSpec((1, w), lambda b, p: (0, p)),
                pl.BlockSpec((1, LANES), lambda b, p: (0, 0)),
                col(COL_C_Q, w), col(COL_C_K, w), col(COL_C_V, 2 * w), col(COL_C_R, 2 * w),
                pl.BlockSpec((sb * t, LANES), lambda b, p: (b, COL_C_A))]
    args = [pm, rd, wa2, ba.reshape(1, -1), norm.reshape(1, LANES), proj, proj, proj, proj, proj]
    if has_s0:
        in_specs.append(st_spec)
        args.append(s0.reshape(nb, npair, 2 * C_DK, C_DV))
    o, s = pl.pallas_call(
        functools.partial(_gla_kernel, c=c, nchunks=nchunks, has_s0=has_s0, sb=sb),
        grid=(nb // sb, npair // npl),
        in_specs=in_specs,
        out_specs=[pl.BlockSpec((sb * t, 2 * w), lambda b, p: (b, p)), st_spec],
        out_shape=[jax.ShapeDtypeStruct((nb * t, C_WIDTH), BF16),
                   jax.ShapeDtypeStruct((nb, npair, 2 * C_DK, C_DV), F32)],
        scratch_shapes=_glr_scratch(t, c, nh, w) if nchunks > 1 else [],
        compiler_params=_cparams(("parallel", "parallel")),
        name="gla",
    )(*args)
    return o, s.reshape(nb, C_HEADS, C_DK, C_DV)


SWA_CB = 8
Q_SCALE = B_SCALE * 1.4426950408889634


def _rope_tables(pos):
    half = B_HD // 2
    inv = ROPE_THETA ** (-jnp.arange(half, dtype=F32) / half)
    ang = pos.astype(F32)[:, None] * inv[None, :]
    cos = jnp.cos(ang)
    sin = jnp.sin(ang)
    cos_t = jnp.tile(cos, (1, 2 * LANES // B_HD))
    sin_t = jnp.tile(jnp.concatenate([-sin, sin], axis=-1), (1, LANES // B_HD))
    return cos_t, sin_t


def _rope(x, cos_t, sin_t):
    reps = x.shape[1] // LANES
    if reps > 1:
        cos_t = jnp.concatenate([cos_t] * reps, axis=1)
        sin_t = jnp.concatenate([sin_t] * reps, axis=1)
    return x * cos_t + _swap_halves(x) * sin_t


def _swap_halves(x):
    n = x.shape[1]
    half = B_HD // 2
    lane = lax.broadcasted_iota(jnp.int32, (1, n), 1)
    first = (lane & (B_HD - 1)) < half
    return jnp.where(first, pltpu.roll(x, n - half, axis=1), pltpu.roll(x, half, axis=1))


def _attend(qs, kb, vb, sink_row, valid):
    return _softmax_pv(_dot_nt(kb, qs), vb, sink_row, valid)


def _softmax_pv(s, vb, sink_row, valid):
    if valid is not None:
        s = jnp.where(valid, s, NEG_BIG)
    sink2 = sink_row * LOG2E
    m = jnp.maximum(jnp.max(s, axis=0, keepdims=True), sink2)
    p = jnp.exp2(s - m)
    den = jnp.sum(p, axis=0, keepdims=True) + jnp.exp2(sink2 - m)
    o_t = _dot_tn(vb, p.astype(BF16)) * (1.0 / den)
    return o_t.T


def _sink_row(sink_ref, base, t):
    lanes = lax.broadcasted_iota(jnp.int32, (1, B_GROUP * t), 1)
    row = jnp.zeros((1, B_GROUP * t), F32)
    for g in range(B_GROUP):
        row = jnp.where((lanes >= g * t) & (lanes < (g + 1) * t), sink_ref[base + g], row)
    return row


def _swa_prompt_kernel(sink_ref, q_ref, k_ref, v_ref, cq_ref, sq_ref, ck_ref, sk_ref,
                       o_ref, kr_ref, vr_ref, kro_ref, vbo_ref, *, t, layer):
    khp = pl.program_id(1)
    step = pl.program_id(2)

    @pl.when(step == 0)
    def _():
        kr = _rope(k_ref[...], ck_ref[...], sk_ref[...])
        kro_ref[...] = kr.astype(BF16)
        vbo_ref[...] = v_ref[...].astype(BF16)
        kr_ref[0] = kr[t - WINDOW:, :]
        vr_ref[0] = v_ref[t - WINDOW:, :]

    band = (WINDOW_CHUNKS + 1) * CHUNK
    sink_rows = [_sink_row(sink_ref, layer * B_HEADS + (khp * 2 + kv) * B_GROUP, CHUNK)
                 for kv in range(2)]
    for ci in range(SWA_CB):
        cidx = step * SWA_CB + ci
        rows = slice(ci * CHUNK, (ci + 1) * CHUNK)
        s0 = pl.multiple_of(jnp.maximum(cidx - WINDOW_CHUNKS, 0) * CHUNK, CHUNK)
        qr = _rope(q_ref[rows, :], cq_ref[rows, :], sq_ref[rows, :]).astype(BF16)
        kband = kro_ref[pl.ds(s0, band), :]
        vband = vbo_ref[pl.ds(s0, band), :]
        key_pos = s0 + lax.broadcasted_iota(jnp.int32, (band, 1), 0)
        valid = key_pos < (cidx + 1) * CHUNK
        for kv in range(2):
            base = kv * B_GROUP * B_HD
            qs = jnp.concatenate(
                [qr[:, base + g * B_HD:base + (g + 1) * B_HD] for g in range(B_GROUP)], axis=0)
            o = _attend(qs, kband[:, kv * B_HD:(kv + 1) * B_HD],
                        vband[:, kv * B_HD:(kv + 1) * B_HD], sink_rows[kv], valid)
            for g in range(0, B_GROUP, 2):
                o_ref[rows, base + g * B_HD:base + (g + 2) * B_HD] = jnp.concatenate(
                    [o[g * CHUNK:(g + 1) * CHUNK], o[(g + 1) * CHUNK:(g + 2) * CHUNK]],
                    axis=1).astype(o_ref.dtype)


def _swa_prompt(proj, sinks, layer, nb, t):
    rb = SWA_CB * CHUNK
    nc = t // rb
    cos_t, sin_t = _rope_tables(jnp.arange(t))
    qw = 2 * B_GROUP * B_HD
    qoff = COL_B_Q * LANES // qw
    kernel = functools.partial(_swa_prompt_kernel, t=t, layer=layer)
    return pl.pallas_call(
        kernel,
        grid=(nb, B_KV_HEADS // 2, nc),
        in_specs=[pl.BlockSpec(memory_space=pltpu.SMEM),
                  pl.BlockSpec((rb, qw), lambda b, p, c: (b * nc + c, qoff + p)),
                  pl.BlockSpec((t, LANES), lambda b, p, c: (b, COL_B_K + p)),
                  pl.BlockSpec((t, LANES), lambda b, p, c: (b, COL_B_V + p)),
                  pl.BlockSpec((rb, LANES), lambda b, p, c: (c, 0)),
                  pl.BlockSpec((rb, LANES), lambda b, p, c: (c, 0)),
                  pl.BlockSpec((t, LANES), lambda b, p, c: (0, 0)),
                  pl.BlockSpec((t, LANES), lambda b, p, c: (0, 0))],
        out_specs=[pl.BlockSpec((rb, qw), lambda b, p, c: (b * nc + c, p)),
                   pl.BlockSpec((1, WINDOW, LANES), lambda b, p, c: (b, 0, p)),
                   pl.BlockSpec((1, WINDOW, LANES), lambda b, p, c: (b, 0, p))],
        out_shape=[jax.ShapeDtypeStruct((nb * t, B_WIDTH), BF16),
                   jax.ShapeDtypeStruct((nb, WINDOW, B_KV_HEADS * B_HD), F32),
                   jax.ShapeDtypeStruct((nb, WINDOW, B_KV_HEADS * B_HD), F32)],
        scratch_shapes=[pltpu.VMEM((t, LANES), BF16), pltpu.VMEM((t, LANES), BF16)],
        compiler_params=_cparams(("parallel", "parallel", "arbitrary")),
        name="swa_prompt",
    )(sinks.reshape(-1), proj, proj, proj, cos_t * Q_SCALE, sin_t * Q_SCALE, cos_t, sin_t)


def _swa_sample_kernel(sink_ref, q_ref, k_ref, v_ref, ckc_ref, cvc_ref, cos_ref, sin_ref,
                       o_ref, kr_ref, vr_ref, *, t, layer):
    cos_t = cos_ref[...]
    sin_t = sin_ref[...]
    kr = _rope(k_ref[...], cos_t, sin_t)
    v = v_ref[...]
    kr_ref[...] = kr
    vr_ref[...] = v
    qr = (_rope(q_ref[...], cos_t, sin_t) * Q_SCALE).astype(BF16)
    ka = jnp.concatenate([ckc_ref[0], kr], axis=0).astype(BF16)
    va = jnp.concatenate([cvc_ref[0], v], axis=0).astype(BF16)
    for kh in range(B_KV_HEADS):
        base = kh * B_GROUP * B_HD
        qs = jnp.concatenate(
            [qr[:, base + g * B_HD:base + (g + 1) * B_HD] for g in range(B_GROUP)], axis=0)
        sink_row = _sink_row(sink_ref, layer * B_HEADS + kh * B_GROUP, t)
        o = _attend(qs, ka[:, kh * B_HD:(kh + 1) * B_HD], va[:, kh * B_HD:(kh + 1) * B_HD],
                    sink_row, None)
        for g in range(0, B_GROUP, 2):
            o_ref[:, base + g * B_HD:base + (g + 2) * B_HD] = jnp.concatenate(
                [o[g * t:(g + 1) * t], o[(g + 1) * t:(g + 2) * t]], axis=1).astype(o_ref.dtype)


def _swa_sample(proj, cache_k, cache_v, sinks, layer, nb, t):
    cos_t, sin_t = _rope_tables(PAST_LEN + jnp.arange(t))
    kvw = B_KV_HEADS * B_HD
    kernel = functools.partial(_swa_sample_kernel, t=t, layer=layer)
    return pl.pallas_call(
        kernel,
        grid=(nb,),
        in_specs=[pl.BlockSpec(memory_space=pltpu.SMEM),
                  pl.BlockSpec((t, B_WIDTH), lambda b: (b, COL_B_Q * LANES // B_WIDTH)),
                  pl.BlockSpec((t, kvw), lambda b: (b, COL_B_K * LANES // kvw)),
                  pl.BlockSpec((t, kvw), lambda b: (b, COL_B_V * LANES // kvw)),
                  pl.BlockSpec((1, WINDOW, kvw), lambda b: (b, 0, 0)),
                  pl.BlockSpec((1, WINDOW, kvw), lambda b: (b, 0, 0)),
                  pl.BlockSpec((t, LANES), lambda b: (0, 0)),
                  pl.BlockSpec((t, LANES), lambda b: (0, 0))],
        out_specs=[pl.BlockSpec((t, B_WIDTH), lambda b: (b, 0)),
                   pl.BlockSpec((t, kvw), lambda b: (b, 0)),
                   pl.BlockSpec((t, kvw), lambda b: (b, 0))],
        out_shape=[jax.ShapeDtypeStruct((nb * t, B_WIDTH), BF16),
                   jax.ShapeDtypeStruct((nb * t, kvw), F32),
                   jax.ShapeDtypeStruct((nb * t, kvw), F32)],
        compiler_params=_cparams(("parallel",)),
        name="swa_sample",
    )(sinks.reshape(-1), proj, proj, proj,
      cache_k.reshape(nb, WINDOW, kvw), cache_v.reshape(nb, WINDOW, kvw), cos_t, sin_t)


def _layer_sample(x, w, layer, nb, t, cache, s_a, s_c, ff_pad):
    m, d = x.shape
    h = _rmsnorm(x, w["norm_mix"][layer], BF16, m)
    proj, wb_in = _mm_in_cast(h, w["w_in"], layer, IN_COLS_PAD, MM_TN_IN)
    o_a, s_a_new = _hgrn(proj, w["lb_logits"], w["hgrn_norm"][layer], s_a, layer, nb, t, t)
    o_b, k_rows, v_rows = _swa_sample(proj, cache[0], cache[1], w["sinks"], layer, nb, t)
    o_c, s_c_new = _gla(proj, w["w_a2"][layer], w["b_a"][layer], w["gla_norm"][layer],
                        s_c, nb, t, t)
    x, wb_out = _mm_out_cast(o_a, o_b, o_c, w["w_out"], x, layer, min(d, 512))
    h = _rmsnorm(x, w["norm_ffn"][layer], BF16, m)
    mid, wb_gate, wb_up = _gate_up_cast(h, w["w_gate_up"], layer, ff_pad)
    x, wb_down = _down_cast(mid, w["w_down"], x, layer)
    outs = (k_rows.reshape(nb, t, B_KV_HEADS, B_HD), v_rows.reshape(nb, t, B_KV_HEADS, B_HD),
            s_a_new, s_c_new)
    return x, outs, dict(w_in=wb_in, w_out=wb_out, w_gate=wb_gate, w_up=wb_up, w_down=wb_down)


def _layer_prompt(x, normed, w, wb, layer, nb, t):
    m, d = x.shape
    tm = min(m, 1024)
    if normed is None:
        h = _rmsnorm(x, w["norm_mix"][layer], BF16, min(m, 256))
        proj = _matmul(h, wb["w_in"], tm, MM_TN_IN, F32)
    else:
        proj = _matmul(normed[0], wb["w_in"], tm, MM_TN_IN, F32, normed[1])
    o_a, s_a_new = _hgrn(proj, w["lb_logits"], w["hgrn_norm"][layer], None, layer, nb, t, CHUNK)
    o_b, k_rows, v_rows = _swa_prompt(proj, w["sinks"], layer, nb, t)
    o_c, s_c_new = _gla(proj, w["w_a2"][layer], w["b_a"][layer], w["gla_norm"][layer],
                        None, nb, t, CHUNK)
    x, xg, ssq = _matmul_out(o_a, o_b, o_c, wb["w_out"], x, w["norm_ffn"][layer], tm, min(d, 1024))
    mid = _matmul_gate_up(xg, ssq, wb["w_gate"], wb["w_up"], tm, 512)
    tk = wb["w_down"].shape[0] // 4
    if layer + 1 < DEPTH:
        x, xg, ssq = _matmul_down(mid, wb["w_down"], x, tm, min(d, 1024), tk,
                                  w["norm_mix"][layer + 1])
        normed = (xg, ssq)
    else:
        x = _matmul_down(mid, wb["w_down"], x, tm, min(d, 1024), tk)
        normed = None
    outs = (k_rows.reshape(nb, WINDOW, B_KV_HEADS, B_HD),
            v_rows.reshape(nb, WINDOW, B_KV_HEADS, B_HD), s_a_new, s_c_new)
    return x, normed, outs


def kernel(x_prompt, x_sample, cache_k_swa, cache_v_swa, state_hgrn, state_gla, norm_mix, w_in,
           hgrn_lb_logits, hgrn_norm, swa_sinks, gla_w_alpha2, gla_b_alpha, gla_norm, w_out,
           norm_ffn, w_gate_up, w_down, norm_final):
    n_p, t_p, d = x_prompt.shape
    n_s, t_s, _ = x_sample.shape
    d_ff = w_down.shape[1]
    ff_pad = -(-d_ff // 1024) * 1024
    w = dict(
        norm_mix=norm_mix, norm_ffn=norm_ffn, hgrn_norm=hgrn_norm, gla_norm=gla_norm,
        lb_logits=hgrn_lb_logits, sinks=swa_sinks, b_a=gla_b_alpha,
        w_in=w_in, w_out=w_out, w_gate_up=w_gate_up, w_down=w_down,
        w_a2=jnp.pad(gla_w_alpha2, ((0, 0), (0, LANES - GLA_RANK), (0, 0))).astype(BF16),
    )
    xp = x_prompt.reshape(n_p * t_p, d)
    xs = x_sample.reshape(n_s * t_s, d)
    outs_p, outs_s = [], []
    normed = None
    for layer in range(DEPTH):
        xs, rest, wb = _layer_sample(xs, w, layer, n_s, t_s,
                                     (cache_k_swa[layer], cache_v_swa[layer]),
                                     state_hgrn[layer], state_gla[layer], ff_pad)
        outs_s.append(rest)
        xp, normed, rest = _layer_prompt(xp, normed, w, wb, layer, n_p, t_p)
        outs_p.append(rest)
    y_p = _rmsnorm(xp, norm_final, F32, min(xp.shape[0], 256)).reshape(n_p, t_p, d)
    y_s = _rmsnorm(xs, norm_final, F32, min(xs.shape[0], 256)).reshape(n_s, t_s, d)
    stack = lambda outs, i: jnp.stack([o[i] for o in outs])
    return (y_p, y_s,
            stack(outs_p, 0), stack(outs_p, 1), stack(outs_p, 2), stack(outs_p, 3),
            stack(outs_s, 0), stack(outs_s, 1), stack(outs_s, 2), stack(outs_s, 3))
```
